```python
import math
import jax, jax.numpy as jnp
from jax import lax
import numpy as np

D_MODEL = 1024
BATCH = 4
SEQ = 4096
DEPTH = 4
DEC_BATCH = 128
DEC_SEQ = 8
PAST_LEN = 8192
PAGE_SIZE = 128

N_MEM = 256
GROUP_W = D_MODEL // 4
MIX_W = 4 * GROUP_W
CONV_W = 3
GLA_HEADS = 4
GLA_DK = GROUP_W // GLA_HEADS
GLA_DV = GROUP_W // GLA_HEADS
GLA_GATE_RANK = 16
GLA_GATE_NORM = 16.0
GLA_CHUNK = 16
SWA_HEADS = 4
SWA_KV_HEADS = 2
SWA_HD = GROUP_W // SWA_HEADS
SWA_GROUP = SWA_HEADS // SWA_KV_HEADS
WINDOW = 128
W_BUF = min(WINDOW, PAST_LEN)
SWA_BLOCK = 128
ROPE_THETA = 10000.0
MEM_HEADS = 4
MEM_HD = GROUP_W // MEM_HEADS
EPS = 1e-6
IN_SIZES = (GROUP_W, GROUP_W, GROUP_W, GROUP_W,
            GROUP_W, GROUP_W, GROUP_W, GLA_GATE_RANK, GROUP_W,
            GROUP_W, SWA_KV_HEADS * SWA_HD, SWA_KV_HEADS * SWA_HD, GROUP_W,
            GROUP_W, GROUP_W)
IN_WIDTH = sum(IN_SIZES)

kernel_name = 'hymba_style_conv_gla_swa_memory_decoder_step'


def rmsnorm(x, g):
    xf = x.astype(jnp.float32)
    y = xf * lax.rsqrt(jnp.mean(xf * xf, axis=-1, keepdims=True) + EPS)
    return (y * g.astype(jnp.float32)).astype(x.dtype)


def split_cols(p):
    out = []
    off = 0
    for s in IN_SIZES:
        out.append(p[..., off:off + s])
        off += s
    return out


def rope(x, pos):
    half = x.shape[-1] // 2
    inv = jnp.power(ROPE_THETA, -jnp.arange(half, dtype=jnp.float32) / half)
    ang = pos.astype(jnp.float32)[:, None] * inv[None, :]
    cos = jnp.cos(ang)[None, :, None, :]
    sin = jnp.sin(ang)[None, :, None, :]
    xf = x.astype(jnp.float32)
    x1, x2 = xf[..., :half], xf[..., half:]
    return jnp.concatenate([x1 * cos - x2 * sin, x2 * cos + x1 * sin], axis=-1).astype(x.dtype)


def short_conv(u, buf, w):
    L = u.shape[1]
    up = jnp.concatenate([buf.astype(u.dtype), u], axis=1)
    y = w[0] * up[:, 0:L]
    for tap in range(1, CONV_W):
        y = y + w[tap] * up[:, tap:tap + L]
    return y, up[:, -(CONV_W - 1):]


def gla_recurrence(q, k, v, log_a, s0):
    B, L, H, DK = q.shape
    DV = v.shape[-1]
    f32 = jnp.float32
    C = min(GLA_CHUNK, L)
    pad = (-L) % C
    qf = q.astype(f32) * (DK ** -0.5)
    kf = k.astype(f32)
    vf = v.astype(f32)
    la = log_a.astype(f32)
    if pad:
        pw = ((0, 0), (0, pad), (0, 0), (0, 0))
        qf, kf, vf, la = [jnp.pad(t, pw) for t in (qf, kf, vf, la)]
    N = (L + pad) // C
    qf = qf.reshape(B, N, C, H, DK)
    kf = kf.reshape(B, N, C, H, DK)
    vf = vf.reshape(B, N, C, H, DV)
    b = jnp.cumsum(la.reshape(B, N, C, H, DK), axis=2)
    causal = jnp.tril(jnp.ones((C, C), dtype=bool))[None, None, :, :, None, None]
    dec = jnp.exp(jnp.where(causal, b[:, :, :, None] - b[:, :, None, :], -jnp.inf))
    attn = jnp.einsum('bnihd,bnjhd,bnijhd->bnhij', qf, kf, dec)
    o_intra = jnp.einsum('bnhij,bnjhv->bnihv', attn, vf)
    b_last = b[:, :, -1]
    k_dec = kf * jnp.exp(b_last[:, :, None] - b)
    kv = jnp.einsum('bnjhd,bnjhv->bnhdv', k_dec, vf)
    a_chunk = jnp.exp(b_last)

    def step(S, inp):
        a_n, kv_n = inp
        return a_n[..., None] * S + kv_n, S

    S_fin, S_prev = lax.scan(step, s0.astype(f32), (jnp.moveaxis(a_chunk, 1, 0), jnp.moveaxis(kv, 1, 0)))
    S_prev = jnp.moveaxis(S_prev, 0, 1)
    o_inter = jnp.einsum('bnihd,bnhdv->bnihv', qf * jnp.exp(b), S_prev)
    o = (o_intra + o_inter).reshape(B, N * C, H, DV)[:, :L]
    return o, S_fin


def swa_sink_attention(q, k, v, kbuf, vbuf, pos0, sinks):
    B, L, Hq, hd = q.shape
    W = kbuf.shape[1]
    k_all = jnp.concatenate([kbuf.astype(k.dtype), k], axis=1)
    v_all = jnp.concatenate([vbuf.astype(v.dtype), v], axis=1)
    QB = min(SWA_BLOCK, L)
    nb = L // QB
    idx = jnp.arange(nb)[:, None] * QB + jnp.arange(W + QB)[None, :]
    kb = k_all[:, idx]
    vb = v_all[:, idx]
    qb = q.reshape(B, nb, QB, SWA_KV_HEADS, SWA_GROUP, hd)
    s = jnp.einsum('bnqkgd,bnskd->bnkgqs', qb, kb).astype(jnp.float32) * (hd ** -0.5)
    qpos = pos0 + jnp.arange(L).reshape(nb, QB)
    kpos = pos0 - W + idx
    valid = ((kpos[:, None, :] <= qpos[:, :, None]) & (kpos[:, None, :] > qpos[:, :, None] - WINDOW)
             & (kpos[:, None, :] >= 0))
    s = jnp.where(valid[None, :, None, None], s, -jnp.inf)
    sink = jnp.broadcast_to(sinks.astype(jnp.float32).reshape(1, 1, SWA_KV_HEADS, SWA_GROUP, 1, 1),
                            s.shape[:-1] + (1,))
    p = jax.nn.softmax(jnp.concatenate([s, sink], axis=-1), axis=-1)[..., :-1]
    o = jnp.einsum('bnkgqs,bnskd->bnqkgd', p.astype(vb.dtype), vb).reshape(B, L, Hq, hd)
    return o, k_all[:, -W:], v_all[:, -W:]


def memory_attention(q, mk, mv):
    hd = q.shape[-1]
    s = jnp.einsum('blhd,bmhd->bhlm', q, mk.astype(q.dtype)).astype(jnp.float32) * (hd ** -0.5)
    p = jax.nn.softmax(s, axis=-1)
    return jnp.einsum('bhlm,bmhd->blhd', p.astype(q.dtype), mv.astype(q.dtype))


def layer(x, pos0, conv_buf, gla_s, kbuf, vbuf, mem_k, mem_v,
          g_pre, g_post, w_in, conv_w, gla_wg, gla_bg, gla_g, sinks, w_out):
    B, L, _ = x.shape
    h = rmsnorm(x, g_pre)
    (c_x, c_b, c_c, c_z, g_q, g_k, g_v, g_lr, g_z,
     s_q, s_k, s_v, s_z, m_q, m_z) = split_cols(h @ w_in)
    cy, new_conv = short_conv(c_c * c_x, conv_buf, conv_w)
    a_out = c_b * cy * jax.nn.silu(c_z)
    log_a = jax.nn.log_sigmoid((g_lr @ gla_wg + gla_bg).astype(jnp.float32)) / GLA_GATE_NORM
    o_b, new_gla = gla_recurrence(g_q.reshape(B, L, GLA_HEADS, GLA_DK), g_k.reshape(B, L, GLA_HEADS, GLA_DK),
                                  g_v.reshape(B, L, GLA_HEADS, GLA_DV), log_a.reshape(B, L, GLA_HEADS, GLA_DK), gla_s)
    o_b = rmsnorm(o_b.astype(x.dtype), gla_g.reshape(GLA_HEADS, GLA_DV))
    b_out = o_b.reshape(B, L, GROUP_W) * jax.nn.silu(g_z)
    pos = pos0 + jnp.arange(L, dtype=jnp.int32)
    q_c = rope(s_q.reshape(B, L, SWA_HEADS, SWA_HD), pos)
    k_c = rope(s_k.reshape(B, L, SWA_KV_HEADS, SWA_HD), pos)
    o_c, new_k, new_v = swa_sink_attention(q_c, k_c, s_v.reshape(B, L, SWA_KV_HEADS, SWA_HD), kbuf, vbuf, pos0, sinks)
    c_out = o_c.reshape(B, L, GROUP_W) * jax.nn.silu(s_z)
    o_d = memory_attention(m_q.reshape(B, L, MEM_HEADS, MEM_HD), mem_k, mem_v)
    d_out = o_d.reshape(B, L, GROUP_W) * jax.nn.silu(m_z)
    mix = jnp.concatenate([a_out, b_out, c_out, d_out], axis=-1) @ w_out
    y = x + rmsnorm(mix, g_post)
    return y, new_conv, new_gla.astype(gla_s.dtype), new_k, new_v


def setup_inputs(seed: int = 0) -> dict:
    key = jax.random.key(seed)
    ks = jax.random.split(key, 20)
    f32 = jnp.float32
    nrm = lambda k, shp, sc: jax.random.normal(k, shp, f32) * sc
    return {
        'x_prompt': nrm(ks[0], (BATCH, SEQ, D_MODEL), 1.0),
        'x_sample': nrm(ks[1], (DEC_BATCH, DEC_SEQ, D_MODEL), 1.0),
        'state_conv': nrm(ks[2], (DEPTH, DEC_BATCH, CONV_W - 1, GROUP_W), 1.0),
        'state_gla': nrm(ks[3], (DEPTH, DEC_BATCH, GLA_HEADS, GLA_DK, GLA_DV), 0.3),
        'cache_swa_k': nrm(ks[4], (DEPTH, DEC_BATCH, W_BUF, SWA_KV_HEADS, SWA_HD), 1.0),
        'cache_swa_v': nrm(ks[5], (DEPTH, DEC_BATCH, W_BUF, SWA_KV_HEADS, SWA_HD), 1.0),
        'cache_mem_k': nrm(ks[6], (DEPTH, DEC_BATCH, N_MEM, MEM_HEADS, MEM_HD), 1.0),
        'cache_mem_v': nrm(ks[7], (DEPTH, DEC_BATCH, N_MEM, MEM_HEADS, MEM_HD), 1.0),
        'mem_prompt': nrm(ks[8], (BATCH, N_MEM, D_MODEL), 1.0),
        'norm_pre': 1.0 + nrm(ks[9], (DEPTH, D_MODEL), 0.02),
        'norm_post': 1.0 + nrm(ks[10], (DEPTH, D_MODEL), 0.02),
        'w_in': nrm(ks[11], (DEPTH, D_MODEL, IN_WIDTH), D_MODEL ** -0.5),
        'conv_w': nrm(ks[12], (DEPTH, CONV_W, GROUP_W), CONV_W ** -0.5),
        'gla_w_gate': nrm(ks[13], (DEPTH, GLA_GATE_RANK, GROUP_W), GLA_GATE_RANK ** -0.5),
        'gla_b_gate': nrm(ks[14], (DEPTH, GROUP_W), 0.1),
        'gla_norm': 1.0 + nrm(ks[15], (DEPTH, GROUP_W), 0.02),
        'swa_sinks': nrm(ks[16], (DEPTH, SWA_HEADS), 0.5),
        'w_mem_kv': nrm(ks[17], (DEPTH, D_MODEL, 2 * GROUP_W), D_MODEL ** -0.5),
        'w_out': nrm(ks[18], (DEPTH, MIX_W, D_MODEL), MIX_W ** -0.5),
    }


def reference(x_prompt, x_sample, state_conv, state_gla, cache_swa_k, cache_swa_v, cache_mem_k, cache_mem_v,
              mem_prompt, norm_pre, norm_post, w_in, conv_w, gla_w_gate, gla_b_gate, gla_norm, swa_sinks,
              w_mem_kv, w_out):
    hp = x_prompt
    hs = x_sample
    conv_p, gla_p, kp_l, vp_l, mk_p, mv_p = [], [], [], [], [], []
    conv_s, gla_s, ks_l, vs_l = [], [], [], []
    dt = x_prompt.dtype
    for l in range(DEPTH):
        ws = (norm_pre[l], norm_post[l], w_in[l], conv_w[l], gla_w_gate[l], gla_b_gate[l],
              gla_norm[l], swa_sinks[l], w_out[l])
        mkv = mem_prompt @ w_mem_kv[l]
        mk = mkv[..., :GROUP_W].reshape(BATCH, N_MEM, MEM_HEADS, MEM_HD)
        mv = mkv[..., GROUP_W:].reshape(BATCH, N_MEM, MEM_HEADS, MEM_HD)
        hp, c1, g1, k1, v1 = layer(
            hp, 0,
            jnp.zeros((BATCH, CONV_W - 1, GROUP_W), dt),
            jnp.zeros((BATCH, GLA_HEADS, GLA_DK, GLA_DV), jnp.float32),
            jnp.zeros((BATCH, W_BUF, SWA_KV_HEADS, SWA_HD), dt),
            jnp.zeros((BATCH, W_BUF, SWA_KV_HEADS, SWA_HD), dt),
            mk, mv, *ws)
        hs, c2, g2, k2, v2 = layer(
            hs, PAST_LEN, state_conv[l], state_gla[l], cache_swa_k[l], cache_swa_v[l],
            cache_mem_k[l], cache_mem_v[l], *ws)
        conv_p.append(c1); gla_p.append(g1); kp_l.append(k1); vp_l.append(v1); mk_p.append(mk); mv_p.append(mv)
        conv_s.append(c2); gla_s.append(g2); ks_l.append(k2); vs_l.append(v2)
    new_conv_p = jnp.stack(conv_p)
    new_gla_p = jnp.stack(gla_p)
    new_swa_k_p = jnp.stack(kp_l)
    new_swa_v_p = jnp.stack(vp_l)
    new_mem_k_p = jnp.stack(mk_p)
    new_mem_v_p = jnp.stack(mv_p)
    new_conv_s = jnp.stack(conv_s)
    new_gla_s = jnp.stack(gla_s)
    new_swa_k_s = jnp.stack(ks_l)
    new_swa_v_s = jnp.stack(vs_l)
    return (hp, hs, new_conv_p, new_gla_p, new_swa_k_p, new_swa_v_p, new_mem_k_p, new_mem_v_p,
            new_conv_s, new_gla_s, new_swa_k_s, new_swa_v_s)
```

```python
import functools
import math

import jax
import jax.numpy as jnp
import numpy as np
from jax import lax
from jax.experimental import pallas as pl
from jax.experimental.pallas import tpu as pltpu

f32 = jnp.float32
bf16 = jnp.bfloat16

D_MODEL = 1024
GROUP_W = 256
HEAD_D = 64
N_HEADS = 4
SWA_KV_W = 128
N_MEM = 256
WINDOW = 128
CONV_W = 3
GATE_RANK = 16
GATE_NORM = 16.0
GLA_BLOCK = 16
ROPE_THETA = 10000.0
EPS = 1e-6
NEG = -1e30
QK_SCALE = HEAD_D ** -0.5

LANES = 128
SUBLANES = 8
VMEM_LIMIT_BYTES = 56 * 1024 * 1024

C_CX, C_CB, C_CC, C_CZ = 0, 256, 512, 768
C_GQ, C_GK, C_GV, C_GZ = 1024, 1280, 1536, 1792
C_SQA, C_SQB, C_SK, C_SV, C_SZ = 2048, 2176, 2304, 2432, 2560
C_MQ, C_MZ = 2816, 3072
C_LR = 3328
NP = 3456
_O_GLR, _O_GZ, _O_SQ, _O_SK, _O_SV, _O_SZ, _O_MQ, _O_MZ, _O_END = 1792, 1808, 2064, 2320, 2448, 2576, 2832, 3088, 3344
_HEAD_PERM = (0, 2, 1, 3)

SUB = 128
TL = 512


def _dot(a, b):
    return jnp.dot(a, b, preferred_element_type=f32)


def _dot_nt(a, b):
    return lax.dot_general(a, b, (((1,), (1,)), ((), ())), preferred_element_type=f32)


def _dot_tn(a, b):
    return lax.dot_general(a, b, (((0,), (0,)), ((), ())), preferred_element_type=f32)


def _rmsnorm(x, g):
    return x * lax.rsqrt(jnp.mean(x * x, axis=-1, keepdims=True) + EPS) * g


def _silu(z):
    return z / (1.0 + jnp.exp(-z))


def _log_sigmoid(x):
    return jnp.minimum(x, 0.0) - jnp.log1p(jnp.exp(-jnp.abs(x)))


def _head_sum(x, j_ref):
    hi = x.astype(bf16)
    lo = (x - hi.astype(f32)).astype(bf16)
    return _dot(hi, j_ref[...]) + _dot(lo, j_ref[...])


def _rope(x, cos, sin_signed):
    lane = lax.broadcasted_iota(jnp.int32, x.shape, 1)
    swapped = jnp.where((lane % HEAD_D) < HEAD_D // 2,
                        pltpu.roll(x, LANES - HEAD_D // 2, 1), pltpu.roll(x, HEAD_D // 2, 1))
    return x * cos + swapped * sin_signed


def _stack_heads(x):
    lane_head = lax.broadcasted_iota(jnp.int32, x.shape, 1) // HEAD_D
    return jnp.concatenate([jnp.where(lane_head == h, x, 0.0) for h in range(N_HEADS)], axis=0)


def _unstack_heads(o, n):
    lane_head = lax.broadcasted_iota(jnp.int32, (n, GROUP_W), 1) // HEAD_D
    out = o[0:n]
    for h in range(1, N_HEADS):
        out = jnp.where(lane_head == h, o[h * n:(h + 1) * n], out)
    return out


def _stack_swa_q(qa, qb):
    low = lax.broadcasted_iota(jnp.int32, qa.shape, 1) < HEAD_D
    return jnp.concatenate([jnp.where(low, qa, 0.0), jnp.where(low, 0.0, qa),
                            jnp.where(low, qb, 0.0), jnp.where(low, 0.0, qb)], axis=0)


def _unstack_swa(o, n):
    low = lax.broadcasted_iota(jnp.int32, (n, SWA_KV_W), 1) < HEAD_D
    ca = jnp.where(low, o[0:n], o[n:2 * n])
    cb = jnp.where(low, o[2 * n:3 * n], o[3 * n:4 * n])
    return jnp.concatenate([ca, cb], axis=1)


def _sink_column(sinks_ref, n):
    return jnp.concatenate([jnp.full((n, 1), sinks_ref[h], f32) for h in _HEAD_PERM], axis=0)


def _block_cumsum(la, block):
    row = lax.broadcasted_iota(jnp.int32, la.shape, 0) % block
    b = la
    s = 1
    while s < block:
        b = b + jnp.where(row >= s, pltpu.roll(b, s, 0), 0.0)
        s *= 2
    return b


def _gate_log_decay(p_lr, wg_ref, bg_ref):
    pre = _dot(p_lr.astype(bf16), wg_ref[...]) + bg_ref[...]
    return _log_sigmoid(pre) * (1.0 / GATE_NORM)


def _gla_norm_gate(o, gz, gnorm_ref, j_ref):
    ms = _head_sum(o * o, j_ref) * (1.0 / HEAD_D)
    return o * lax.rsqrt(ms + EPS) * gnorm_ref[...] * _silu(gz)


def _prompt_layer_kernel(sinks_ref, x_ref, memp_ref, cos_ref, sin_ref, gpre_ref, gpost_ref, win_ref, wout_ref,
                         wmem_ref, convw_ref, wg_ref, bg_ref, gnorm_ref, j_ref,
                         y_ref, conv_out, gla_out, swak_out, swav_out, mk_out, mv_out,
                         p_scr, mix_scr, ubuf, kpad, bpad, vpad, st_scr, kprev, vprev, mkb, mvb):
    t = pl.program_id(1)
    nt = pl.num_programs(1)
    nsub = TL // SUB

    @pl.when(t == 0)
    def _init():
        ubuf[0:SUBLANES, :] = jnp.zeros((SUBLANES, GROUP_W), f32)
        kpad[0:GLA_BLOCK, :] = jnp.zeros((GLA_BLOCK, GROUP_W), f32)
        bpad[0:GLA_BLOCK, :] = jnp.zeros((GLA_BLOCK, GROUP_W), f32)
        vpad[0:GLA_BLOCK, :] = jnp.zeros((GLA_BLOCK, GROUP_W), f32)
        st_scr[...] = jnp.zeros((GROUP_W, GROUP_W), f32)
        kprev[...] = jnp.zeros((SUB, SWA_KV_W), f32)
        vprev[...] = jnp.zeros((SUB, SWA_KV_W), f32)
        mkv = _dot(memp_ref[0].astype(bf16), wmem_ref[...])
        mk_out[0] = mkv[:, 0:GROUP_W]
        mv_out[0] = mkv[:, GROUP_W:2 * GROUP_W]
        mkb[...] = mkv[:, 0:GROUP_W].astype(bf16)
        mvb[...] = mkv[:, GROUP_W:2 * GROUP_W].astype(bf16)

    x = x_ref[0]
    h = _rmsnorm(x, gpre_ref[...]).astype(bf16)
    p_scr[...] = _dot(h, win_ref[...])

    row_in_block = lax.broadcasted_iota(jnp.int32, (SUB, GROUP_W), 0) % GLA_BLOCK
    blocks = SUB // GLA_BLOCK

    def sub_tile(j, carry):
        r0 = pl.multiple_of(j * SUB, SUB)
        rows = pl.ds(r0, SUB)

        u = p_scr[rows, C_CC:C_CC + GROUP_W] * p_scr[rows, C_CX:C_CX + GROUP_W]
        ubuf[SUBLANES:SUBLANES + SUB, :] = u
        cy = (convw_ref[0:1, :] * ubuf[pl.ds(SUBLANES - 2, SUB), :]
              + convw_ref[1:2, :] * ubuf[pl.ds(SUBLANES - 1, SUB), :]
              + convw_ref[2:3, :] * u)
        ubuf[0:SUBLANES, :] = u[SUB - SUBLANES:SUB, :]
        a_out = p_scr[rows, C_CB:C_CB + GROUP_W] * cy * _silu(p_scr[rows, C_CZ:C_CZ + GROUP_W])
        mix_scr[rows, 0:GROUP_W] = a_out.astype(bf16)

        qs = p_scr[rows, C_GQ:C_GQ + GROUP_W] * QK_SCALE
        k = p_scr[rows, C_GK:C_GK + GROUP_W]
        v = p_scr[rows, C_GV:C_GV + GROUP_W]
        la = _gate_log_decay(p_scr[rows, C_LR:C_LR + LANES], wg_ref, bg_ref)
        b16 = _block_cumsum(la, GLA_BLOCK)
        kpad[GLA_BLOCK:GLA_BLOCK + SUB, :] = k
        bpad[GLA_BLOCK:GLA_BLOCK + SUB, :] = b16
        vpad[GLA_BLOCK:GLA_BLOCK + SUB, :] = v
        o = _dot((qs * k).astype(bf16), j_ref[...]) * v
        for d in range(1, GLA_BLOCK):
            win = pl.ds(GLA_BLOCK - d, SUB)
            dec = jnp.exp(jnp.where(row_in_block >= d, b16 - bpad[win, :], NEG))
            pw = qs * kpad[win, :] * dec
            o = o + _dot(pw.astype(bf16), j_ref[...]) * vpad[win, :]
        tot = [b16[GLA_BLOCK * i + GLA_BLOCK - 1:GLA_BLOCK * i + GLA_BLOCK, :] for i in range(blocks)]
        rest = [tot[i] - b16[GLA_BLOCK * i:GLA_BLOCK * (i + 1), :] for i in range(blocks)]

        def decays(group):
            eq, ek = [], []
            for i in range(blocks):
                first = (i // group) * group
                e = b16[GLA_BLOCK * i:GLA_BLOCK * (i + 1), :]
                for i2 in range(first, i):
                    e = e + tot[i2]
                r = rest[i]
                for i2 in range(i + 1, first + group):
                    r = r + tot[i2]
                eq.append(e)
                ek.append(r)
            return jnp.concatenate(eq, axis=0), jnp.concatenate(ek, axis=0)

        qi = lax.broadcasted_iota(jnp.int32, (N_HEADS * SUB, SUB), 0) % SUB
        kj = lax.broadcasted_iota(jnp.int32, (N_HEADS * SUB, SUB), 1)
        attn = jnp.zeros((N_HEADS * SUB, SUB), f32)
        group = 1
        while group * GLA_BLOCK < SUB:
            m = group * GLA_BLOCK
            eq, ek = decays(group)
            qd = _stack_heads(qs * jnp.exp(eq)).astype(bf16)
            kd = (k * jnp.exp(ek)).astype(bf16)
            s = _dot_nt(qd, kd)
            sel = ((qi // m) == (kj // m) + 1) & (((qi // m) % 2) == 1)
            attn = attn + jnp.where(sel, s, 0.0)
            group *= 2
        o = o + _unstack_heads(_dot(attn.astype(bf16), v.astype(bf16)), SUB)
        eq, ek = decays(blocks)
        st = st_scr[...]
        o = o + _dot_nt((qs * jnp.exp(eq)).astype(bf16), st.astype(bf16))
        total = tot[0]
        for i in range(1, blocks):
            total = total + tot[i]
        upd = _dot_tn(v.astype(bf16), (k * jnp.exp(ek)).astype(bf16))
        same_head = (lax.broadcasted_iota(jnp.int32, (GROUP_W, GROUP_W), 0) // HEAD_D
                     == lax.broadcasted_iota(jnp.int32, (GROUP_W, GROUP_W), 1) // HEAD_D)
        st_scr[...] = st * jnp.exp(total) + jnp.where(same_head, upd, 0.0)
        b_out = _gla_norm_gate(o, p_scr[rows, C_GZ:C_GZ + GROUP_W], gnorm_ref, j_ref)
        mix_scr[rows, GROUP_W:2 * GROUP_W] = b_out.astype(bf16)

        cos = cos_ref[rows, :]
        sin = sin_ref[rows, :]
        qa = _rope(p_scr[rows, C_SQA:C_SQA + LANES], cos, sin)
        qb = _rope(p_scr[rows, C_SQB:C_SQB + LANES], cos, sin)
        kr = _rope(p_scr[rows, C_SK:C_SK + SWA_KV_W], cos, sin)
        vx = p_scr[rows, C_SV:C_SV + SWA_KV_W]
        k_all = jnp.concatenate([kprev[...], kr], axis=0).astype(bf16)
        v_all = jnp.concatenate([vprev[...], vx], axis=0).astype(bf16)
        sc = _dot_nt(_stack_swa_q(qa, qb).astype(bf16), k_all) * QK_SCALE
        qrow = lax.broadcasted_iota(jnp.int32, sc.shape, 0) % SUB
        kcol = lax.broadcasted_iota(jnp.int32, sc.shape, 1)
        first_key = jnp.where((t * nsub + j) == 0, WINDOW, 0)
        valid = (kcol > qrow) & (kcol <= qrow + WINDOW) & (kcol >= first_key)
        sc = jnp.where(valid, sc, NEG)
        sink = _sink_column(sinks_ref, SUB)
        mx = jnp.maximum(jnp.max(sc, axis=1, keepdims=True), sink)
        e = jnp.exp(sc - mx)
        den = jnp.sum(e, axis=1, keepdims=True) + jnp.exp(sink - mx)
        oc = _unstack_swa(_dot(e.astype(bf16), v_all) / den, SUB)
        mix_scr[rows, 2 * GROUP_W:3 * GROUP_W] = (oc * _silu(p_scr[rows, C_SZ:C_SZ + GROUP_W])).astype(bf16)
        kprev[...] = kr
        vprev[...] = vx

        qm = _stack_heads(p_scr[rows, C_MQ:C_MQ + GROUP_W]).astype(bf16)
        sm = _dot_nt(qm, mkb[...]) * QK_SCALE
        mm = jnp.max(sm, axis=1, keepdims=True)
        em = jnp.exp(sm - mm)
        dm = jnp.sum(em, axis=1, keepdims=True)
        od = _unstack_heads(_dot(em.astype(bf16), mvb[...]) / dm, SUB)
        mix_scr[rows, 3 * GROUP_W:4 * GROUP_W] = (od * _silu(p_scr[rows, C_MZ:C_MZ + GROUP_W])).astype(bf16)
        return carry

    lax.fori_loop(0, nsub, sub_tile, 0)

    mo = _dot(mix_scr[...], wout_ref[...])
    y_ref[0] = x + _rmsnorm(mo, gpost_ref[...])

    @pl.when(t == nt - 1)
    def _final():
        conv_out[0] = ubuf[SUBLANES - (CONV_W - 1):SUBLANES, :]
        swak_out[0] = kprev[...]
        swav_out[0] = vprev[...]
        st = st_scr[...]
        for hh in range(N_HEADS):
            gla_out[0, hh] = st[HEAD_D * hh:HEAD_D * (hh + 1), HEAD_D * hh:HEAD_D * (hh + 1)].T


def _const_spec(shape):
    nd = len(shape)
    return pl.BlockSpec(shape, lambda b, t, _nd=nd: (0,) * _nd)


def _prompt_layer(x, memp, cos, sin, gpre, gpost, win, wout, wmem, convw, wg, bg, gnorm, sinks, jmat):
    B, L, _ = x.shape
    assert L % TL == 0
    nt = L // TL
    out_shape = (
        jax.ShapeDtypeStruct((B, L, D_MODEL), f32),
        jax.ShapeDtypeStruct((B, CONV_W - 1, GROUP_W), f32),
        jax.ShapeDtypeStruct((B, N_HEADS, HEAD_D, HEAD_D), f32),
        jax.ShapeDtypeStruct((B, WINDOW, SWA_KV_W), f32),
        jax.ShapeDtypeStruct((B, WINDOW, SWA_KV_W), f32),
        jax.ShapeDtypeStruct((B, N_MEM, GROUP_W), f32),
        jax.ShapeDtypeStruct((B, N_MEM, GROUP_W), f32),
    )
    in_specs = [
        pl.BlockSpec(memory_space=pltpu.SMEM),
        pl.BlockSpec((1, TL, D_MODEL), lambda b, t: (b, t, 0)),
        pl.BlockSpec((1, N_MEM, D_MODEL), lambda b, t: (b, 0, 0)),
        pl.BlockSpec((TL, LANES), lambda b, t: (t, 0)),
        pl.BlockSpec((TL, LANES), lambda b, t: (t, 0)),
        _const_spec((1, D_MODEL)), _const_spec((1, D_MODEL)),
        _const_spec((D_MODEL, NP)), _const_spec((D_MODEL, D_MODEL)), _const_spec((D_MODEL, 2 * GROUP_W)),
        _const_spec((CONV_W, GROUP_W)), _const_spec((LANES, GROUP_W)), _const_spec((1, GROUP_W)),
        _const_spec((1, GROUP_W)), _const_spec((GROUP_W, GROUP_W)),
    ]
    out_specs = (
        pl.BlockSpec((1, TL, D_MODEL), lambda b, t: (b, t, 0)),
        pl.BlockSpec((1, CONV_W - 1, GROUP_W), lambda b, t: (b, 0, 0)),
        pl.BlockSpec((1, N_HEADS, HEAD_D, HEAD_D), lambda b, t: (b, 0, 0, 0)),
        pl.BlockSpec((1, WINDOW, SWA_KV_W), lambda b, t: (b, 0, 0)),
        pl.BlockSpec((1, WINDOW, SWA_KV_W), lambda b, t: (b, 0, 0)),
        pl.BlockSpec((1, N_MEM, GROUP_W), lambda b, t: (b, 0, 0)),
        pl.BlockSpec((1, N_MEM, GROUP_W), lambda b, t: (b, 0, 0)),
    )
    scratch = [
        pltpu.VMEM((TL, NP), f32),
        pltpu.VMEM((TL, D_MODEL), bf16),
        pltpu.VMEM((SUBLANES + SUB, GROUP_W), f32),
        pltpu.VMEM((GLA_BLOCK + SUB, GROUP_W), f32),
        pltpu.VMEM((GLA_BLOCK + SUB, GROUP_W), f32),
        pltpu.VMEM((GLA_BLOCK + SUB, GROUP_W), f32),
        pltpu.VMEM((GROUP_W, GROUP_W), f32),
        pltpu.VMEM((SUB, SWA_KV_W), f32),
        pltpu.VMEM((SUB, SWA_KV_W), f32),
        pltpu.VMEM((N_MEM, GROUP_W), bf16),
        pltpu.VMEM((N_MEM, GROUP_W), bf16),
    ]
    return pl.pallas_call(
        _prompt_layer_kernel,
        grid=(B, nt),
        in_specs=in_specs,
        out_specs=out_specs,
        out_shape=out_shape,
        scratch_shapes=scratch,
        compiler_params=pltpu.CompilerParams(dimension_semantics=("arbitrary", "arbitrary"),
                                             vmem_limit_bytes=VMEM_LIMIT_BYTES),
        name="prompt_layer",
    )(sinks, x, memp, cos, sin, gpre, gpost, win, wout, wmem, convw, wg, bg, gnorm, jmat)


def _in_proj_kernel(x_ref, gpre_ref, win_ref, p_ref):
    p_ref[...] = _dot(_rmsnorm(x_ref[...], gpre_ref[...]).astype(bf16), win_ref[...])


def _in_proj(x, gpre, win):
    n = x.shape[0]
    tm = min(TL, n)
    assert n % tm == 0
    return pl.pallas_call(
        _in_proj_kernel,
        grid=(n // tm,),
        in_specs=[pl.BlockSpec((tm, D_MODEL), lambda i: (i, 0)),
                  pl.BlockSpec((1, D_MODEL), lambda i: (0, 0)),
                  pl.BlockSpec((D_MODEL, NP), lambda i: (0, 0))],
        out_specs=pl.BlockSpec((tm, NP), lambda i: (i, 0)),
        out_shape=jax.ShapeDtypeStruct((n, NP), f32),
        compiler_params=pltpu.CompilerParams(dimension_semantics=("arbitrary",),
                                             vmem_limit_bytes=VMEM_LIMIT_BYTES),
        name="sample_in_proj",
    )(x, gpre, win)


def _out_proj_kernel(x_ref, mix_ref, gpost_ref, wout_ref, y_ref):
    y_ref[...] = x_ref[...] + _rmsnorm(_dot(mix_ref[...], wout_ref[...]), gpost_ref[...])


def _out_proj(x, mix, gpost, wout):
    n = x.shape[0]
    tm = min(TL, n)
    assert n % tm == 0
    return pl.pallas_call(
        _out_proj_kernel,
        grid=(n // tm,),
        in_specs=[pl.BlockSpec((tm, D_MODEL), lambda i: (i, 0)),
                  pl.BlockSpec((tm, D_MODEL), lambda i: (i, 0)),
                  pl.BlockSpec((1, D_MODEL), lambda i: (0, 0)),
                  pl.BlockSpec((D_MODEL, D_MODEL), lambda i: (0, 0))],
        out_specs=pl.BlockSpec((tm, D_MODEL), lambda i: (i, 0)),
        out_shape=jax.ShapeDtypeStruct((n, D_MODEL), f32),
        compiler_params=pltpu.CompilerParams(dimension_semantics=("arbitrary",),
                                             vmem_limit_bytes=VMEM_LIMIT_BYTES),
        name="sample_out_proj",
    )(x, mix, gpost, wout)


def _sample_mix_kernel(seq_len, group, sinks_ref, p_ref, cos_ref, sin_ref, convb_ref, sgla_ref, kc_ref, vc_ref,
                       mkc_ref, mvc_ref, convw_ref, wg_ref, bg_ref, gnorm_ref, j_ref,
                       mix_ref, conv_out, gla_out, swak_out, swav_out):
    n = group * seq_len
    row_t = lax.broadcasted_iota(jnp.int32, (n, GROUP_W), 0) % seq_len

    u = p_ref[:, C_CC:C_CC + GROUP_W] * p_ref[:, C_CX:C_CX + GROUP_W]
    hist = convb_ref[...]
    um1 = jnp.where(row_t >= 1, pltpu.roll(u, 1, 0), pltpu.roll(hist, n - 1, 0))
    um2 = jnp.where(row_t >= 2, pltpu.roll(u, 2, 0), hist)
    cy = convw_ref[0:1, :] * um2 + convw_ref[1:2, :] * um1 + convw_ref[2:3, :] * u
    mix_ref[:, 0:GROUP_W] = (p_ref[:, C_CB:C_CB + GROUP_W] * cy * _silu(p_ref[:, C_CZ:C_CZ + GROUP_W])).astype(bf16)
    for g in range(group):
        conv_out[g] = u[g * seq_len + seq_len - (CONV_W - 1):(g + 1) * seq_len, :]

    qs = p_ref[:, C_GQ:C_GQ + GROUP_W] * QK_SCALE
    k = p_ref[:, C_GK:C_GK + GROUP_W]
    v = p_ref[:, C_GV:C_GV + GROUP_W]
    la = _gate_log_decay(p_ref[:, C_LR:C_LR + LANES], wg_ref, bg_ref)
    bc = _block_cumsum(la, seq_len)
    o = _dot((qs * k).astype(bf16), j_ref[...]) * v
    for d in range(1, seq_len):
        dec = jnp.exp(jnp.where(row_t >= d, bc - pltpu.roll(bc, d, 0), NEG))
        pw = qs * pltpu.roll(k, d, 0) * dec
        o = o + _dot(pw.astype(bf16), j_ref[...]) * pltpu.roll(v, d, 0)
    qd = qs * jnp.exp(bc)
    tot = [bc[(g + 1) * seq_len - 1:(g + 1) * seq_len, :] for g in range(group)]
    kd = k * jnp.exp(jnp.concatenate(
        [tot[g] - bc[g * seq_len:(g + 1) * seq_len, :] for g in range(group)], axis=0))
    eye = (lax.broadcasted_iota(jnp.int32, (HEAD_D, HEAD_D), 0)
           == lax.broadcasted_iota(jnp.int32, (HEAD_D, HEAD_D), 1))
    o_inter = []
    for g in range(group):
        rs = slice(g * seq_len, (g + 1) * seq_len)
        alpha = jnp.exp(tot[g])
        per_head = []
        for hh in range(N_HEADS):
            ls = slice(hh * HEAD_D, (hh + 1) * HEAD_D)
            s0 = sgla_ref[g, hh]
            per_head.append(_dot(qd[rs, ls].astype(bf16), s0.astype(bf16)))
            alpha_col = jnp.sum(jnp.where(eye, alpha[:, ls], 0.0), axis=1, keepdims=True)
            gla_out[g, hh] = alpha_col * s0 + _dot_tn(kd[rs, ls].astype(bf16), v[rs, ls].astype(bf16))
        o_inter.append(jnp.concatenate(per_head, axis=1))
    o = o + jnp.concatenate(o_inter, axis=0)
    mix_ref[:, GROUP_W:2 * GROUP_W] = _gla_norm_gate(o, p_ref[:, C_GZ:C_GZ + GROUP_W], gnorm_ref, j_ref).astype(bf16)

    cos = cos_ref[...]
    sin = sin_ref[...]
    qa = _rope(p_ref[:, C_SQA:C_SQA + LANES], cos, sin)
    qb = _rope(p_ref[:, C_SQB:C_SQB + LANES], cos, sin)
    kr = _rope(p_ref[:, C_SK:C_SK + SWA_KV_W], cos, sin)
    vx = p_ref[:, C_SV:C_SV + SWA_KV_W]
    qm = p_ref[:, C_MQ:C_MQ + GROUP_W]
    sink = _sink_column(sinks_ref, seq_len)
    hq = N_HEADS * seq_len
    qrow = lax.broadcasted_iota(jnp.int32, (hq, WINDOW), 0) % seq_len
    ccol = lax.broadcasted_iota(jnp.int32, (hq, WINDOW), 1)
    cache_valid = ccol > qrow
    nrow = lax.broadcasted_iota(jnp.int32, (hq, seq_len), 0) % seq_len
    ncol = lax.broadcasted_iota(jnp.int32, (hq, seq_len), 1)
    new_valid = ncol <= nrow
    oc_all, od_all = [], []
    for g in range(group):
        rs = slice(g * seq_len, (g + 1) * seq_len)
        kc = kc_ref[g]
        vc = vc_ref[g]
        qst = _stack_swa_q(qa[rs], qb[rs]).astype(bf16)
        s_c = jnp.where(cache_valid, _dot_nt(qst, kc.astype(bf16)) * QK_SCALE, NEG)
        s_n = jnp.where(new_valid, _dot_nt(qst, kr[rs].astype(bf16)) * QK_SCALE, NEG)
        mx = jnp.maximum(jnp.maximum(jnp.max(s_c, axis=1, keepdims=True), jnp.max(s_n, axis=1, keepdims=True)), sink)
        e_c = jnp.exp(s_c - mx)
        e_n = jnp.exp(s_n - mx)
        den = jnp.sum(e_c, axis=1, keepdims=True) + jnp.sum(e_n, axis=1, keepdims=True) + jnp.exp(sink - mx)
        ov = (_dot(e_c.astype(bf16), vc.astype(bf16)) + _dot(e_n.astype(bf16), vx[rs].astype(bf16))) / den
        oc_all.append(_unstack_swa(ov, seq_len))
        swak_out[g, 0:WINDOW - seq_len, :] = kc[seq_len:WINDOW, :]
        swak_out[g, WINDOW - seq_len:WINDOW, :] = kr[rs]
        swav_out[g, 0:WINDOW - seq_len, :] = vc[seq_len:WINDOW, :]
        swav_out[g, WINDOW - seq_len:WINDOW, :] = vx[rs]

        sm = _dot_nt(_stack_heads(qm[rs]).astype(bf16), mkc_ref[g].astype(bf16)) * QK_SCALE
        mm = jnp.max(sm, axis=1, keepdims=True)
        em = jnp.exp(sm - mm)
        dm = jnp.sum(em, axis=1, keepdims=True)
        od_all.append(_unstack_heads(_dot(em.astype(bf16), mvc_ref[g].astype(bf16)) / dm, seq_len))
    oc = jnp.concatenate(oc_all, axis=0)
    od = jnp.concatenate(od_all, axis=0)
    mix_ref[:, 2 * GROUP_W:3 * GROUP_W] = (oc * _silu(p_ref[:, C_SZ:C_SZ + GROUP_W])).astype(bf16)
    mix_ref[:, 3 * GROUP_W:4 * GROUP_W] = (od * _silu(p_ref[:, C_MZ:C_MZ + GROUP_W])).astype(bf16)


def _sample_mix(p, cos, sin, convb, sgla, kc, vc, mkc, mvc, convw, wg, bg, gnorm, sinks, jmat, seq_len, group):
    nseq = sgla.shape[0]
    assert nseq % group == 0 and seq_len == SUBLANES
    n = group * seq_len
    c2 = lambda shape: pl.BlockSpec(shape, lambda i: (0, 0))
    in_specs = [
        pl.BlockSpec(memory_space=pltpu.SMEM),
        pl.BlockSpec((n, NP), lambda i: (i, 0)),
        c2((n, LANES)), c2((n, LANES)),
        pl.BlockSpec((n, GROUP_W), lambda i: (i, 0)),
        pl.BlockSpec((group, N_HEADS, HEAD_D, HEAD_D), lambda i: (i, 0, 0, 0)),
        pl.BlockSpec((group, WINDOW, SWA_KV_W), lambda i: (i, 0, 0)),
        pl.BlockSpec((group, WINDOW, SWA_KV_W), lambda i: (i, 0, 0)),
        pl.BlockSpec((group, N_MEM, GROUP_W), lambda i: (i, 0, 0)),
        pl.BlockSpec((group, N_MEM, GROUP_W), lambda i: (i, 0, 0)),
        c2((CONV_W, GROUP_W)), c2((LANES, GROUP_W)), c2((1, GROUP_W)), c2((1, GROUP_W)), c2((GROUP_W, GROUP_W)),
    ]
    out_shape = (
        jax.ShapeDtypeStruct((nseq * seq_len, D_MODEL), bf16),
        jax.ShapeDtypeStruct((nseq, CONV_W - 1, GROUP_W), f32),
        jax.ShapeDtypeStruct((nseq, N_HEADS, HEAD_D, HEAD_D), f32),
        jax.ShapeDtypeStruct((nseq, WINDOW, SWA_KV_W), f32),
        jax.ShapeDtypeStruct((nseq, WINDOW, SWA_KV_W), f32),
    )
    out_specs = (
        pl.BlockSpec((n, D_MODEL), lambda i: (i, 0)),
        pl.BlockSpec((group, CONV_W - 1, GROUP_W), lambda i: (i, 0, 0)),
        pl.BlockSpec((group, N_HEADS, HEAD_D, HEAD_D), lambda i: (i, 0, 0, 0)),
        pl.BlockSpec((group, WINDOW, SWA_KV_W), lambda i: (i, 0, 0)),
        pl.BlockSpec((group, WINDOW, SWA_KV_W), lambda i: (i, 0, 0)),
    )
    return pl.pallas_call(
        functools.partial(_sample_mix_kernel, seq_len, group),
        grid=(nseq // group,),
        in_specs=in_specs,
        out_specs=out_specs,
        out_shape=out_shape,
        compiler_params=pltpu.CompilerParams(dimension_semantics=("arbitrary",),
                                             vmem_limit_bytes=VMEM_LIMIT_BYTES),
        name="sample_mix",
    )(sinks, p, cos, sin, convb, sgla, kc, vc, mkc, mvc, convw, wg, bg, gnorm, jmat)


def _permute_heads(w, axis):
    parts = [lax.slice_in_dim(w, h * HEAD_D, (h + 1) * HEAD_D, axis=axis) for h in _HEAD_PERM]
    return jnp.concatenate(parts, axis=axis)


def _layout_w_in(w_in):
    sl = lambda a, b: w_in[..., a:b]
    parts = [sl(0, _O_GLR), sl(_O_GZ, _O_SQ), _permute_heads(sl(_O_SQ, _O_SK), 2), sl(_O_SK, _O_SV), sl(_O_SV, _O_SZ),
             _permute_heads(sl(_O_SZ, _O_MQ), 2), sl(_O_MQ, _O_MZ), sl(_O_MZ, _O_END), sl(_O_GLR, _O_GZ)]
    w = jnp.concatenate(parts, axis=-1)
    pad = NP - w.shape[-1]
    return jnp.pad(w, ((0, 0), (0, 0), (0, pad))).astype(bf16)


def _layout_w_out(w_out):
    parts = [w_out[:, 0:2 * GROUP_W], _permute_heads(w_out[:, 2 * GROUP_W:3 * GROUP_W], 1), w_out[:, 3 * GROUP_W:]]
    return jnp.concatenate(parts, axis=1).astype(bf16)


def _rope_tables(pos):
    half = HEAD_D // 2
    inv = jnp.power(ROPE_THETA, -jnp.arange(half, dtype=f32) / half)
    ang = pos.astype(f32)[:, None] * inv[None, :]
    cos = jnp.tile(jnp.cos(ang), (1, LANES // half))
    sin = jnp.sin(ang)
    sin_signed = jnp.tile(jnp.concatenate([-sin, sin], axis=1), (1, LANES // HEAD_D))
    return cos, sin_signed


def kernel(x_prompt, x_sample, state_conv, state_gla, cache_swa_k, cache_swa_v, cache_mem_k, cache_mem_v,
           mem_prompt, norm_pre, norm_post, w_in, conv_w, gla_w_gate, gla_b_gate, gla_norm, swa_sinks,
           w_mem_kv, w_out):
    depth = w_in.shape[0]
    B, L, _ = x_prompt.shape
    nseq, seq_len, _ = x_sample.shape
    past_len = 8192
    group = 8

    win = _layout_w_in(w_in)
    wout = _layout_w_out(w_out)
    wmem = w_mem_kv.astype(bf16)
    wg = jnp.pad(gla_w_gate, ((0, 0), (0, LANES - GATE_RANK), (0, 0))).astype(bf16)
    head_id = np.arange(GROUP_W) // HEAD_D
    jmat = jnp.asarray(head_id[:, None] == head_id[None, :], dtype=bf16)
    sinks = swa_sinks.astype(f32)

    cos_p, sin_p = _rope_tables(jnp.arange(L, dtype=jnp.int32))
    cos_s, sin_s = _rope_tables(past_len + jnp.arange(seq_len, dtype=jnp.int32))
    cos_s = jnp.tile(cos_s, (group, 1))
    sin_s = jnp.tile(sin_s, (group, 1))

    convb = jnp.pad(state_conv, ((0, 0), (0, 0), (0, seq_len - (CONV_W - 1)), (0, 0)))
    convb = convb.reshape(depth, nseq * seq_len, GROUP_W)
    kc = cache_swa_k.reshape(depth, nseq, WINDOW, SWA_KV_W)
    vc = cache_swa_v.reshape(depth, nseq, WINDOW, SWA_KV_W)
    mkc = cache_mem_k.reshape(depth, nseq, N_MEM, GROUP_W)
    mvc = cache_mem_v.reshape(depth, nseq, N_MEM, GROUP_W)

    hp = x_prompt
    hs = x_sample.reshape(nseq * seq_len, D_MODEL)
    outs_p = [[] for _ in range(6)]
    outs_s = [[] for _ in range(4)]
    for l in range(depth):
        gpre = norm_pre[l][None, :]
        gpost = norm_post[l][None, :]
        bg = gla_b_gate[l][None, :]
        gn = gla_norm[l][None, :]
        res = _prompt_layer(hp, mem_prompt, cos_p, sin_p, gpre, gpost, win[l], wout[l], wmem[l], conv_w[l],
                            wg[l], bg, gn, sinks[l], jmat)
        hp = res[0]
        for i in range(6):
            outs_p[i].append(res[i + 1])
        ps = _in_proj(hs, gpre, win[l])
        mix, c2, g2, k2, v2 = _sample_mix(ps, cos_s, sin_s, convb[l], state_gla[l], kc[l], vc[l], mkc[l], mvc[l],
                                          conv_w[l], wg[l], bg, gn, sinks[l], jmat, seq_len, group)
        hs = _out_proj(hs, mix, gpost, wout[l])
        for i, a in enumerate((c2, g2, k2, v2)):
            outs_s[i].append(a)

    kv_shape_p = (depth, B, WINDOW, 2, HEAD_D)
    mem_shape = (depth, B, N_MEM, N_HEADS, HEAD_D)
    kv_shape_s = (depth, nseq, WINDOW, 2, HEAD_D)
    return (hp, hs.reshape(nseq, seq_len, D_MODEL),
            jnp.stack(outs_p[0]), jnp.stack(outs_p[1]),
            jnp.stack(outs_p[2]).reshape(kv_shape_p), jnp.stack(outs_p[3]).reshape(kv_shape_p),
            jnp.stack(outs_p[4]).reshape(mem_shape), jnp.stack(outs_p[5]).reshape(mem_shape),
            jnp.stack(outs_s[0]), jnp.stack(outs_s[1]),
            jnp.stack(outs_s[2]).reshape(kv_shape_s), jnp.stack(outs_s[3]).reshape(kv_shape_s))
```

```python
import functools
import math

import jax
import jax.numpy as jnp
import numpy as np
from jax import lax
from jax.experimental import pallas as pl
from jax.experimental.pallas import tpu as pltpu

f32 = jnp.float32
bf16 = jnp.bfloat16

D_MODEL = 1024
GROUP_W = 256
HEAD_D = 64
N_HEADS = 4
SWA_KV_W = 128
N_MEM = 256
WINDOW = 128
CONV_W = 3
GATE_RANK = 16
GATE_NORM = 16.0
GLA_BLOCK = 16
ROPE_THETA = 10000.0
PAST_LEN = 8192
EPS = 1e-6
NEG = -1e30
QK_SCALE = HEAD_D ** -0.5

LANES = 128
SUBLANES = 8
VMEM_LIMIT_BYTES = 56 * 1024 * 1024

C_CX, C_CB, C_CC, C_CZ = 0, 256, 512, 768
C_GQ, C_GK, C_GV, C_GZ = 1024, 1280, 1536, 1792
C_SQA, C_SQB, C_SK, C_SV, C_SZ = 2048, 2176, 2304, 2432, 2560
C_MQ, C_MZ = 2816, 3072
C_LR = 3328
NP = 3456
_O_GLR, _O_GZ, _O_SQ, _O_SK, _O_SV, _O_SZ, _O_MQ, _O_MZ, _O_END = 1792, 1808, 2064, 2320, 2448, 2576, 2832, 3088, 3344
_HEAD_PERM = (0, 2, 1, 3)

SUB = 128
TL = 512


def _dot(a, b):
    return jnp.dot(a, b, preferred_element_type=f32)


def _dot_nt(a, b):
    return lax.dot_general(a, b, (((1,), (1,)), ((), ())), preferred_element_type=f32)


def _dot_tn(a, b):
    return lax.dot_general(a, b, (((0,), (0,)), ((), ())), preferred_element_type=f32)


def _rmsnorm(x, g):
    return x * lax.rsqrt(jnp.mean(x * x, axis=-1, keepdims=True) + EPS) * g


def _silu(z):
    return z * (0.5 + 0.5 * jnp.tanh(0.5 * z))


def _log_sigmoid(x):
    return jnp.minimum(x, 0.0) - jnp.log1p(jnp.exp(-jnp.abs(x)))


def _head_sum(x, j_ref):
    hi = x.astype(bf16)
    lo = (x - hi.astype(f32)).astype(bf16)
    return _dot(hi, j_ref[...]) + _dot(lo, j_ref[...])


def _rope(x, cos, sin_signed):
    lane = lax.broadcasted_iota(jnp.int32, x.shape, 1)
    swapped = jnp.where((lane % HEAD_D) < HEAD_D // 2,
                        pltpu.roll(x, LANES - HEAD_D // 2, 1), pltpu.roll(x, HEAD_D // 2, 1))
    return x * cos + swapped * sin_signed


def _stack_heads(x):
    lane_head = lax.broadcasted_iota(jnp.int32, x.shape, 1) // HEAD_D
    return jnp.concatenate([jnp.where(lane_head == h, x, 0.0) for h in range(N_HEADS)], axis=0)


def _unstack_heads(o, n):
    lane_head = lax.broadcasted_iota(jnp.int32, (n, GROUP_W), 1) // HEAD_D
    out = o[0:n]
    for h in range(1, N_HEADS):
        out = jnp.where(lane_head == h, o[h * n:(h + 1) * n], out)
    return out


def _stack_swa_q(qa, qb):
    low = lax.broadcasted_iota(jnp.int32, qa.shape, 1) < HEAD_D
    return jnp.concatenate([jnp.where(low, qa, 0.0), jnp.where(low, 0.0, qa),
                            jnp.where(low, qb, 0.0), jnp.where(low, 0.0, qb)], axis=0)


def _unstack_swa(o, n):
    low = lax.broadcasted_iota(jnp.int32, (n, SWA_KV_W), 1) < HEAD_D
    ca = jnp.where(low, o[0:n], o[n:2 * n])
    cb = jnp.where(low, o[2 * n:3 * n], o[3 * n:4 * n])
    return jnp.concatenate([ca, cb], axis=1)


def _sink_column(sinks_ref, n):
    return jnp.concatenate([jnp.full((n, 1), sinks_ref[h], f32) for h in _HEAD_PERM], axis=0)


def _block_cumsum(la, block):
    row = lax.broadcasted_iota(jnp.int32, la.shape, 0) % block
    b = la
    s = 1
    while s < block:
        b = b + jnp.where(row >= s, pltpu.roll(b, s, 0), 0.0)
        s *= 2
    return b


def _gate_log_decay(p_lr, wg_ref, bg_ref):
    pre = _dot(p_lr.astype(bf16), wg_ref[...]) + bg_ref[...]
    return _log_sigmoid(pre) * (1.0 / GATE_NORM)


def _gla_norm_gate(o, gz, gnorm_ref, j_ref):
    ms = _head_sum(o * o, j_ref) * (1.0 / HEAD_D)
    return o * lax.rsqrt(ms + EPS) * gnorm_ref[...] * _silu(gz)


GROUPS = SUB // SUBLANES
LEVEL_GROUPS = (1, 2, 4)


def _row_groups(x):
    return [x[SUBLANES * r:SUBLANES * (r + 1), :] for r in range(x.shape[0] // SUBLANES)]


def _prompt_layer_kernel(sinks_ref, x_ref, memp_ref, cos_ref, sin_ref, gpre_ref, gpost_ref, win_ref, wout_ref,
                         wmem_ref, convw_ref, wg_ref, bg_ref, gnorm_ref, j_ref, lvl_ref, swab_ref,
                         y_ref, conv_out, gla_out, swak_out, swav_out, mk_out, mv_out,
                         p_scr, mix_scr, cbuf, st_scr, kprev, vprev, mkb, mvb):
    t = pl.program_id(1)
    nt = pl.num_programs(1)
    nsub = TL // SUB

    @pl.when(t == 0)
    def _init():
        cbuf[...] = jnp.zeros((SUBLANES, GROUP_W), f32)
        st_scr[...] = jnp.zeros((GROUP_W, GROUP_W), f32)
        kprev[...] = jnp.zeros((SUB, SWA_KV_W), f32)
        vprev[...] = jnp.zeros((SUB, SWA_KV_W), f32)
        mkv = _dot(memp_ref[0].astype(bf16), wmem_ref[...])
        mk_out[0] = mkv[:, 0:GROUP_W]
        mv_out[0] = mkv[:, GROUP_W:2 * GROUP_W]
        mkb[...] = mkv[:, 0:GROUP_W].astype(bf16)
        mvb[...] = mkv[:, GROUP_W:2 * GROUP_W].astype(bf16)

    x = x_ref[0]
    h = _rmsnorm(x, gpre_ref[...]).astype(bf16)
    p_scr[...] = _dot(h, win_ref[...])

    sub8 = lax.broadcasted_iota(jnp.int32, (SUBLANES, GROUP_W), 0)

    def sub_tile(j, carry):
        r0 = pl.multiple_of(j * SUB, SUB)
        rows = pl.ds(r0, SUB)

        u = p_scr[rows, C_CC:C_CC + GROUP_W] * p_scr[rows, C_CX:C_CX + GROUP_W]
        last = SUB - SUBLANES
        prev1 = jnp.where(sub8 == 0, cbuf[0:1, :], pltpu.roll(u[last:SUB, :], 1, 0))
        prev2 = jnp.where(sub8 == 0, cbuf[1:2, :], pltpu.roll(u[last - SUBLANES:last, :], 1, 0))
        um1 = jnp.concatenate([prev1, u[0:last, :]], axis=0)
        um2 = jnp.concatenate([prev2, prev1, u[0:last - SUBLANES, :]], axis=0)
        cy = convw_ref[0:1, :] * um2 + convw_ref[1:2, :] * um1 + convw_ref[2:3, :] * u
        cbuf[0:1, :] = u[SUB - 1:SUB, :]
        cbuf[1:2, :] = u[last - 1:last, :]
        a_out = p_scr[rows, C_CB:C_CB + GROUP_W] * cy * _silu(p_scr[rows, C_CZ:C_CZ + GROUP_W])
        mix_scr[rows, 0:GROUP_W] = a_out.astype(bf16)

        qg = _row_groups(p_scr[rows, C_GQ:C_GQ + GROUP_W] * QK_SCALE)
        k = p_scr[rows, C_GK:C_GK + GROUP_W]
        v = p_scr[rows, C_GV:C_GV + GROUP_W]
        kg = _row_groups(k)
        vg = _row_groups(v)
        lag = _row_groups(_gate_log_decay(p_scr[rows, C_LR:C_LR + LANES], wg_ref, bg_ref))
        bg_ = [lag[0]]
        for r in range(1, GROUPS):
            bg_.append(bg_[-1] + lag[r])
        tot = bg_[GROUPS - 1]
        pw = []
        for r in range(GROUPS):
            for s in range(r):
                pw.append(qg[r] * kg[s] * jnp.exp(bg_[r] - bg_[s]))
            pw.append(qg[r] * kg[r])
        scores = _dot(jnp.concatenate(pw, axis=0).astype(bf16), j_ref[...])
        og = []
        idx = 0
        for r in range(GROUPS):
            acc = None
            for s in range(r + 1):
                term = scores[SUBLANES * idx:SUBLANES * (idx + 1), :] * vg[s]
                acc = term if acc is None else acc + term
                idx += 1
            og.append(acc)
        o = jnp.concatenate(og, axis=0)

        def decayed(group):
            before = jnp.zeros((SUBLANES, GROUP_W), f32)
            after = jnp.zeros((SUBLANES, GROUP_W), f32)
            for s in range(1, group):
                before = before + jnp.where(sub8 % group >= s, pltpu.roll(tot, s, 0), 0.0)
                after = after + jnp.where(sub8 % group < group - s, pltpu.roll(tot, SUBLANES - s, 0), 0.0)
            qd = jnp.concatenate([qg[r] * jnp.exp(bg_[r] + before) for r in range(GROUPS)], axis=0)
            kd = jnp.concatenate([kg[r] * jnp.exp((tot - bg_[r]) + after) for r in range(GROUPS)], axis=0)
            return qd, kd

        attn = None
        for li, group in enumerate(LEVEL_GROUPS):
            qd, kd = decayed(group)
            s = _dot_nt(_stack_heads(qd).astype(bf16), kd.astype(bf16)) * lvl_ref[li]
            attn = s if attn is None else attn + s
        o = o + _unstack_heads(_dot(attn.astype(bf16), v.astype(bf16)), SUB)
        qd, kd = decayed(SUB // GLA_BLOCK)
        st = st_scr[...]
        o = o + _dot_nt(qd.astype(bf16), st.astype(bf16))
        total = jnp.sum(tot, axis=0, keepdims=True)
        upd = _dot_tn(v.astype(bf16), kd.astype(bf16))
        same_head = (lax.broadcasted_iota(jnp.int32, (GROUP_W, GROUP_W), 0) // HEAD_D
                     == lax.broadcasted_iota(jnp.int32, (GROUP_W, GROUP_W), 1) // HEAD_D)
        st_scr[...] = st * jnp.exp(total) + jnp.where(same_head, upd, 0.0)
        b_out = _gla_norm_gate(o, p_scr[rows, C_GZ:C_GZ + GROUP_W], gnorm_ref, j_ref)
        mix_scr[rows, GROUP_W:2 * GROUP_W] = b_out.astype(bf16)

        cos = cos_ref[rows, :]
        sin = sin_ref[rows, :]
        qa = _rope(p_scr[rows, C_SQA:C_SQA + LANES], cos, sin) * QK_SCALE
        qb = _rope(p_scr[rows, C_SQB:C_SQB + LANES], cos, sin) * QK_SCALE
        kr = _rope(p_scr[rows, C_SK:C_SK + SWA_KV_W], cos, sin)
        vx = p_scr[rows, C_SV:C_SV + SWA_KV_W]
        k_all = jnp.concatenate([kprev[...], kr], axis=0).astype(bf16)
        v_all = jnp.concatenate([vprev[...], vx], axis=0).astype(bf16)
        first = jnp.where((t * nsub + j) == 0, 1, 0)
        sc = _dot_nt(_stack_swa_q(qa, qb).astype(bf16), k_all) + swab_ref[first]
        sink = _sink_column(sinks_ref, SUB)
        mx = jnp.maximum(jnp.max(sc, axis=1, keepdims=True), sink)
        e = jnp.exp(sc - mx)
        den = jnp.sum(e, axis=1, keepdims=True) + jnp.exp(sink - mx)
        oc = _unstack_swa(_dot(e.astype(bf16), v_all) * (1.0 / den), SUB)
        mix_scr[rows, 2 * GROUP_W:3 * GROUP_W] = (oc * _silu(p_scr[rows, C_SZ:C_SZ + GROUP_W])).astype(bf16)
        kprev[...] = kr
        vprev[...] = vx

        qm = _stack_heads(p_scr[rows, C_MQ:C_MQ + GROUP_W] * QK_SCALE).astype(bf16)
        sm = _dot_nt(qm, mkb[...])
        mm = jnp.max(sm, axis=1, keepdims=True)
        em = jnp.exp(sm - mm)
        dm = jnp.sum(em, axis=1, keepdims=True)
        od = _unstack_heads(_dot(em.astype(bf16), mvb[...]) * (1.0 / dm), SUB)
        mix_scr[rows, 3 * GROUP_W:4 * GROUP_W] = (od * _silu(p_scr[rows, C_MZ:C_MZ + GROUP_W])).astype(bf16)
        return carry

    lax.fori_loop(0, nsub, sub_tile, 0)

    mo = _dot(mix_scr[...], wout_ref[...])
    y_ref[0] = x + _rmsnorm(mo, gpost_ref[...])

    @pl.when(t == nt - 1)
    def _final():
        conv_out[0, 0:1, :] = cbuf[1:2, :]
        conv_out[0, 1:2, :] = cbuf[0:1, :]
        for r in range(GROUPS):
            swak_out[0, pl.ds(r, SUBLANES, stride=GLA_BLOCK), :] = kprev[SUBLANES * r:SUBLANES * (r + 1), :]
            swav_out[0, pl.ds(r, SUBLANES, stride=GLA_BLOCK), :] = vprev[SUBLANES * r:SUBLANES * (r + 1), :]
        st = st_scr[...]
        for hh in range(N_HEADS):
            gla_out[0, hh] = st[HEAD_D * hh:HEAD_D * (hh + 1), HEAD_D * hh:HEAD_D * (hh + 1)].T


def _layer_spec(shape, l):
    nd = len(shape)
    return pl.BlockSpec((None,) + tuple(shape), lambda b, t, _l=l, _nd=nd: (_l,) + (0,) * _nd)


def _const_spec(shape):
    nd = len(shape)
    return pl.BlockSpec(shape, lambda b, t, _nd=nd: (0,) * _nd)


def _prompt_layer(l, x, memp, cos, sin, gpre, gpost, win, wout, wmem, convw, wg, bg, gnorm, sinks, jmat, lvl, swab):
    B, L, _ = x.shape
    assert L % TL == 0
    nt = L // TL
    out_shape = (
        jax.ShapeDtypeStruct((B, L, D_MODEL), f32),
        jax.ShapeDtypeStruct((B, CONV_W - 1, GROUP_W), f32),
        jax.ShapeDtypeStruct((B, N_HEADS, HEAD_D, HEAD_D), f32),
        jax.ShapeDtypeStruct((B, WINDOW, SWA_KV_W), f32),
        jax.ShapeDtypeStruct((B, WINDOW, SWA_KV_W), f32),
        jax.ShapeDtypeStruct((B, N_MEM, GROUP_W), f32),
        jax.ShapeDtypeStruct((B, N_MEM, GROUP_W), f32),
    )
    in_specs = [
        pl.BlockSpec(memory_space=pltpu.SMEM),
        pl.BlockSpec((1, TL, D_MODEL), lambda b, t: (b, t, 0)),
        pl.BlockSpec((1, N_MEM, D_MODEL), lambda b, t: (b, 0, 0)),
        pl.BlockSpec((TL, LANES), lambda b, t: (t, 0)),
        pl.BlockSpec((TL, LANES), lambda b, t: (t, 0)),
        _layer_spec((1, D_MODEL), l), _layer_spec((1, D_MODEL), l),
        _layer_spec((D_MODEL, NP), l), _layer_spec((D_MODEL, D_MODEL), l), _layer_spec((D_MODEL, 2 * GROUP_W), l),
        _layer_spec((CONV_W, GROUP_W), l), _layer_spec((LANES, GROUP_W), l), _layer_spec((1, GROUP_W), l),
        _layer_spec((1, GROUP_W), l), _const_spec((GROUP_W, GROUP_W)),
        _const_spec((len(LEVEL_GROUPS), N_HEADS * SUB, SUB)), _const_spec((2, N_HEADS * SUB, 2 * SUB)),
    ]
    out_specs = (
        pl.BlockSpec((1, TL, D_MODEL), lambda b, t: (b, t, 0)),
        pl.BlockSpec((1, CONV_W - 1, GROUP_W), lambda b, t: (b, 0, 0)),
        pl.BlockSpec((1, N_HEADS, HEAD_D, HEAD_D), lambda b, t: (b, 0, 0, 0)),
        pl.BlockSpec((1, WINDOW, SWA_KV_W), lambda b, t: (b, 0, 0)),
        pl.BlockSpec((1, WINDOW, SWA_KV_W), lambda b, t: (b, 0, 0)),
        pl.BlockSpec((1, N_MEM, GROUP_W), lambda b, t: (b, 0, 0)),
        pl.BlockSpec((1, N_MEM, GROUP_W), lambda b, t: (b, 0, 0)),
    )
    scratch = [
        pltpu.VMEM((TL, NP), f32),
        pltpu.VMEM((TL, D_MODEL), bf16),
        pltpu.VMEM((SUBLANES, GROUP_W), f32),
        pltpu.VMEM((GROUP_W, GROUP_W), f32),
        pltpu.VMEM((SUB, SWA_KV_W), f32),
        pltpu.VMEM((SUB, SWA_KV_W), f32),
        pltpu.VMEM((N_MEM, GROUP_W), bf16),
        pltpu.VMEM((N_MEM, GROUP_W), bf16),
    ]
    return pl.pallas_call(
        _prompt_layer_kernel,
        grid=(B, nt),
        in_specs=in_specs,
        out_specs=out_specs,
        out_shape=out_shape,
        scratch_shapes=scratch,
        compiler_params=pltpu.CompilerParams(dimension_semantics=("arbitrary", "arbitrary"),
                                             vmem_limit_bytes=VMEM_LIMIT_BYTES),
        name="prompt_layer",
    )(sinks[l], x, memp, cos, sin, gpre, gpost, win, wout, wmem, convw, wg, bg, gnorm, jmat, lvl, swab)


def _subtile_constants():
    row = np.arange(SUB)
    tok = (row % SUBLANES) * GLA_BLOCK + row // SUBLANES
    blk = row % SUBLANES
    levels = []
    for group in LEVEL_GROUPS:
        g = blk // group
        sel = (g[:, None] == g[None, :] + 1) & (g[:, None] % 2 == 1)
        levels.append(np.tile(sel, (N_HEADS, 1)))
    key_tok = np.concatenate([tok - SUB, tok])
    valid = (key_tok[None, :] <= tok[:, None]) & (key_tok[None, :] > tok[:, None] - WINDOW)
    bias = [np.where(valid, 0.0, NEG), np.where(valid & (key_tok[None, :] >= 0), 0.0, NEG)]
    bias = np.stack([np.tile(b, (N_HEADS, 1)) for b in bias])
    return jnp.asarray(np.stack(levels), dtype=f32), jnp.asarray(bias, dtype=f32)


def _to_kernel_order(x, axis):
    shp = x.shape
    n = shp[axis]
    x = x.reshape(shp[:axis] + (n // SUB, SUBLANES, GLA_BLOCK) + shp[axis + 1:])
    return jnp.swapaxes(x, axis + 1, axis + 2).reshape(shp)


def _from_kernel_order(x, axis):
    shp = x.shape
    n = shp[axis]
    x = x.reshape(shp[:axis] + (n // SUB, GLA_BLOCK, SUBLANES) + shp[axis + 1:])
    return jnp.swapaxes(x, axis + 1, axis + 2).reshape(shp)


def _in_proj_kernel(x_ref, gpre_ref, win_ref, p_ref):
    p_ref[...] = _dot(_rmsnorm(x_ref[...], gpre_ref[...]).astype(bf16), win_ref[...])


def _in_proj(x, gpre, win):
    n = x.shape[0]
    tm = min(TL, n)
    assert n % tm == 0
    return pl.pallas_call(
        _in_proj_kernel,
        grid=(n // tm,),
        in_specs=[pl.BlockSpec((tm, D_MODEL), lambda i: (i, 0)),
                  pl.BlockSpec((1, D_MODEL), lambda i: (0, 0)),
                  pl.BlockSpec((D_MODEL, NP), lambda i: (0, 0))],
        out_specs=pl.BlockSpec((tm, NP), lambda i: (i, 0)),
        out_shape=jax.ShapeDtypeStruct((n, NP), f32),
        compiler_params=pltpu.CompilerParams(dimension_semantics=("arbitrary",),
                                             vmem_limit_bytes=VMEM_LIMIT_BYTES),
        name="sample_in_proj",
    )(x, gpre, win)


def _out_proj_kernel(x_ref, mix_ref, gpost_ref, wout_ref, y_ref):
    y_ref[...] = x_ref[...] + _rmsnorm(_dot(mix_ref[...], wout_ref[...]), gpost_ref[...])


def _out_proj(x, mix, gpost, wout):
    n = x.shape[0]
    tm = min(TL, n)
    assert n % tm == 0
    return pl.pallas_call(
        _out_proj_kernel,
        grid=(n // tm,),
        in_specs=[pl.BlockSpec((tm, D_MODEL), lambda i: (i, 0)),
                  pl.BlockSpec((tm, D_MODEL), lambda i: (i, 0)),
                  pl.BlockSpec((1, D_MODEL), lambda i: (0, 0)),
                  pl.BlockSpec((D_MODEL, D_MODEL), lambda i: (0, 0))],
        out_specs=pl.BlockSpec((tm, D_MODEL), lambda i: (i, 0)),
        out_shape=jax.ShapeDtypeStruct((n, D_MODEL), f32),
        compiler_params=pltpu.CompilerParams(dimension_semantics=("arbitrary",),
                                             vmem_limit_bytes=VMEM_LIMIT_BYTES),
        name="sample_out_proj",
    )(x, mix, gpost, wout)


def _sample_mix_kernel(seq_len, group, sinks_ref, p_ref, cos_ref, sin_ref, convb_ref, sgla_ref, kc_ref, vc_ref,
                       mkc_ref, mvc_ref, convw_ref, wg_ref, bg_ref, gnorm_ref, j_ref,
                       mix_ref, conv_out, gla_out, swak_out, swav_out):
    n = group * seq_len
    row_t = lax.broadcasted_iota(jnp.int32, (n, GROUP_W), 0) % seq_len

    u = p_ref[:, C_CC:C_CC + GROUP_W] * p_ref[:, C_CX:C_CX + GROUP_W]
    hist = convb_ref[...]
    um1 = jnp.where(row_t >= 1, pltpu.roll(u, 1, 0), pltpu.roll(hist, n - 1, 0))
    um2 = jnp.where(row_t >= 2, pltpu.roll(u, 2, 0), hist)
    cy = convw_ref[0:1, :] * um2 + convw_ref[1:2, :] * um1 + convw_ref[2:3, :] * u
    mix_ref[:, 0:GROUP_W] = (p_ref[:, C_CB:C_CB + GROUP_W] * cy * _silu(p_ref[:, C_CZ:C_CZ + GROUP_W])).astype(bf16)
    for g in range(group):
        conv_out[g] = u[g * seq_len + seq_len - (CONV_W - 1):(g + 1) * seq_len, :]

    qs = p_ref[:, C_GQ:C_GQ + GROUP_W] * QK_SCALE
    k = p_ref[:, C_GK:C_GK + GROUP_W]
    v = p_ref[:, C_GV:C_GV + GROUP_W]
    la = _gate_log_decay(p_ref[:, C_LR:C_LR + LANES], wg_ref, bg_ref)
    bc = _block_cumsum(la, seq_len)
    o = _dot((qs * k).astype(bf16), j_ref[...]) * v
    for d in range(1, seq_len):
        dec = jnp.exp(jnp.where(row_t >= d, bc - pltpu.roll(bc, d, 0), NEG))
        pw = qs * pltpu.roll(k, d, 0) * dec
        o = o + _dot(pw.astype(bf16), j_ref[...]) * pltpu.roll(v, d, 0)
    qd = qs * jnp.exp(bc)
    tot = [bc[(g + 1) * seq_len - 1:(g + 1) * seq_len, :] for g in range(group)]
    kd = k * jnp.exp(jnp.concatenate(
        [tot[g] - bc[g * seq_len:(g + 1) * seq_len, :] for g in range(group)], axis=0))
    eye = (lax.broadcasted_iota(jnp.int32, (HEAD_D, HEAD_D), 0)
           == lax.broadcasted_iota(jnp.int32, (HEAD_D, HEAD_D), 1))
    o_inter = []
    for g in range(group):
        rs = slice(g * seq_len, (g + 1) * seq_len)
        alpha = jnp.exp(tot[g])
        per_head = []
        for hh in range(N_HEADS):
            ls = slice(hh * HEAD_D, (hh + 1) * HEAD_D)
            s0 = sgla_ref[g, hh]
            per_head.append(_dot(qd[rs, ls].astype(bf16), s0.astype(bf16)))
            alpha_col = jnp.sum(jnp.where(eye, alpha[:, ls], 0.0), axis=1, keepdims=True)
            gla_out[g, hh] = alpha_col * s0 + _dot_tn(kd[rs, ls].astype(bf16), v[rs, ls].astype(bf16))
        o_inter.append(jnp.concatenate(per_head, axis=1))
    o = o + jnp.concatenate(o_inter, axis=0)
    mix_ref[:, GROUP_W:2 * GROUP_W] = _gla_norm_gate(o, p_ref[:, C_GZ:C_GZ + GROUP_W], gnorm_ref, j_ref).astype(bf16)

    cos = cos_ref[...]
    sin = sin_ref[...]
    qa = _rope(p_ref[:, C_SQA:C_SQA + LANES], cos, sin)
    qb = _rope(p_ref[:, C_SQB:C_SQB + LANES], cos, sin)
    kr = _rope(p_ref[:, C_SK:C_SK + SWA_KV_W], cos, sin)
    vx = p_ref[:, C_SV:C_SV + SWA_KV_W]
    qm = p_ref[:, C_MQ:C_MQ + GROUP_W]
    sink = _sink_column(sinks_ref, seq_len)
    hq = N_HEADS * seq_len
    qrow = lax.broadcasted_iota(jnp.int32, (hq, WINDOW), 0) % seq_len
    ccol = lax.broadcasted_iota(jnp.int32, (hq, WINDOW), 1)
    cache_valid = ccol > qrow
    nrow = lax.broadcasted_iota(jnp.int32, (hq, seq_len), 0) % seq_len
    ncol = lax.broadcasted_iota(jnp.int32, (hq, seq_len), 1)
    new_valid = ncol <= nrow
    oc_all, od_all = [], []
    for g in range(group):
        rs = slice(g * seq_len, (g + 1) * seq_len)
        kc = kc_ref[g]
        vc = vc_ref[g]
        qst = _stack_swa_q(qa[rs], qb[rs]).astype(bf16)
        s_c = jnp.where(cache_valid, _dot_nt(qst, kc.astype(bf16)) * QK_SCALE, NEG)
        s_n = jnp.where(new_valid, _dot_nt(qst, kr[rs].astype(bf16)) * QK_SCALE, NEG)
        mx = jnp.maximum(jnp.maximum(jnp.max(s_c, axis=1, keepdims=True), jnp.max(s_n, axis=1, keepdims=True)), sink)
        e_c = jnp.exp(s_c - mx)
        e_n = jnp.exp(s_n - mx)
        den = jnp.sum(e_c, axis=1, keepdims=True) + jnp.sum(e_n, axis=1, keepdims=True) + jnp.exp(sink - mx)
        ov = (_dot(e_c.astype(bf16), vc.astype(bf16)) + _dot(e_n.astype(bf16), vx[rs].astype(bf16))) / den
        oc_all.append(_unstack_swa(ov, seq_len))
        swak_out[g, 0:WINDOW - seq_len, :] = kc[seq_len:WINDOW, :]
        swak_out[g, WINDOW - seq_len:WINDOW, :] = kr[rs]
        swav_out[g, 0:WINDOW - seq_len, :] = vc[seq_len:WINDOW, :]
        swav_out[g, WINDOW - seq_len:WINDOW, :] = vx[rs]

        sm = _dot_nt(_stack_heads(qm[rs]).astype(bf16), mkc_ref[g].astype(bf16)) * QK_SCALE
        mm = jnp.max(sm, axis=1, keepdims=True)
        em = jnp.exp(sm - mm)
        dm = jnp.sum(em, axis=1, keepdims=True)
        od_all.append(_unstack_heads(_dot(em.astype(bf16), mvc_ref[g].astype(bf16)) / dm, seq_len))
    oc = jnp.concatenate(oc_all, axis=0)
    od = jnp.concatenate(od_all, axis=0)
    mix_ref[:, 2 * GROUP_W:3 * GROUP_W] = (oc * _silu(p_ref[:, C_SZ:C_SZ + GROUP_W])).astype(bf16)
    mix_ref[:, 3 * GROUP_W:4 * GROUP_W] = (od * _silu(p_ref[:, C_MZ:C_MZ + GROUP_W])).astype(bf16)


def _sample_mix(p, cos, sin, convb, sgla, kc, vc, mkc, mvc, convw, wg, bg, gnorm, sinks, jmat, seq_len, group):
    nseq = sgla.shape[0]
    assert nseq % group == 0 and seq_len == SUBLANES
    n = group * seq_len
    c2 = lambda shape: pl.BlockSpec(shape, lambda i: (0, 0))
    in_specs = [
        pl.BlockSpec(memory_space=pltpu.SMEM),
        pl.BlockSpec((n, NP), lambda i: (i, 0)),
        c2((n, LANES)), c2((n, LANES)),
        pl.BlockSpec((n, GROUP_W), lambda i: (i, 0)),
        pl.BlockSpec((group, N_HEADS, HEAD_D, HEAD_D), lambda i: (i, 0, 0, 0)),
        pl.BlockSpec((group, WINDOW, SWA_KV_W), lambda i: (i, 0, 0)),
        pl.BlockSpec((group, WINDOW, SWA_KV_W), lambda i: (i, 0, 0)),
        pl.BlockSpec((group, N_MEM, GROUP_W), lambda i: (i, 0, 0)),
        pl.BlockSpec((group, N_MEM, GROUP_W), lambda i: (i, 0, 0)),
        c2((CONV_W, GROUP_W)), c2((LANES, GROUP_W)), c2((1, GROUP_W)), c2((1, GROUP_W)), c2((GROUP_W, GROUP_W)),
    ]
    out_shape = (
        jax.ShapeDtypeStruct((nseq * seq_len, D_MODEL), bf16),
        jax.ShapeDtypeStruct((nseq, CONV_W - 1, GROUP_W), f32),
        jax.ShapeDtypeStruct((nseq, N_HEADS, HEAD_D, HEAD_D), f32),
        jax.ShapeDtypeStruct((nseq, WINDOW, SWA_KV_W), f32),
        jax.ShapeDtypeStruct((nseq, WINDOW, SWA_KV_W), f32),
    )
    out_specs = (
        pl.BlockSpec((n, D_MODEL), lambda i: (i, 0)),
        pl.BlockSpec((group, CONV_W - 1, GROUP_W), lambda i: (i, 0, 0)),
        pl.BlockSpec((group, N_HEADS, HEAD_D, HEAD_D), lambda i: (i, 0, 0, 0)),
        pl.BlockSpec((group, WINDOW, SWA_KV_W), lambda i: (i, 0, 0)),
        pl.BlockSpec((group, WINDOW, SWA_KV_W), lambda i: (i, 0, 0)),
    )
    return pl.pallas_call(
        functools.partial(_sample_mix_kernel, seq_len, group),
        grid=(nseq // group,),
        in_specs=in_specs,
        out_specs=out_specs,
        out_shape=out_shape,
        compiler_params=pltpu.CompilerParams(dimension_semantics=("arbitrary",),
                                             vmem_limit_bytes=VMEM_LIMIT_BYTES),
        name="sample_mix",
    )(sinks, p, cos, sin, convb, sgla, kc, vc, mkc, mvc, convw, wg, bg, gnorm, jmat)


def _permute_heads(w, axis):
    parts = [lax.slice_in_dim(w, h * HEAD_D, (h + 1) * HEAD_D, axis=axis) for h in _HEAD_PERM]
    return jnp.concatenate(parts, axis=axis)


def _layout_w_in(w_in):
    sl = lambda a, b: w_in[..., a:b]
    parts = [sl(0, _O_GLR), sl(_O_GZ, _O_SQ), _permute_heads(sl(_O_SQ, _O_SK), 2), sl(_O_SK, _O_SV), sl(_O_SV, _O_SZ),
             _permute_heads(sl(_O_SZ, _O_MQ), 2), sl(_O_MQ, _O_MZ), sl(_O_MZ, _O_END), sl(_O_GLR, _O_GZ)]
    w = jnp.concatenate(parts, axis=-1)
    pad = NP - w.shape[-1]
    return jnp.pad(w, ((0, 0), (0, 0), (0, pad))).astype(bf16)


def _layout_w_out(w_out):
    parts = [w_out[:, 0:2 * GROUP_W], _permute_heads(w_out[:, 2 * GROUP_W:3 * GROUP_W], 1), w_out[:, 3 * GROUP_W:]]
    return jnp.concatenate(parts, axis=1).astype(bf16)


def _rope_tables(pos):
    half = HEAD_D // 2
    inv = jnp.power(ROPE_THETA, -jnp.arange(half, dtype=f32) / half)
    ang = pos.astype(f32)[:, None] * inv[None, :]
    cos = jnp.tile(jnp.cos(ang), (1, LANES // half))
    sin = jnp.sin(ang)
    sin_signed = jnp.tile(jnp.concatenate([-sin, sin], axis=1), (1, LANES // HEAD_D))
    return cos, sin_signed


def kernel(x_prompt, x_sample, state_conv, state_gla, cache_swa_k, cache_swa_v, cache_mem_k, cache_mem_v,
           mem_prompt, norm_pre, norm_post, w_in, conv_w, gla_w_gate, gla_b_gate, gla_norm, swa_sinks,
           w_mem_kv, w_out):
    depth = w_in.shape[0]
    B, L, _ = x_prompt.shape
    nseq, seq_len, _ = x_sample.shape
    group = 8

    win = _layout_w_in(w_in)
    wout = _layout_w_out(w_out)
    wmem = w_mem_kv.astype(bf16)
    wg = jnp.pad(gla_w_gate, ((0, 0), (0, LANES - GATE_RANK), (0, 0))).astype(bf16)
    head_id = np.arange(GROUP_W) // HEAD_D
    jmat = jnp.asarray(head_id[:, None] == head_id[None, :], dtype=bf16)
    lvl, swab = _subtile_constants()
    sinks = swa_sinks.astype(f32)
    gpre = norm_pre[:, None, :]
    gpost = norm_post[:, None, :]
    bg = gla_b_gate[:, None, :]
    gn = gla_norm[:, None, :]

    cos_p, sin_p = _rope_tables(jnp.arange(L, dtype=jnp.int32))
    cos_p = _to_kernel_order(cos_p, 0)
    sin_p = _to_kernel_order(sin_p, 0)
    cos_s, sin_s = _rope_tables(PAST_LEN + jnp.arange(seq_len, dtype=jnp.int32))
    cos_s = jnp.tile(cos_s, (group, 1))
    sin_s = jnp.tile(sin_s, (group, 1))

    convb = jnp.pad(state_conv, ((0, 0), (0, 0), (0, seq_len - (CONV_W - 1)), (0, 0)))
    convb = convb.reshape(depth, nseq * seq_len, GROUP_W)
    kc = cache_swa_k.reshape(depth, nseq, WINDOW, SWA_KV_W)
    vc = cache_swa_v.reshape(depth, nseq, WINDOW, SWA_KV_W)
    mkc = cache_mem_k.reshape(depth, nseq, N_MEM, GROUP_W)
    mvc = cache_mem_v.reshape(depth, nseq, N_MEM, GROUP_W)

    hp = _to_kernel_order(x_prompt, 1)
    hs = x_sample.reshape(nseq * seq_len, D_MODEL)
    outs_p = [[] for _ in range(6)]
    outs_s = [[] for _ in range(4)]
    for l in range(depth):
        res = _prompt_layer(l, hp, mem_prompt, cos_p, sin_p, gpre, gpost, win, wout, wmem, conv_w, wg, bg, gn,
                            sinks, jmat, lvl, swab)
        hp = res[0]
        for i in range(6):
            outs_p[i].append(res[i + 1])
        ps = _in_proj(hs, gpre[l], win[l])
        mix, c2, g2, k2, v2 = _sample_mix(ps, cos_s, sin_s, convb[l], state_gla[l], kc[l], vc[l], mkc[l], mvc[l],
                                          conv_w[l], wg[l], bg[l], gn[l], sinks[l], jmat, seq_len, group)
        hs = _out_proj(hs, mix, gpost[l], wout[l])
        for i, a in enumerate((c2, g2, k2, v2)):
            outs_s[i].append(a)

    kv_shape_p = (depth, B, WINDOW, 2, HEAD_D)
    mem_shape = (depth, B, N_MEM, N_HEADS, HEAD_D)
    kv_shape_s = (depth, nseq, WINDOW, 2, HEAD_D)
    return (_from_kernel_order(hp, 1), hs.reshape(nseq, seq_len, D_MODEL),
            jnp.stack(outs_p[0]), jnp.stack(outs_p[1]),
            jnp.stack(outs_p[2]).reshape(kv_shape_p), jnp.stack(outs_p[3]).reshape(kv_shape_p),
            jnp.stack(outs_p[4]).reshape(mem_shape), jnp.stack(outs_p[5]).reshape(mem_shape),
            jnp.stack(outs_s[0]), jnp.stack(outs_s[1]),
            jnp.stack(outs_s[2]).reshape(kv_shape_s), jnp.stack(outs_s[3]).reshape(kv_shape_s))
```

```python
import functools

import jax
import jax.numpy as jnp
import numpy as np
from jax import lax
from jax.experimental import pallas as pl
from jax.experimental.pallas import tpu as pltpu

f32 = jnp.float32
bf16 = jnp.bfloat16

D_MODEL = 1024
GROUP_W = 256
HEAD_D = 64
N_HEADS = 4
SWA_KV_W = 128
N_MEM = 256
WINDOW = 128
CONV_W = 3
GATE_RANK = 16
GATE_NORM = 16.0
GLA_BLOCK = 16
ROPE_THETA = 10000.0
PAST_LEN = 8192
EPS = 1e-6
NEG = -1e30
QK_SCALE = HEAD_D ** -0.5

LANES = 128
SUBLANES = 8
VMEM_LIMIT_BYTES = 56 * 1024 * 1024

C_CX, C_CB, C_CC, C_CZ = 0, 256, 512, 768
C_GQ, C_GK, C_GV, C_GZ = 1024, 1280, 1536, 1792
C_SQ, C_SK, C_SV, C_SZ = 2048, 2304, 2432, 2560
C_MQ, C_MZ = 2816, 3072
C_LR = 3328
NP = 3456
IN_WIDTH = 3344
_O_GLR, _O_GZ = 1792, 1808

SUB = 128
TL = 512


def _dot(a, b):
    return jnp.dot(a, b, preferred_element_type=f32)


def _dot_nt(a, b):
    return lax.dot_general(a, b, (((1,), (1,)), ((), ())), preferred_element_type=f32)


def _dot_tn(a, b):
    return lax.dot_general(a, b, (((0,), (0,)), ((), ())), preferred_element_type=f32)


def _rmsnorm(x, g):
    return x * lax.rsqrt(jnp.mean(x * x, axis=-1, keepdims=True) + EPS) * g


def _silu(z):
    return z * (0.5 + 0.5 * jnp.tanh(0.5 * z))


def _log_sigmoid(x):
    return jnp.minimum(x, 0.0) - jnp.log1p(jnp.exp(-jnp.abs(x)))


def _project(h, wt_ref, p_ref, rows):
    p_ref[rows, 0:_O_GLR] = _dot_nt(h, wt_ref[0:_O_GLR, :])
    p_ref[rows, _O_GLR:C_LR] = _dot_nt(h, wt_ref[_O_GZ:IN_WIDTH, :])
    p_ref[rows, C_LR:C_LR + GATE_RANK] = _dot_nt(h, wt_ref[_O_GLR:_O_GZ, :])


def _head_sum(x, j_ref):
    hi = x.astype(bf16)
    lo = (x - hi.astype(f32)).astype(bf16)
    return _dot(hi, j_ref[...]) + _dot(lo, j_ref[...])


def _rope(x, cos, sin_signed):
    lane = lax.broadcasted_iota(jnp.int32, x.shape, 1)
    swapped = jnp.where((lane % HEAD_D) < HEAD_D // 2,
                        pltpu.roll(x, LANES - HEAD_D // 2, 1), pltpu.roll(x, HEAD_D // 2, 1))
    return x * cos + swapped * sin_signed


def _stack_heads(x):
    lane_head = lax.broadcasted_iota(jnp.int32, x.shape, 1) // HEAD_D
    return jnp.concatenate([jnp.where(lane_head == h, x, 0.0) for h in range(N_HEADS)], axis=0)


def _unstack_heads(o, n):
    lane_head = lax.broadcasted_iota(jnp.int32, (n, GROUP_W), 1) // HEAD_D
    out = o[0:n]
    for h in range(1, N_HEADS):
        out = jnp.where(lane_head == h, o[h * n:(h + 1) * n], out)
    return out


def _stack_swa_q(q01, q23):
    low = lax.broadcasted_iota(jnp.int32, q01.shape, 1) < HEAD_D
    return jnp.concatenate([jnp.where(low, q01, 0.0), jnp.where(low, pltpu.roll(q01, HEAD_D, 1), 0.0),
                            jnp.where(low, 0.0, pltpu.roll(q23, HEAD_D, 1)), jnp.where(low, 0.0, q23)], axis=0)


def _unstack_swa(o, n):
    low = lax.broadcasted_iota(jnp.int32, (n, SWA_KV_W), 1) < HEAD_D
    c01 = jnp.where(low, o[0:n], pltpu.roll(o[n:2 * n], HEAD_D, 1))
    c23 = jnp.where(low, pltpu.roll(o[2 * n:3 * n], HEAD_D, 1), o[3 * n:4 * n])
    return jnp.concatenate([c01, c23], axis=1)


def _sink_column(sinks, n):
    return jnp.concatenate([jnp.full((n, 1), s, f32) for s in sinks], axis=0)


def _block_cumsum(la, block):
    row = lax.broadcasted_iota(jnp.int32, la.shape, 0) % block
    b = la
    s = 1
    while s < block:
        b = b + jnp.where(row >= s, pltpu.roll(b, s, 0), 0.0)
        s *= 2
    return b


def _gate_log_decay(p_lr, wg_ref, bg_ref):
    pre = _dot(p_lr.astype(bf16), wg_ref[...]) + bg_ref[...]
    return _log_sigmoid(pre) * (1.0 / GATE_NORM)


def _gla_norm_gate(o, gz, gnorm_ref, j_ref):
    ms = _head_sum(o * o, j_ref) * (1.0 / HEAD_D)
    return o * lax.rsqrt(ms + EPS) * gnorm_ref[...] * _silu(gz)


GROUPS = SUB // SUBLANES
LEVEL_GROUPS = (1, 2, 4)


def _row_groups(x):
    return [x[SUBLANES * r:SUBLANES * (r + 1), :] for r in range(x.shape[0] // SUBLANES)]


def _prompt_layer_kernel(sinks_ref, x_ref, memp_ref, cos_ref, sin_ref, gpre_ref, gpost_ref, wt_ref,
                         wout_ref, wmem_ref, convw_ref, wg_ref, bg_ref, gnorm_ref, j_ref, lvl_ref, swab_ref,
                         y_ref, conv_out, gla_out, swak_out, swav_out, mk_out, mv_out,
                         p_scr, mix_scr, cbuf, st_scr, kprev, vprev, mkb, mvb, kv_nat):
    t = pl.program_id(1)
    nt = pl.num_programs(1)
    nsub = TL // SUB

    @pl.when(t == 0)
    def _init():
        cbuf[...] = jnp.zeros((SUBLANES, GROUP_W), f32)
        st_scr[...] = jnp.zeros((GROUP_W, GROUP_W), f32)
        kprev[...] = jnp.zeros((SUB, SWA_KV_W), f32)
        vprev[...] = jnp.zeros((SUB, SWA_KV_W), f32)
        mkv = _dot(memp_ref[0].astype(bf16), wmem_ref[...])
        mk_out[0] = mkv[:, 0:GROUP_W].T
        mv_out[0] = mkv[:, GROUP_W:2 * GROUP_W].T
        mkb[...] = mkv[:, 0:GROUP_W].astype(bf16)
        mvb[...] = mkv[:, GROUP_W:2 * GROUP_W].astype(bf16)

    x = x_ref[0]
    h = _rmsnorm(x, gpre_ref[...]).astype(bf16)
    _project(h, wt_ref, p_scr, slice(None))

    sub8 = lax.broadcasted_iota(jnp.int32, (SUBLANES, GROUP_W), 0)

    def sub_tile(j, carry):
        r0 = pl.multiple_of(j * SUB, SUB)
        rows = pl.ds(r0, SUB)

        u = p_scr[rows, C_CC:C_CC + GROUP_W] * p_scr[rows, C_CX:C_CX + GROUP_W]
        last = SUB - SUBLANES
        prev1 = jnp.where(sub8 == 0, cbuf[0:1, :], pltpu.roll(u[last:SUB, :], 1, 0))
        prev2 = jnp.where(sub8 == 0, cbuf[1:2, :], pltpu.roll(u[last - SUBLANES:last, :], 1, 0))
        um1 = jnp.concatenate([prev1, u[0:last, :]], axis=0)
        um2 = jnp.concatenate([prev2, prev1, u[0:last - SUBLANES, :]], axis=0)
        cy = convw_ref[0:1, :] * um2 + convw_ref[1:2, :] * um1 + convw_ref[2:3, :] * u
        cbuf[0:1, :] = u[SUB - 1:SUB, :]
        cbuf[1:2, :] = u[last - 1:last, :]
        a_out = p_scr[rows, C_CB:C_CB + GROUP_W] * cy * _silu(p_scr[rows, C_CZ:C_CZ + GROUP_W])
        mix_scr[rows, 0:GROUP_W] = a_out.astype(bf16)

        qg = _row_groups(p_scr[rows, C_GQ:C_GQ + GROUP_W] * QK_SCALE)
        k = p_scr[rows, C_GK:C_GK + GROUP_W]
        v = p_scr[rows, C_GV:C_GV + GROUP_W]
        kg = _row_groups(k)
        vg = _row_groups(v)
        lag = _row_groups(_gate_log_decay(p_scr[rows, C_LR:C_LR + GATE_RANK], wg_ref, bg_ref))
        bg_ = [lag[0]]
        for r in range(1, GROUPS):
            bg_.append(bg_[-1] + lag[r])
        tot = bg_[GROUPS - 1]
        pw = []
        for r in range(GROUPS):
            for s in range(r):
                pw.append(qg[r] * kg[s] * jnp.exp(bg_[r] - bg_[s]))
            pw.append(qg[r] * kg[r])
        scores = _dot(jnp.concatenate(pw, axis=0).astype(bf16), j_ref[...])
        og = []
        idx = 0
        for r in range(GROUPS):
            acc = None
            for s in range(r + 1):
                term = scores[SUBLANES * idx:SUBLANES * (idx + 1), :] * vg[s]
                acc = term if acc is None else acc + term
                idx += 1
            og.append(acc)
        o = jnp.concatenate(og, axis=0)

        def decayed(group):
            before = jnp.zeros((SUBLANES, GROUP_W), f32)
            after = jnp.zeros((SUBLANES, GROUP_W), f32)
            for s in range(1, group):
                before = before + jnp.where(sub8 % group >= s, pltpu.roll(tot, s, 0), 0.0)
                after = after + jnp.where(sub8 % group < group - s, pltpu.roll(tot, SUBLANES - s, 0), 0.0)
            qd = jnp.concatenate([qg[r] * jnp.exp(bg_[r] + before) for r in range(GROUPS)], axis=0)
            kd = jnp.concatenate([kg[r] * jnp.exp((tot - bg_[r]) + after) for r in range(GROUPS)], axis=0)
            return qd, kd

        attn = None
        for li, group in enumerate(LEVEL_GROUPS):
            qd, kd = decayed(group)
            s = _dot_nt(_stack_heads(qd).astype(bf16), kd.astype(bf16)) * lvl_ref[li]
            attn = s if attn is None else attn + s
        o = o + _unstack_heads(_dot(attn.astype(bf16), v.astype(bf16)), SUB)
        qd, kd = decayed(SUB // GLA_BLOCK)
        st = st_scr[...]
        o = o + _dot_nt(qd.astype(bf16), st.astype(bf16))
        total = jnp.sum(tot, axis=0, keepdims=True)
        upd = _dot_tn(v.astype(bf16), kd.astype(bf16))
        same_head = (lax.broadcasted_iota(jnp.int32, (GROUP_W, GROUP_W), 0) // HEAD_D
                     == lax.broadcasted_iota(jnp.int32, (GROUP_W, GROUP_W), 1) // HEAD_D)
        st_scr[...] = st * jnp.exp(total) + jnp.where(same_head, upd, 0.0)
        b_out = _gla_norm_gate(o, p_scr[rows, C_GZ:C_GZ + GROUP_W], gnorm_ref, j_ref)
        mix_scr[rows, GROUP_W:2 * GROUP_W] = b_out.astype(bf16)

        cos = cos_ref[rows, :]
        sin = sin_ref[rows, :]
        q01 = _rope(p_scr[rows, C_SQ:C_SQ + LANES], cos, sin) * QK_SCALE
        q23 = _rope(p_scr[rows, C_SQ + LANES:C_SQ + 2 * LANES], cos, sin) * QK_SCALE
        kr = _rope(p_scr[rows, C_SK:C_SK + SWA_KV_W], cos, sin)
        vx = p_scr[rows, C_SV:C_SV + SWA_KV_W]
        k_all = jnp.concatenate([kprev[...], kr], axis=0).astype(bf16)
        v_all = jnp.concatenate([vprev[...], vx], axis=0).astype(bf16)
        first = jnp.where((t * nsub + j) == 0, 1, 0)
        sc = _dot_nt(_stack_swa_q(q01, q23).astype(bf16), k_all) + swab_ref[first]
        sink = _sink_column([sinks_ref[hh] for hh in range(N_HEADS)], SUB)
        mx = jnp.maximum(jnp.max(sc, axis=1, keepdims=True), sink)
        e = jnp.exp(sc - mx)
        den = jnp.sum(e, axis=1, keepdims=True) + jnp.exp(sink - mx)
        oc = _unstack_swa(_dot(e.astype(bf16), v_all) * (1.0 / den), SUB)
        mix_scr[rows, 2 * GROUP_W:3 * GROUP_W] = (oc * _silu(p_scr[rows, C_SZ:C_SZ + GROUP_W])).astype(bf16)
        kprev[...] = kr
        vprev[...] = vx

        qm = _stack_heads(p_scr[rows, C_MQ:C_MQ + GROUP_W] * QK_SCALE).astype(bf16)
        sm = _dot_nt(qm, mkb[...])
        mm = jnp.max(sm, axis=1, keepdims=True)
        em = jnp.exp(sm - mm)
        dm = jnp.sum(em, axis=1, keepdims=True)
        od = _unstack_heads(_dot(em.astype(bf16), mvb[...]) * (1.0 / dm), SUB)
        mix_scr[rows, 3 * GROUP_W:4 * GROUP_W] = (od * _silu(p_scr[rows, C_MZ:C_MZ + GROUP_W])).astype(bf16)
        return carry

    lax.fori_loop(0, nsub, sub_tile, 0)

    mo = _dot(mix_scr[...], wout_ref[...])
    y_ref[0] = x + _rmsnorm(mo, gpost_ref[...])

    @pl.when(t == nt - 1)
    def _final():
        conv_out[0, 0:1, :] = cbuf[1:2, :]
        conv_out[0, 1:2, :] = cbuf[0:1, :]
        for src, dst in ((kprev, swak_out), (vprev, swav_out)):
            for r in range(GROUPS):
                kv_nat[pl.ds(r, SUBLANES, stride=GLA_BLOCK), :] = src[SUBLANES * r:SUBLANES * (r + 1), :]
            dst[0] = kv_nat[...].T
        st = st_scr[...]
        for hh in range(N_HEADS):
            gla_out[0, hh] = st[HEAD_D * hh:HEAD_D * (hh + 1), HEAD_D * hh:HEAD_D * (hh + 1)].T


def _layer_spec(shape, l):
    nd = len(shape)
    return pl.BlockSpec((None,) + tuple(shape), lambda b, t, _l=l, _nd=nd: (_l,) + (0,) * _nd)


def _const_spec(shape):
    nd = len(shape)
    return pl.BlockSpec(shape, lambda b, t, _nd=nd: (0,) * _nd)


def _prompt_layer(l, x, memp, cos, sin, gpre, gpost, wt, wout, wmem, convw, wg, bg, gnorm, sinks, jmat,
                  lvl, swab):
    B, L, _ = x.shape
    assert L % TL == 0
    nt = L // TL
    out_shape = (
        jax.ShapeDtypeStruct((B, L, D_MODEL), f32),
        jax.ShapeDtypeStruct((B, CONV_W - 1, GROUP_W), f32),
        jax.ShapeDtypeStruct((B, N_HEADS, HEAD_D, HEAD_D), f32),
        jax.ShapeDtypeStruct((B, SWA_KV_W, WINDOW), f32),
        jax.ShapeDtypeStruct((B, SWA_KV_W, WINDOW), f32),
        jax.ShapeDtypeStruct((B, GROUP_W, N_MEM), f32),
        jax.ShapeDtypeStruct((B, GROUP_W, N_MEM), f32),
    )
    in_specs = [
        pl.BlockSpec(memory_space=pltpu.SMEM),
        pl.BlockSpec((1, TL, D_MODEL), lambda b, t: (b, t, 0)),
        pl.BlockSpec((1, N_MEM, D_MODEL), lambda b, t: (b, 0, 0)),
        pl.BlockSpec((TL, LANES), lambda b, t: (t, 0)),
        pl.BlockSpec((TL, LANES), lambda b, t: (t, 0)),
        _layer_spec((1, D_MODEL), l), _layer_spec((1, D_MODEL), l),
        _layer_spec((IN_WIDTH, D_MODEL), l),
        _layer_spec((D_MODEL, D_MODEL), l), _layer_spec((D_MODEL, 2 * GROUP_W), l),
        _layer_spec((CONV_W, GROUP_W), l), _layer_spec((GATE_RANK, GROUP_W), l), _layer_spec((1, GROUP_W), l),
        _layer_spec((1, GROUP_W), l), _const_spec((GROUP_W, GROUP_W)),
        _const_spec((len(LEVEL_GROUPS), N_HEADS * SUB, SUB)), _const_spec((2, N_HEADS * SUB, 2 * SUB)),
    ]
    out_specs = (
        pl.BlockSpec((1, TL, D_MODEL), lambda b, t: (b, t, 0)),
        pl.BlockSpec((1, CONV_W - 1, GROUP_W), lambda b, t: (b, 0, 0)),
        pl.BlockSpec((1, N_HEADS, HEAD_D, HEAD_D), lambda b, t: (b, 0, 0, 0)),
        pl.BlockSpec((1, SWA_KV_W, WINDOW), lambda b, t: (b, 0, 0)),
        pl.BlockSpec((1, SWA_KV_W, WINDOW), lambda b, t: (b, 0, 0)),
        pl.BlockSpec((1, GROUP_W, N_MEM), lambda b, t: (b, 0, 0)),
        pl.BlockSpec((1, GROUP_W, N_MEM), lambda b, t: (b, 0, 0)),
    )
    scratch = [
        pltpu.VMEM((TL, NP), f32),
        pltpu.VMEM((TL, D_MODEL), bf16),
        pltpu.VMEM((SUBLANES, GROUP_W), f32),
        pltpu.VMEM((GROUP_W, GROUP_W), f32),
        pltpu.VMEM((SUB, SWA_KV_W), f32),
        pltpu.VMEM((SUB, SWA_KV_W), f32),
        pltpu.VMEM((N_MEM, GROUP_W), bf16),
        pltpu.VMEM((N_MEM, GROUP_W), bf16),
        pltpu.VMEM((WINDOW, SWA_KV_W), f32),
    ]
    return pl.pallas_call(
        _prompt_layer_kernel,
        grid=(B, nt),
        in_specs=in_specs,
        out_specs=out_specs,
        out_shape=out_shape,
        scratch_shapes=scratch,
        compiler_params=pltpu.CompilerParams(dimension_semantics=("arbitrary", "arbitrary"),
                                             vmem_limit_bytes=VMEM_LIMIT_BYTES),
        name="prompt_layer",
    )(sinks[l], x, memp, cos, sin, gpre, gpost, wt, wout, wmem, convw, wg, bg, gnorm, jmat, lvl, swab)


def _subtile_constants():
    row = np.arange(SUB)
    tok = (row % SUBLANES) * GLA_BLOCK + row // SUBLANES
    blk = row % SUBLANES
    levels = []
    for group in LEVEL_GROUPS:
        g = blk // group
        sel = (g[:, None] == g[None, :] + 1) & (g[:, None] % 2 == 1)
        levels.append(np.tile(sel, (N_HEADS, 1)))
    key_tok = np.concatenate([tok - SUB, tok])
    valid = (key_tok[None, :] <= tok[:, None]) & (key_tok[None, :] > tok[:, None] - WINDOW)
    bias = [np.where(valid, 0.0, NEG), np.where(valid & (key_tok[None, :] >= 0), 0.0, NEG)]
    bias = np.stack([np.tile(b, (N_HEADS, 1)) for b in bias])
    return jnp.asarray(np.stack(levels), dtype=f32), jnp.asarray(bias, dtype=f32)


def _to_kernel_order(x, axis):
    shp = x.shape
    n = shp[axis]
    x = x.reshape(shp[:axis] + (n // SUB, SUBLANES, GLA_BLOCK) + shp[axis + 1:])
    return jnp.swapaxes(x, axis + 1, axis + 2).reshape(shp)


def _from_kernel_order(x, axis):
    shp = x.shape
    n = shp[axis]
    x = x.reshape(shp[:axis] + (n // SUB, GLA_BLOCK, SUBLANES) + shp[axis + 1:])
    return jnp.swapaxes(x, axis + 1, axis + 2).reshape(shp)


def _sample_kernel(seq_len, group, sinks_ref, x_hbm, cos_ref, sin_ref, convb_ref, sgla_ref, kc_ref, vc_ref,
                   mkc_ref, mvc_ref, gpre_ref, gpost_ref, wt_ref, wout_ref, convw_ref, wg_ref, bg_ref,
                   gnorm_ref, j_ref, place_ref,
                   y_hbm, conv_out, gla_out, swak_out, swav_out,
                   hs, p_scr, mix_scr, sem):
    l = pl.program_id(0)
    g_step = pl.program_id(1)
    n_layers = pl.num_programs(0)
    n_steps = pl.num_programs(1)
    n_tok = hs.shape[0]
    n = group * seq_len
    chunks = [slice(c * TL, (c + 1) * TL) for c in range(n_tok // TL)]

    @pl.when((l == 0) & (g_step == 0))
    def _load():
        cp = pltpu.make_async_copy(x_hbm, hs, sem.at[0])
        cp.start()
        cp.wait()

    @pl.when(g_step == 0)
    def _project_all():
        for rows in chunks:
            h = _rmsnorm(hs[rows, :], gpre_ref[...]).astype(bf16)
            _project(h, wt_ref, p_scr, rows)

    rows = pl.ds(pl.multiple_of(g_step * n, n), n)
    row_t = lax.broadcasted_iota(jnp.int32, (n, GROUP_W), 0) % seq_len

    u = p_scr[rows, C_CC:C_CC + GROUP_W] * p_scr[rows, C_CX:C_CX + GROUP_W]
    hist = convb_ref[...]
    um1 = jnp.where(row_t >= 1, pltpu.roll(u, 1, 0), pltpu.roll(hist, n - 1, 0))
    um2 = jnp.where(row_t >= 2, pltpu.roll(u, 2, 0), hist)
    cy = convw_ref[0:1, :] * um2 + convw_ref[1:2, :] * um1 + convw_ref[2:3, :] * u
    a_out = p_scr[rows, C_CB:C_CB + GROUP_W] * cy * _silu(p_scr[rows, C_CZ:C_CZ + GROUP_W])
    mix_scr[rows, 0:GROUP_W] = a_out.astype(bf16)
    for g in range(group):
        conv_out[g] = u[g * seq_len + seq_len - (CONV_W - 1):(g + 1) * seq_len, :]

    qs = p_scr[rows, C_GQ:C_GQ + GROUP_W] * QK_SCALE
    k = p_scr[rows, C_GK:C_GK + GROUP_W]
    v = p_scr[rows, C_GV:C_GV + GROUP_W]
    la = _gate_log_decay(p_scr[rows, C_LR:C_LR + GATE_RANK], wg_ref, bg_ref)
    bc = _block_cumsum(la, seq_len)
    o = _dot((qs * k).astype(bf16), j_ref[...]) * v
    for d in range(1, seq_len):
        dec = jnp.exp(jnp.where(row_t >= d, bc - pltpu.roll(bc, d, 0), NEG))
        pw = qs * pltpu.roll(k, d, 0) * dec
        o = o + _dot(pw.astype(bf16), j_ref[...]) * pltpu.roll(v, d, 0)
    qd = qs * jnp.exp(bc)
    tot = [bc[(g + 1) * seq_len - 1:(g + 1) * seq_len, :] for g in range(group)]
    kd = k * jnp.exp(jnp.concatenate(
        [tot[g] - bc[g * seq_len:(g + 1) * seq_len, :] for g in range(group)], axis=0))
    eye = (lax.broadcasted_iota(jnp.int32, (HEAD_D, HEAD_D), 0)
           == lax.broadcasted_iota(jnp.int32, (HEAD_D, HEAD_D), 1))
    o_inter = []
    for g in range(group):
        rs = slice(g * seq_len, (g + 1) * seq_len)
        alpha = jnp.exp(tot[g])
        per_head = []
        for hh in range(N_HEADS):
            ls = slice(hh * HEAD_D, (hh + 1) * HEAD_D)
            s0 = sgla_ref[g, hh]
            per_head.append(_dot(qd[rs, ls].astype(bf16), s0.astype(bf16)))
            alpha_col = jnp.sum(jnp.where(eye, alpha[:, ls], 0.0), axis=1, keepdims=True)
            gla_out[g, hh] = alpha_col * s0 + _dot_tn(kd[rs, ls].astype(bf16), v[rs, ls].astype(bf16))
        o_inter.append(jnp.concatenate(per_head, axis=1))
    o = o + jnp.concatenate(o_inter, axis=0)
    b_out = _gla_norm_gate(o, p_scr[rows, C_GZ:C_GZ + GROUP_W], gnorm_ref, j_ref)
    mix_scr[rows, GROUP_W:2 * GROUP_W] = b_out.astype(bf16)

    cos = cos_ref[...]
    sin = sin_ref[...]
    q01 = _rope(p_scr[rows, C_SQ:C_SQ + LANES], cos, sin) * QK_SCALE
    q23 = _rope(p_scr[rows, C_SQ + LANES:C_SQ + 2 * LANES], cos, sin) * QK_SCALE
    kr = _rope(p_scr[rows, C_SK:C_SK + SWA_KV_W], cos, sin)
    vx = p_scr[rows, C_SV:C_SV + SWA_KV_W]
    qm = p_scr[rows, C_MQ:C_MQ + GROUP_W] * QK_SCALE
    sink = _sink_column([sinks_ref[l, hh] for hh in range(N_HEADS)], seq_len)
    hq = N_HEADS * seq_len
    qrow = lax.broadcasted_iota(jnp.int32, (hq, WINDOW), 0) % seq_len
    ccol = lax.broadcasted_iota(jnp.int32, (hq, WINDOW), 1)
    cache_valid = ccol > qrow
    nrow = lax.broadcasted_iota(jnp.int32, (hq, seq_len), 0) % seq_len
    ncol = lax.broadcasted_iota(jnp.int32, (hq, seq_len), 1)
    new_valid = ncol <= nrow
    keep_old = lax.broadcasted_iota(jnp.int32, (SWA_KV_W, WINDOW), 1) < WINDOW - seq_len

    def exact_split(a):
        hi = a.astype(bf16)
        r1 = a - hi.astype(f32)
        mid = r1.astype(bf16)
        lo = (r1 - mid.astype(f32)).astype(bf16)
        return jnp.concatenate([hi, mid, lo], axis=0)

    def shifted_cache(old_t, new_rows):
        placed = _dot_tn(exact_split(new_rows), place_ref[...])
        return jnp.where(keep_old, pltpu.roll(old_t, WINDOW - seq_len, 1), placed)

    oc_all, od_all = [], []
    for g in range(group):
        rs = slice(g * seq_len, (g + 1) * seq_len)
        kc = kc_ref[g]
        vc = vc_ref[g]
        qst = _stack_swa_q(q01[rs], q23[rs]).astype(bf16)
        s_c = jnp.where(cache_valid, _dot(qst, kc.astype(bf16)), NEG)
        s_n = jnp.where(new_valid, _dot_nt(qst, kr[rs].astype(bf16)), NEG)
        mx = jnp.maximum(jnp.maximum(jnp.max(s_c, axis=1, keepdims=True), jnp.max(s_n, axis=1, keepdims=True)), sink)
        e_c = jnp.exp(s_c - mx)
        e_n = jnp.exp(s_n - mx)
        den = jnp.sum(e_c, axis=1, keepdims=True) + jnp.sum(e_n, axis=1, keepdims=True) + jnp.exp(sink - mx)
        ov = (_dot_nt(e_c.astype(bf16), vc.astype(bf16)) + _dot(e_n.astype(bf16), vx[rs].astype(bf16))) * (1.0 / den)
        oc_all.append(_unstack_swa(ov, seq_len))
        swak_out[g] = shifted_cache(kc, kr[rs])
        swav_out[g] = shifted_cache(vc, vx[rs])

        sm = _dot(_stack_heads(qm[rs]).astype(bf16), mkc_ref[g].astype(bf16))
        mm = jnp.max(sm, axis=1, keepdims=True)
        em = jnp.exp(sm - mm)
        dm = jnp.sum(em, axis=1, keepdims=True)
        od_all.append(_unstack_heads(_dot_nt(em.astype(bf16), mvc_ref[g].astype(bf16)) * (1.0 / dm), seq_len))
    oc = jnp.concatenate(oc_all, axis=0)
    od = jnp.concatenate(od_all, axis=0)
    mix_scr[rows, 2 * GROUP_W:3 * GROUP_W] = (oc * _silu(p_scr[rows, C_SZ:C_SZ + GROUP_W])).astype(bf16)
    mix_scr[rows, 3 * GROUP_W:4 * GROUP_W] = (od * _silu(p_scr[rows, C_MZ:C_MZ + GROUP_W])).astype(bf16)

    @pl.when(g_step == n_steps - 1)
    def _residual():
        for rws in chunks:
            mo = _dot(mix_scr[rws, :], wout_ref[...])
            hs[rws, :] = hs[rws, :] + _rmsnorm(mo, gpost_ref[...])

    @pl.when((l == n_layers - 1) & (g_step == n_steps - 1))
    def _store():
        cp = pltpu.make_async_copy(hs, y_hbm, sem.at[0])
        cp.start()
        cp.wait()


def _sample_layers(x, cos, sin, convb, sgla, kc, vc, mkc, mvc, gpre, gpost, wt, wout, convw, wg, bg, gnorm,
                   sinks, jmat, place, seq_len, group):
    depth, nseq = sgla.shape[0], sgla.shape[1]
    n_tok = nseq * seq_len
    assert nseq % group == 0 and seq_len == SUBLANES and n_tok % TL == 0
    n = group * seq_len

    def per_layer(shape, single_buffer=False):
        nd = len(shape)
        kw = dict(pipeline_mode=pl.Buffered(1)) if single_buffer else {}
        return pl.BlockSpec((None,) + tuple(shape), lambda l, g, _nd=nd: (l,) + (0,) * _nd, **kw)

    def per_group(shape):
        nd = len(shape) - 1
        return pl.BlockSpec((None,) + tuple(shape), lambda l, g, _nd=nd: (l, g) + (0,) * _nd)

    def const(shape):
        nd = len(shape)
        return pl.BlockSpec(shape, lambda l, g, _nd=nd: (0,) * _nd)

    in_specs = [
        pl.BlockSpec(memory_space=pltpu.SMEM),
        pl.BlockSpec(memory_space=pl.ANY),
        const((n, LANES)), const((n, LANES)),
        per_group((n, GROUP_W)),
        per_group((group, N_HEADS, HEAD_D, HEAD_D)),
        per_group((group, SWA_KV_W, WINDOW)), per_group((group, SWA_KV_W, WINDOW)),
        per_group((group, GROUP_W, N_MEM)), per_group((group, GROUP_W, N_MEM)),
        per_layer((1, D_MODEL)), per_layer((1, D_MODEL)),
        per_layer((IN_WIDTH, D_MODEL), True),
        per_layer((D_MODEL, D_MODEL), True),
        per_layer((CONV_W, GROUP_W)), per_layer((GATE_RANK, GROUP_W)), per_layer((1, GROUP_W)), per_layer((1, GROUP_W)),
        const((GROUP_W, GROUP_W)), const((3 * seq_len, WINDOW)),
    ]
    out_shape = (
        jax.ShapeDtypeStruct((n_tok, D_MODEL), f32),
        jax.ShapeDtypeStruct((depth, nseq, CONV_W - 1, GROUP_W), f32),
        jax.ShapeDtypeStruct((depth, nseq, N_HEADS, HEAD_D, HEAD_D), f32),
        jax.ShapeDtypeStruct((depth, nseq, SWA_KV_W, WINDOW), f32),
        jax.ShapeDtypeStruct((depth, nseq, SWA_KV_W, WINDOW), f32),
    )
    out_specs = (
        pl.BlockSpec(memory_space=pl.ANY),
        per_group((group, CONV_W - 1, GROUP_W)),
        per_group((group, N_HEADS, HEAD_D, HEAD_D)),
        per_group((group, SWA_KV_W, WINDOW)), per_group((group, SWA_KV_W, WINDOW)),
    )
    scratch = [
        pltpu.VMEM((n_tok, D_MODEL), f32),
        pltpu.VMEM((n_tok, NP), f32),
        pltpu.VMEM((n_tok, D_MODEL), bf16),
        pltpu.SemaphoreType.DMA((1,)),
    ]
    return pl.pallas_call(
        functools.partial(_sample_kernel, seq_len, group),
        grid=(depth, nseq // group),
        in_specs=in_specs,
        out_specs=out_specs,
        out_shape=out_shape,
        scratch_shapes=scratch,
        compiler_params=pltpu.CompilerParams(dimension_semantics=("arbitrary", "arbitrary"),
                                             vmem_limit_bytes=VMEM_LIMIT_BYTES),
        name="sample_layers",
    )(sinks, x, cos, sin, convb, sgla, kc, vc, mkc, mvc, gpre, gpost, wt, wout, convw, wg, bg, gnorm,
      jmat, place)


def _rope_tables(pos):
    half = HEAD_D // 2
    inv = jnp.power(ROPE_THETA, -jnp.arange(half, dtype=f32) / half)
    ang = pos.astype(f32)[:, None] * inv[None, :]
    cos = jnp.tile(jnp.cos(ang), (1, LANES // half))
    sin = jnp.sin(ang)
    sin_signed = jnp.tile(jnp.concatenate([-sin, sin], axis=1), (1, LANES // HEAD_D))
    return cos, sin_signed


def _feature_major(cache):
    d, s, p, h, e = cache.shape
    return jnp.transpose(cache, (0, 1, 3, 4, 2)).reshape(d, s, h * e, p)


def _position_major(cache_t, heads):
    d, s, he, p = cache_t.shape
    return jnp.transpose(cache_t.reshape(d, s, heads, he // heads, p), (0, 1, 4, 2, 3))


def kernel(x_prompt, x_sample, state_conv, state_gla, cache_swa_k, cache_swa_v, cache_mem_k, cache_mem_v,
           mem_prompt, norm_pre, norm_post, w_in, conv_w, gla_w_gate, gla_b_gate, gla_norm, swa_sinks,
           w_mem_kv, w_out):
    depth = w_in.shape[0]
    B, L, _ = x_prompt.shape
    nseq, seq_len, _ = x_sample.shape
    group = 8

    wt = jnp.swapaxes(w_in, 1, 2).astype(bf16)
    wout = w_out.astype(bf16)
    wmem = w_mem_kv.astype(bf16)
    wg = gla_w_gate.astype(bf16)
    head_id = np.arange(GROUP_W) // HEAD_D
    jmat = jnp.asarray(head_id[:, None] == head_id[None, :], dtype=bf16)
    lvl, swab = _subtile_constants()
    place = np.zeros((3 * seq_len, WINDOW), np.float32)
    for piece in range(3):
        place[piece * seq_len + np.arange(seq_len), WINDOW - seq_len + np.arange(seq_len)] = 1.0
    place = jnp.asarray(place, dtype=bf16)
    sinks = swa_sinks.astype(f32)
    gpre = norm_pre[:, None, :]
    gpost = norm_post[:, None, :]
    bg = gla_b_gate[:, None, :]
    gn = gla_norm[:, None, :]

    cos_p, sin_p = _rope_tables(jnp.arange(L, dtype=jnp.int32))
    cos_p = _to_kernel_order(cos_p, 0)
    sin_p = _to_kernel_order(sin_p, 0)
    cos_s, sin_s = _rope_tables(PAST_LEN + jnp.arange(seq_len, dtype=jnp.int32))
    cos_s = jnp.tile(cos_s, (group, 1))
    sin_s = jnp.tile(sin_s, (group, 1))

    convb = jnp.pad(state_conv, ((0, 0), (0, 0), (0, seq_len - (CONV_W - 1)), (0, 0)))
    convb = convb.reshape(depth, nseq * seq_len, GROUP_W)
    ys, conv_s, gla_s, swak_s, swav_s = _sample_layers(
        x_sample.reshape(nseq * seq_len, D_MODEL), cos_s, sin_s, convb, state_gla,
        _feature_major(cache_swa_k), _feature_major(cache_swa_v),
        _feature_major(cache_mem_k), _feature_major(cache_mem_v),
        gpre, gpost, wt, wout, conv_w, wg, bg, gn, sinks, jmat, place, seq_len, group)

    hp = _to_kernel_order(x_prompt, 1)
    outs_p = [[] for _ in range(6)]
    for l in range(depth):
        res = _prompt_layer(l, hp, mem_prompt, cos_p, sin_p, gpre, gpost, wt, wout, wmem, conv_w, wg, bg, gn,
                            sinks, jmat, lvl, swab)
        hp = res[0]
        for i in range(6):
            outs_p[i].append(res[i + 1])

    return (_from_kernel_order(hp, 1), ys.reshape(nseq, seq_len, D_MODEL),
            jnp.stack(outs_p[0]), jnp.stack(outs_p[1]),
            _position_major(jnp.stack(outs_p[2]), 2), _position_major(jnp.stack(outs_p[3]), 2),
            _position_major(jnp.stack(outs_p[4]), N_HEADS), _position_major(jnp.stack(outs_p[5]), N_HEADS),
            conv_s, gla_s, _position_major(swak_s, 2), _position_major(swav_s, 2))
```

```python
import functools

import jax
import jax.numpy as jnp
import numpy as np
from jax import lax
from jax.experimental import pallas as pl
from jax.experimental.pallas import tpu as pltpu

f32 = jnp.float32
bf16 = jnp.bfloat16

D_MODEL = 1024
GROUP_W = 256
HEAD_D = 64
N_HEADS = 4
SWA_KV_W = 128
N_MEM = 256
WINDOW = 128
CONV_W = 3
GATE_RANK = 16
GATE_NORM = 16.0
GLA_BLOCK = 16
ROPE_THETA = 10000.0
PAST_LEN = 8192
EPS = 1e-6
NEG = -1e30
QK_SCALE = HEAD_D ** -0.5

LANES = 128
SUBLANES = 8
VMEM_LIMIT_BYTES = 56 * 1024 * 1024

C_CX, C_CB, C_CC, C_CZ = 0, 256, 512, 768
C_GQ, C_GK, C_GV, C_GZ = 1024, 1280, 1536, 1792
C_SQ, C_SK, C_SV, C_SZ = 2048, 2304, 2432, 2560
C_MQ, C_MZ = 2816, 3072
C_LR = 3328
NP = 3456
IN_WIDTH = 3344
_O_GLR, _O_GZ = 1792, 1808

SUB = 128
TL = 512


def _dot(a, b):
    return jnp.dot(a, b, preferred_element_type=f32)


def _dot_nt(a, b):
    return lax.dot_general(a, b, (((1,), (1,)), ((), ())), preferred_element_type=f32)


def _dot_tn(a, b):
    return lax.dot_general(a, b, (((0,), (0,)), ((), ())), preferred_element_type=f32)


def _rmsnorm(x, g):
    return x * lax.rsqrt(jnp.mean(x * x, axis=-1, keepdims=True) + EPS) * g


def _silu(z):
    return z * (0.5 + 0.5 * jnp.tanh(0.5 * z))


def _log_sigmoid(x):
    return jnp.minimum(x, 0.0) - jnp.log1p(jnp.exp(-jnp.abs(x)))


def _project(h, wt_ref, p_ref, rows):
    p_ref[rows, 0:_O_GLR] = _dot_nt(h, wt_ref[0:_O_GLR, :])
    p_ref[rows, _O_GLR:C_LR] = _dot_nt(h, wt_ref[_O_GZ:IN_WIDTH, :])
    p_ref[rows, C_LR:C_LR + GATE_RANK] = _dot_nt(h, wt_ref[_O_GLR:_O_GZ, :])


def _head_sum(x, j_ref):
    hi = x.astype(bf16)
    lo = (x - hi.astype(f32)).astype(bf16)
    return _dot(hi, j_ref[...]) + _dot(lo, j_ref[...])


def _rope(x, cos, sin_signed):
    lane = lax.broadcasted_iota(jnp.int32, x.shape, 1)
    swapped = jnp.where((lane % HEAD_D) < HEAD_D // 2,
                        pltpu.roll(x, LANES - HEAD_D // 2, 1), pltpu.roll(x, HEAD_D // 2, 1))
    return x * cos + swapped * sin_signed


def _stack_heads(x):
    lane_head = lax.broadcasted_iota(jnp.int32, x.shape, 1) // HEAD_D
    return jnp.concatenate([jnp.where(lane_head == h, x, 0.0) for h in range(N_HEADS)], axis=0)


def _unstack_heads(o, n):
    lane_head = lax.broadcasted_iota(jnp.int32, (n, GROUP_W), 1) // HEAD_D
    out = o[0:n]
    for h in range(1, N_HEADS):
        out = jnp.where(lane_head == h, o[h * n:(h + 1) * n], out)
    return out


def _stack_swa_q(q01, q23):
    low = lax.broadcasted_iota(jnp.int32, q01.shape, 1) < HEAD_D
    return jnp.concatenate([jnp.where(low, q01, 0.0), jnp.where(low, pltpu.roll(q01, HEAD_D, 1), 0.0),
                            jnp.where(low, 0.0, pltpu.roll(q23, HEAD_D, 1)), jnp.where(low, 0.0, q23)], axis=0)


def _unstack_swa(o, n):
    low = lax.broadcasted_iota(jnp.int32, (n, SWA_KV_W), 1) < HEAD_D
    c01 = jnp.where(low, o[0:n], pltpu.roll(o[n:2 * n], HEAD_D, 1))
    c23 = jnp.where(low, pltpu.roll(o[2 * n:3 * n], HEAD_D, 1), o[3 * n:4 * n])
    return jnp.concatenate([c01, c23], axis=1)


def _sink_column(sinks, n):
    return jnp.concatenate([jnp.full((n, 1), s, f32) for s in sinks], axis=0)


def _block_cumsum(la, block):
    row = lax.broadcasted_iota(jnp.int32, la.shape, 0) % block
    b = la
    s = 1
    while s < block:
        b = b + jnp.where(row >= s, pltpu.roll(b, s, 0), 0.0)
        s *= 2
    return b


def _gate_log_decay(p_lr, wg_ref, bg_ref):
    pre = _dot(p_lr.astype(bf16), wg_ref[...]) + bg_ref[...]
    return _log_sigmoid(pre) * (1.0 / GATE_NORM)


def _gla_norm_gate(o, gz, gnorm_ref, j_ref):
    ms = _head_sum(o * o, j_ref) * (1.0 / HEAD_D)
    return o * lax.rsqrt(ms + EPS) * gnorm_ref[...] * _silu(gz)


GROUPS = SUB // SUBLANES
LEVEL_GROUPS = (1, 2, 4)


def _row_groups(x):
    return [x[SUBLANES * r:SUBLANES * (r + 1), :] for r in range(x.shape[0] // SUBLANES)]


def _prompt_layer_kernel(sinks_ref, x_ref, memp_ref, cos_ref, sin_ref, gpre_ref, gpost_ref, wt_ref,
                         wout_ref, wmem_ref, convw_ref, wg_ref, bg_ref, gnorm_ref, j_ref, lvl_ref, swab_ref,
                         y_ref, conv_out, gla_out, swak_out, swav_out, mk_out, mv_out,
                         p_scr, mix_scr, cbuf, st_scr, kprev, vprev, vtprev, mkb, mvtb, kv_nat):
    t = pl.program_id(1)
    nt = pl.num_programs(1)
    nsub = TL // SUB

    @pl.when(t == 0)
    def _init():
        cbuf[...] = jnp.zeros((SUBLANES, GROUP_W), f32)
        st_scr[...] = jnp.zeros((GROUP_W, GROUP_W), f32)
        kprev[...] = jnp.zeros((SUB, SWA_KV_W), f32)
        vprev[...] = jnp.zeros((SUB, SWA_KV_W), f32)
        vtprev[...] = jnp.zeros((SWA_KV_W, SUB), f32)
        mkv = _dot(memp_ref[0].astype(bf16), wmem_ref[...])
        mvt = mkv[:, GROUP_W:2 * GROUP_W].T
        mk_out[0] = mkv[:, 0:GROUP_W].T
        mv_out[0] = mvt
        mkb[...] = mkv[:, 0:GROUP_W].astype(bf16)
        mvtb[...] = mvt.astype(bf16)

    x = x_ref[0]
    h = _rmsnorm(x, gpre_ref[...]).astype(bf16)
    _project(h, wt_ref, p_scr, slice(None))

    sub8 = lax.broadcasted_iota(jnp.int32, (SUBLANES, GROUP_W), 0)

    def sub_tile(j, carry):
        r0 = pl.multiple_of(j * SUB, SUB)
        rows = pl.ds(r0, SUB)

        u = p_scr[rows, C_CC:C_CC + GROUP_W] * p_scr[rows, C_CX:C_CX + GROUP_W]
        last = SUB - SUBLANES
        prev1 = jnp.where(sub8 == 0, cbuf[0:1, :], pltpu.roll(u[last:SUB, :], 1, 0))
        prev2 = jnp.where(sub8 == 0, cbuf[1:2, :], pltpu.roll(u[last - SUBLANES:last, :], 1, 0))
        um1 = jnp.concatenate([prev1, u[0:last, :]], axis=0)
        um2 = jnp.concatenate([prev2, prev1, u[0:last - SUBLANES, :]], axis=0)
        cy = convw_ref[0:1, :] * um2 + convw_ref[1:2, :] * um1 + convw_ref[2:3, :] * u
        cbuf[0:1, :] = u[SUB - 1:SUB, :]
        cbuf[1:2, :] = u[last - 1:last, :]
        a_out = p_scr[rows, C_CB:C_CB + GROUP_W] * cy * _silu(p_scr[rows, C_CZ:C_CZ + GROUP_W])
        mix_scr[rows, 0:GROUP_W] = a_out.astype(bf16)

        qg = _row_groups(p_scr[rows, C_GQ:C_GQ + GROUP_W] * QK_SCALE)
        k = p_scr[rows, C_GK:C_GK + GROUP_W]
        v = p_scr[rows, C_GV:C_GV + GROUP_W]
        kg = _row_groups(k)
        vg = _row_groups(v)
        lag = _row_groups(_gate_log_decay(p_scr[rows, C_LR:C_LR + GATE_RANK], wg_ref, bg_ref))
        bg_ = [lag[0]]
        for r in range(1, GROUPS):
            bg_.append(bg_[-1] + lag[r])
        tot = bg_[GROUPS - 1]
        pw = []
        for r in range(GROUPS):
            for s in range(r):
                pw.append(qg[r] * kg[s] * jnp.exp(bg_[r] - bg_[s]))
            pw.append(qg[r] * kg[r])
        scores = _dot(jnp.concatenate(pw, axis=0).astype(bf16), j_ref[...])
        og = []
        idx = 0
        for r in range(GROUPS):
            acc = None
            for s in range(r + 1):
                term = scores[SUBLANES * idx:SUBLANES * (idx + 1), :] * vg[s]
                acc = term if acc is None else acc + term
                idx += 1
            og.append(acc)
        o = jnp.concatenate(og, axis=0)

        def decayed(group):
            before = jnp.zeros((SUBLANES, GROUP_W), f32)
            after = jnp.zeros((SUBLANES, GROUP_W), f32)
            for s in range(1, group):
                before = before + jnp.where(sub8 % group >= s, pltpu.roll(tot, s, 0), 0.0)
                after = after + jnp.where(sub8 % group < group - s, pltpu.roll(tot, SUBLANES - s, 0), 0.0)
            qd = jnp.concatenate([qg[r] * jnp.exp(bg_[r] + before) for r in range(GROUPS)], axis=0)
            kd = jnp.concatenate([kg[r] * jnp.exp((tot - bg_[r]) + after) for r in range(GROUPS)], axis=0)
            return qd, kd

        attn = None
        for li, group in enumerate(LEVEL_GROUPS):
            qd, kd = decayed(group)
            s = _dot_nt(_stack_heads(qd).astype(bf16), kd.astype(bf16)) * lvl_ref[li]
            attn = s if attn is None else attn + s
        o = o + _unstack_heads(_dot(attn.astype(bf16), v.astype(bf16)), SUB)
        qd, kd = decayed(SUB // GLA_BLOCK)
        st = st_scr[...]
        o = o + _dot_nt(qd.astype(bf16), st.astype(bf16))
        total = jnp.sum(tot, axis=0, keepdims=True)
        upd = _dot_tn(v.astype(bf16), kd.astype(bf16))
        same_head = (lax.broadcasted_iota(jnp.int32, (GROUP_W, GROUP_W), 0) // HEAD_D
                     == lax.broadcasted_iota(jnp.int32, (GROUP_W, GROUP_W), 1) // HEAD_D)
        st_scr[...] = st * jnp.exp(total) + jnp.where(same_head, upd, 0.0)
        b_out = _gla_norm_gate(o, p_scr[rows, C_GZ:C_GZ + GROUP_W], gnorm_ref, j_ref)
        mix_scr[rows, GROUP_W:2 * GROUP_W] = b_out.astype(bf16)

        cos = cos_ref[rows, :]
        sin = sin_ref[rows, :]
        q01 = _rope(p_scr[rows, C_SQ:C_SQ + LANES], cos, sin) * QK_SCALE
        q23 = _rope(p_scr[rows, C_SQ + LANES:C_SQ + 2 * LANES], cos, sin) * QK_SCALE
        kr = _rope(p_scr[rows, C_SK:C_SK + SWA_KV_W], cos, sin)
        vx = p_scr[rows, C_SV:C_SV + SWA_KV_W]
        k_all = jnp.concatenate([kprev[...], kr], axis=0).astype(bf16)
        vt_new = vx.T
        vt_all = jnp.concatenate([vtprev[...], vt_new], axis=1).astype(bf16)
        first = jnp.where((t * nsub + j) == 0, 1, 0)
        sc = _dot_nt(k_all, _stack_swa_q(q01, q23).astype(bf16)) + swab_ref[first]
        lane_head = lax.broadcasted_iota(jnp.int32, (1, N_HEADS * SUB), 1) // SUB
        sink = jnp.full((1, N_HEADS * SUB), sinks_ref[0], f32)
        for hh in range(1, N_HEADS):
            sink = jnp.where(lane_head == hh, sinks_ref[hh], sink)
        mx = jnp.maximum(jnp.max(sc, axis=0, keepdims=True), sink)
        e = jnp.exp(sc - mx)
        den = jnp.sum(e, axis=0, keepdims=True) + jnp.exp(sink - mx)
        ot = _dot(vt_all, e.astype(bf16)) * (1.0 / den)
        oc = _unstack_swa(jnp.concatenate([ot[:, SUB * hh:SUB * (hh + 1)].T for hh in range(N_HEADS)], axis=0), SUB)
        mix_scr[rows, 2 * GROUP_W:3 * GROUP_W] = (oc * _silu(p_scr[rows, C_SZ:C_SZ + GROUP_W])).astype(bf16)
        kprev[...] = kr
        vprev[...] = vx
        vtprev[...] = vt_new

        qm = _stack_heads(p_scr[rows, C_MQ:C_MQ + GROUP_W] * QK_SCALE).astype(bf16)
        sm = _dot_nt(mkb[...], qm)
        mm = jnp.max(sm, axis=0, keepdims=True)
        em = jnp.exp(sm - mm)
        dm = jnp.sum(em, axis=0, keepdims=True)
        odt = _dot(mvtb[...], em.astype(bf16)) * (1.0 / dm)
        low = lax.broadcasted_iota(jnp.int32, (SUB, LANES), 1) < HEAD_D
        halves = []
        for pair in range(N_HEADS // 2):
            blk_rows = odt[LANES * pair:LANES * (pair + 1), :]
            even = blk_rows[:, SUB * (2 * pair):SUB * (2 * pair + 1)].T
            odd = blk_rows[:, SUB * (2 * pair + 1):SUB * (2 * pair + 2)].T
            halves.append(jnp.where(low, even, odd))
        od = jnp.concatenate(halves, axis=1)
        mix_scr[rows, 3 * GROUP_W:4 * GROUP_W] = (od * _silu(p_scr[rows, C_MZ:C_MZ + GROUP_W])).astype(bf16)
        return carry

    lax.fori_loop(0, nsub, sub_tile, 0)

    mo = _dot(mix_scr[...], wout_ref[...])
    y_ref[0] = x + _rmsnorm(mo, gpost_ref[...])

    @pl.when(t == nt - 1)
    def _final():
        conv_out[0, 0:1, :] = cbuf[1:2, :]
        conv_out[0, 1:2, :] = cbuf[0:1, :]
        for src, dst in ((kprev, swak_out), (vprev, swav_out)):
            for r in range(GROUPS):
                kv_nat[pl.ds(r, SUBLANES, stride=GLA_BLOCK), :] = src[SUBLANES * r:SUBLANES * (r + 1), :]
            dst[0] = kv_nat[...].T
        st = st_scr[...]
        for hh in range(N_HEADS):
            gla_out[0, hh] = st[HEAD_D * hh:HEAD_D * (hh + 1), HEAD_D * hh:HEAD_D * (hh + 1)].T


def _layer_spec(shape, l):
    nd = len(shape)
    return pl.BlockSpec((None,) + tuple(shape), lambda b, t, _l=l, _nd=nd: (_l,) + (0,) * _nd)


def _const_spec(shape):
    nd = len(shape)
    return pl.BlockSpec(shape, lambda b, t, _nd=nd: (0,) * _nd)


def _prompt_layer(l, x, memp, cos, sin, gpre, gpost, wt, wout, wmem, convw, wg, bg, gnorm, sinks, jmat,
                  lvl, swab):
    B, L, _ = x.shape
    assert L % TL == 0
    nt = L // TL
    out_shape = (
        jax.ShapeDtypeStruct((B, L, D_MODEL), f32),
        jax.ShapeDtypeStruct((B, CONV_W - 1, GROUP_W), f32),
        jax.ShapeDtypeStruct((B, N_HEADS, HEAD_D, HEAD_D), f32),
        jax.ShapeDtypeStruct((B, SWA_KV_W, WINDOW), f32),
        jax.ShapeDtypeStruct((B, SWA_KV_W, WINDOW), f32),
        jax.ShapeDtypeStruct((B, GROUP_W, N_MEM), f32),
        jax.ShapeDtypeStruct((B, GROUP_W, N_MEM), f32),
    )
    in_specs = [
        pl.BlockSpec(memory_space=pltpu.SMEM),
        pl.BlockSpec((1, TL, D_MODEL), lambda b, t: (b, t, 0)),
        pl.BlockSpec((1, N_MEM, D_MODEL), lambda b, t: (b, 0, 0)),
        pl.BlockSpec((TL, LANES), lambda b, t: (t, 0)),
        pl.BlockSpec((TL, LANES), lambda b, t: (t, 0)),
        _layer_spec((1, D_MODEL), l), _layer_spec((1, D_MODEL), l),
        _layer_spec((IN_WIDTH, D_MODEL), l),
        _layer_spec((D_MODEL, D_MODEL), l), _layer_spec((D_MODEL, 2 * GROUP_W), l),
        _layer_spec((CONV_W, GROUP_W), l), _layer_spec((GATE_RANK, GROUP_W), l), _layer_spec((1, GROUP_W), l),
        _layer_spec((1, GROUP_W), l), _const_spec((GROUP_W, GROUP_W)),
        _const_spec((len(LEVEL_GROUPS), N_HEADS * SUB, SUB)), _const_spec((2, 2 * SUB, N_HEADS * SUB)),
    ]
    out_specs = (
        pl.BlockSpec((1, TL, D_MODEL), lambda b, t: (b, t, 0)),
        pl.BlockSpec((1, CONV_W - 1, GROUP_W), lambda b, t: (b, 0, 0)),
        pl.BlockSpec((1, N_HEADS, HEAD_D, HEAD_D), lambda b, t: (b, 0, 0, 0)),
        pl.BlockSpec((1, SWA_KV_W, WINDOW), lambda b, t: (b, 0, 0)),
        pl.BlockSpec((1, SWA_KV_W, WINDOW), lambda b, t: (b, 0, 0)),
        pl.BlockSpec((1, GROUP_W, N_MEM), lambda b, t: (b, 0, 0)),
        pl.BlockSpec((1, GROUP_W, N_MEM), lambda b, t: (b, 0, 0)),
    )
    scratch = [
        pltpu.VMEM((TL, NP), f32),
        pltpu.VMEM((TL, D_MODEL), bf16),
        pltpu.VMEM((SUBLANES, GROUP_W), f32),
        pltpu.VMEM((GROUP_W, GROUP_W), f32),
        pltpu.VMEM((SUB, SWA_KV_W), f32),
        pltpu.VMEM((SUB, SWA_KV_W), f32),
        pltpu.VMEM((SWA_KV_W, SUB), f32),
        pltpu.VMEM((N_MEM, GROUP_W), bf16),
        pltpu.VMEM((GROUP_W, N_MEM), bf16),
        pltpu.VMEM((WINDOW, SWA_KV_W), f32),
    ]
    return pl.pallas_call(
        _prompt_layer_kernel,
        grid=(B, nt),
        in_specs=in_specs,
        out_specs=out_specs,
        out_shape=out_shape,
        scratch_shapes=scratch,
        compiler_params=pltpu.CompilerParams(dimension_semantics=("arbitrary", "arbitrary"),
                                             vmem_limit_bytes=VMEM_LIMIT_BYTES),
        name="prompt_layer",
    )(sinks[l], x, memp, cos, sin, gpre, gpost, wt, wout, wmem, convw, wg, bg, gnorm, jmat, lvl, swab)


def _subtile_constants():
    row = np.arange(SUB)
    tok = (row % SUBLANES) * GLA_BLOCK + row // SUBLANES
    blk = row % SUBLANES
    levels = []
    for group in LEVEL_GROUPS:
        g = blk // group
        sel = (g[:, None] == g[None, :] + 1) & (g[:, None] % 2 == 1)
        levels.append(np.tile(sel, (N_HEADS, 1)))
    key_tok = np.concatenate([tok - SUB, tok])
    valid = (key_tok[None, :] <= tok[:, None]) & (key_tok[None, :] > tok[:, None] - WINDOW)
    bias = [np.where(valid, 0.0, NEG), np.where(valid & (key_tok[None, :] >= 0), 0.0, NEG)]
    bias = np.stack([np.tile(b, (N_HEADS, 1)).T for b in bias])
    return jnp.asarray(np.stack(levels), dtype=f32), jnp.asarray(bias, dtype=f32)


def _to_kernel_order(x, axis):
    shp = x.shape
    n = shp[axis]
    x = x.reshape(shp[:axis] + (n // SUB, SUBLANES, GLA_BLOCK) + shp[axis + 1:])
    return jnp.swapaxes(x, axis + 1, axis + 2).reshape(shp)


def _from_kernel_order(x, axis):
    shp = x.shape
    n = shp[axis]
    x = x.reshape(shp[:axis] + (n // SUB, GLA_BLOCK, SUBLANES) + shp[axis + 1:])
    return jnp.swapaxes(x, axis + 1, axis + 2).reshape(shp)


def _sample_kernel(seq_len, group, sinks_ref, x_hbm, cos_ref, sin_ref, convb_ref, sgla_ref, kc_ref, vc_ref,
                   mkc_ref, mvc_ref, gpre_ref, gpost_ref, wt_ref, wout_ref, convw_ref, wg_ref, bg_ref,
                   gnorm_ref, j_ref, place_ref,
                   y_hbm, conv_out, gla_out, swak_out, swav_out,
                   hs, p_scr, mix_scr, sem):
    l = pl.program_id(0)
    g_step = pl.program_id(1)
    n_layers = pl.num_programs(0)
    n_steps = pl.num_programs(1)
    n_tok = hs.shape[0]
    n = group * seq_len
    chunks = [slice(c * TL, (c + 1) * TL) for c in range(n_tok // TL)]

    @pl.when((l == 0) & (g_step == 0))
    def _load():
        cp = pltpu.make_async_copy(x_hbm, hs, sem.at[0])
        cp.start()
        cp.wait()

    @pl.when(g_step == 0)
    def _project_all():
        for rows in chunks:
            h = _rmsnorm(hs[rows, :], gpre_ref[...]).astype(bf16)
            _project(h, wt_ref, p_scr, rows)

    rows = pl.ds(pl.multiple_of(g_step * n, n), n)
    row_t = lax.broadcasted_iota(jnp.int32, (n, GROUP_W), 0) % seq_len

    u = p_scr[rows, C_CC:C_CC + GROUP_W] * p_scr[rows, C_CX:C_CX + GROUP_W]
    hist = convb_ref[...]
    um1 = jnp.where(row_t >= 1, pltpu.roll(u, 1, 0), pltpu.roll(hist, n - 1, 0))
    um2 = jnp.where(row_t >= 2, pltpu.roll(u, 2, 0), hist)
    cy = convw_ref[0:1, :] * um2 + convw_ref[1:2, :] * um1 + convw_ref[2:3, :] * u
    a_out = p_scr[rows, C_CB:C_CB + GROUP_W] * cy * _silu(p_scr[rows, C_CZ:C_CZ + GROUP_W])
    mix_scr[rows, 0:GROUP_W] = a_out.astype(bf16)
    for g in range(group):
        conv_out[g] = u[g * seq_len + seq_len - (CONV_W - 1):(g + 1) * seq_len, :]

    qs = p_scr[rows, C_GQ:C_GQ + GROUP_W] * QK_SCALE
    k = p_scr[rows, C_GK:C_GK + GROUP_W]
    v = p_scr[rows, C_GV:C_GV + GROUP_W]
    la = _gate_log_decay(p_scr[rows, C_LR:C_LR + GATE_RANK], wg_ref, bg_ref)
    bc = _block_cumsum(la, seq_len)
    o = _dot((qs * k).astype(bf16), j_ref[...]) * v
    for d in range(1, seq_len):
        dec = jnp.exp(jnp.where(row_t >= d, bc - pltpu.roll(bc, d, 0), NEG))
        pw = qs * pltpu.roll(k, d, 0) * dec
        o = o + _dot(pw.astype(bf16), j_ref[...]) * pltpu.roll(v, d, 0)
    qd = qs * jnp.exp(bc)
    tot = [bc[(g + 1) * seq_len - 1:(g + 1) * seq_len, :] for g in range(group)]
    kd = k * jnp.exp(jnp.concatenate(
        [tot[g] - bc[g * seq_len:(g + 1) * seq_len, :] for g in range(group)], axis=0))
    eye = (lax.broadcasted_iota(jnp.int32, (HEAD_D, HEAD_D), 0)
           == lax.broadcasted_iota(jnp.int32, (HEAD_D, HEAD_D), 1))
    o_inter = []
    for g in range(group):
        rs = slice(g * seq_len, (g + 1) * seq_len)
        alpha = jnp.exp(tot[g])
        per_head = []
        for hh in range(N_HEADS):
            ls = slice(hh * HEAD_D, (hh + 1) * HEAD_D)
            s0 = sgla_ref[g, hh]
            per_head.append(_dot(qd[rs, ls].astype(bf16), s0.astype(bf16)))
            alpha_col = jnp.sum(jnp.where(eye, alpha[:, ls], 0.0), axis=1, keepdims=True)
            gla_out[g, hh] = alpha_col * s0 + _dot_tn(kd[rs, ls].astype(bf16), v[rs, ls].astype(bf16))
        o_inter.append(jnp.concatenate(per_head, axis=1))
    o = o + jnp.concatenate(o_inter, axis=0)
    b_out = _gla_norm_gate(o, p_scr[rows, C_GZ:C_GZ + GROUP_W], gnorm_ref, j_ref)
    mix_scr[rows, GROUP_W:2 * GROUP_W] = b_out.astype(bf16)

    cos = cos_ref[...]
    sin = sin_ref[...]
    q01 = _rope(p_scr[rows, C_SQ:C_SQ + LANES], cos, sin) * QK_SCALE
    q23 = _rope(p_scr[rows, C_SQ + LANES:C_SQ + 2 * LANES], cos, sin) * QK_SCALE
    kr = _rope(p_scr[rows, C_SK:C_SK + SWA_KV_W], cos, sin)
    vx = p_scr[rows, C_SV:C_SV + SWA_KV_W]
    qm = p_scr[rows, C_MQ:C_MQ + GROUP_W] * QK_SCALE
    sink = _sink_column([sinks_ref[l, hh] for hh in range(N_HEADS)], seq_len)
    hq = N_HEADS * seq_len
    qrow = lax.broadcasted_iota(jnp.int32, (hq, WINDOW), 0) % seq_len
    ccol = lax.broadcasted_iota(jnp.int32, (hq, WINDOW), 1)
    cache_valid = ccol > qrow
    nrow = lax.broadcasted_iota(jnp.int32, (hq, seq_len), 0) % seq_len
    ncol = lax.broadcasted_iota(jnp.int32, (hq, seq_len), 1)
    new_valid = ncol <= nrow
    keep_old = lax.broadcasted_iota(jnp.int32, (SWA_KV_W, WINDOW), 1) < WINDOW - seq_len

    def exact_split(a):
        hi = a.astype(bf16)
        r1 = a - hi.astype(f32)
        mid = r1.astype(bf16)
        lo = (r1 - mid.astype(f32)).astype(bf16)
        return jnp.concatenate([hi, mid, lo], axis=0)

    def shifted_cache(old_t, new_rows):
        placed = _dot_tn(exact_split(new_rows), place_ref[...])
        return jnp.where(keep_old, pltpu.roll(old_t, WINDOW - seq_len, 1), placed)

    oc_all, od_all = [], []
    for g in range(group):
        rs = slice(g * seq_len, (g + 1) * seq_len)
        kc = kc_ref[g]
        vc = vc_ref[g]
        qst = _stack_swa_q(q01[rs], q23[rs]).astype(bf16)
        s_c = jnp.where(cache_valid, _dot(qst, kc.astype(bf16)), NEG)
        s_n = jnp.where(new_valid, _dot_nt(qst, kr[rs].astype(bf16)), NEG)
        mx = jnp.maximum(jnp.maximum(jnp.max(s_c, axis=1, keepdims=True), jnp.max(s_n, axis=1, keepdims=True)), sink)
        e_c = jnp.exp(s_c - mx)
        e_n = jnp.exp(s_n - mx)
        den = jnp.sum(e_c, axis=1, keepdims=True) + jnp.sum(e_n, axis=1, keepdims=True) + jnp.exp(sink - mx)
        ov = (_dot_nt(e_c.astype(bf16), vc.astype(bf16)) + _dot(e_n.astype(bf16), vx[rs].astype(bf16))) * (1.0 / den)
        oc_all.append(_unstack_swa(ov, seq_len))
        swak_out[g] = shifted_cache(kc, kr[rs])
        swav_out[g] = shifted_cache(vc, vx[rs])

        sm = _dot(_stack_heads(qm[rs]).astype(bf16), mkc_ref[g].astype(bf16))
        mm = jnp.max(sm, axis=1, keepdims=True)
        em = jnp.exp(sm - mm)
        dm = jnp.sum(em, axis=1, keepdims=True)
        od_all.append(_unstack_heads(_dot_nt(em.astype(bf16), mvc_ref[g].astype(bf16)) * (1.0 / dm), seq_len))
    oc = jnp.concatenate(oc_all, axis=0)
    od = jnp.concatenate(od_all, axis=0)
    mix_scr[rows, 2 * GROUP_W:3 * GROUP_W] = (oc * _silu(p_scr[rows, C_SZ:C_SZ + GROUP_W])).astype(bf16)
    mix_scr[rows, 3 * GROUP_W:4 * GROUP_W] = (od * _silu(p_scr[rows, C_MZ:C_MZ + GROUP_W])).astype(bf16)

    @pl.when(g_step == n_steps - 1)
    def _residual():
        for rws in chunks:
            mo = _dot(mix_scr[rws, :], wout_ref[...])
            hs[rws, :] = hs[rws, :] + _rmsnorm(mo, gpost_ref[...])

    @pl.when((l == n_layers - 1) & (g_step == n_steps - 1))
    def _store():
        cp = pltpu.make_async_copy(hs, y_hbm, sem.at[0])
        cp.start()
        cp.wait()


def _sample_layers(x, cos, sin, convb, sgla, kc, vc, mkc, mvc, gpre, gpost, wt, wout, convw, wg, bg, gnorm,
                   sinks, jmat, place, seq_len, group):
    depth, nseq = sgla.shape[0], sgla.shape[1]
    n_tok = nseq * seq_len
    assert nseq % group == 0 and seq_len == SUBLANES and n_tok % TL == 0
    n = group * seq_len

    def per_layer(shape, single_buffer=False):
        nd = len(shape)
        kw = dict(pipeline_mode=pl.Buffered(1)) if single_buffer else {}
        return pl.BlockSpec((None,) + tuple(shape), lambda l, g, _nd=nd: (l,) + (0,) * _nd, **kw)

    def per_group(shape):
        nd = len(shape) - 1
        return pl.BlockSpec((None,) + tuple(shape), lambda l, g, _nd=nd: (l, g) + (0,) * _nd)

    def const(shape):
        nd = len(shape)
        return pl.BlockSpec(shape, lambda l, g, _nd=nd: (0,) * _nd)

    in_specs = [
        pl.BlockSpec(memory_space=pltpu.SMEM),
        pl.BlockSpec(memory_space=pl.ANY),
        const((n, LANES)), const((n, LANES)),
        per_group((n, GROUP_W)),
        per_group((group, N_HEADS, HEAD_D, HEAD_D)),
        per_group((group, SWA_KV_W, WINDOW)), per_group((group, SWA_KV_W, WINDOW)),
        per_group((group, GROUP_W, N_MEM)), per_group((group, GROUP_W, N_MEM)),
        per_layer((1, D_MODEL)), per_layer((1, D_MODEL)),
        per_layer((IN_WIDTH, D_MODEL), True),
        per_layer((D_MODEL, D_MODEL), True),
        per_layer((CONV_W, GROUP_W)), per_layer((GATE_RANK, GROUP_W)), per_layer((1, GROUP_W)), per_layer((1, GROUP_W)),
        const((GROUP_W, GROUP_W)), const((3 * seq_len, WINDOW)),
    ]
    out_shape = (
        jax.ShapeDtypeStruct((n_tok, D_MODEL), f32),
        jax.ShapeDtypeStruct((depth, nseq, CONV_W - 1, GROUP_W), f32),
        jax.ShapeDtypeStruct((depth, nseq, N_HEADS, HEAD_D, HEAD_D), f32),
        jax.ShapeDtypeStruct((depth, nseq, SWA_KV_W, WINDOW), f32),
        jax.ShapeDtypeStruct((depth, nseq, SWA_KV_W, WINDOW), f32),
    )
    out_specs = (
        pl.BlockSpec(memory_space=pl.ANY),
        per_group((group, CONV_W - 1, GROUP_W)),
        per_group((group, N_HEADS, HEAD_D, HEAD_D)),
        per_group((group, SWA_KV_W, WINDOW)), per_group((group, SWA_KV_W, WINDOW)),
    )
    scratch = [
        pltpu.VMEM((n_tok, D_MODEL), f32),
        pltpu.VMEM((n_tok, NP), f32),
        pltpu.VMEM((n_tok, D_MODEL), bf16),
        pltpu.SemaphoreType.DMA((1,)),
    ]
    return pl.pallas_call(
        functools.partial(_sample_kernel, seq_len, group),
        grid=(depth, nseq // group),
        in_specs=in_specs,
        out_specs=out_specs,
        out_shape=out_shape,
        scratch_shapes=scratch,
        compiler_params=pltpu.CompilerParams(dimension_semantics=("arbitrary", "arbitrary"),
                                             vmem_limit_bytes=VMEM_LIMIT_BYTES),
        name="sample_layers",
    )(sinks, x, cos, sin, convb, sgla, kc, vc, mkc, mvc, gpre, gpost, wt, wout, convw, wg, bg, gnorm,
      jmat, place)


def _rope_tables(pos):
    half = HEAD_D // 2
    inv = jnp.power(ROPE_THETA, -jnp.arange(half, dtype=f32) / half)
    ang = pos.astype(f32)[:, None] * inv[None, :]
    cos = jnp.tile(jnp.cos(ang), (1, LANES // half))
    sin = jnp.sin(ang)
    sin_signed = jnp.tile(jnp.concatenate([-sin, sin], axis=1), (1, LANES // HEAD_D))
    return cos, sin_signed


def _feature_major(cache):
    d, s, p, h, e = cache.shape
    return jnp.transpose(cache, (0, 1, 3, 4, 2)).reshape(d, s, h * e, p)


def _position_major(cache_t, heads):
    d, s, he, p = cache_t.shape
    return jnp.transpose(cache_t.reshape(d, s, heads, he // heads, p), (0, 1, 4, 2, 3))


def kernel(x_prompt, x_sample, state_conv, state_gla, cache_swa_k, cache_swa_v, cache_mem_k, cache_mem_v,
           mem_prompt, norm_pre, norm_post, w_in, conv_w, gla_w_gate, gla_b_gate, gla_norm, swa_sinks,
           w_mem_kv, w_out):
    depth = w_in.shape[0]
    B, L, _ = x_prompt.shape
    nseq, seq_len, _ = x_sample.shape
    group = 8

    wt = jnp.swapaxes(w_in, 1, 2).astype(bf16)
    wout = w_out.astype(bf16)
    wmem = w_mem_kv.astype(bf16)
    wg = gla_w_gate.astype(bf16)
    head_id = np.arange(GROUP_W) // HEAD_D
    jmat = jnp.asarray(head_id[:, None] == head_id[None, :], dtype=bf16)
    lvl, swab = _subtile_constants()
    place = np.zeros((3 * seq_len, WINDOW), np.float32)
    for piece in range(3):
        place[piece * seq_len + np.arange(seq_len), WINDOW - seq_len + np.arange(seq_len)] = 1.0
    place = jnp.asarray(place, dtype=bf16)
    sinks = swa_sinks.astype(f32)
    gpre = norm_pre[:, None, :]
    gpost = norm_post[:, None, :]
    bg = gla_b_gate[:, None, :]
    gn = gla_norm[:, None, :]

    cos_p, sin_p = _rope_tables(jnp.arange(L, dtype=jnp.int32))
    cos_p = _to_kernel_order(cos_p, 0)
    sin_p = _to_kernel_order(sin_p, 0)
    cos_s, sin_s = _rope_tables(PAST_LEN + jnp.arange(seq_len, dtype=jnp.int32))
    cos_s = jnp.tile(cos_s, (group, 1))
    sin_s = jnp.tile(sin_s, (group, 1))

    convb = jnp.pad(state_conv, ((0, 0), (0, 0), (0, seq_len - (CONV_W - 1)), (0, 0)))
    convb = convb.reshape(depth, nseq * seq_len, GROUP_W)
    ys, conv_s, gla_s, swak_s, swav_s = _sample_layers(
        x_sample.reshape(nseq * seq_len, D_MODEL), cos_s, sin_s, convb, state_gla,
        _feature_major(cache_swa_k), _feature_major(cache_swa_v),
        _feature_major(cache_mem_k), _feature_major(cache_mem_v),
        gpre, gpost, wt, wout, conv_w, wg, bg, gn, sinks, jmat, place, seq_len, group)

    hp = _to_kernel_order(x_prompt, 1)
    outs_p = [[] for _ in range(6)]
    for l in range(depth):
        res = _prompt_layer(l, hp, mem_prompt, cos_p, sin_p, gpre, gpost, wt, wout, wmem, conv_w, wg, bg, gn,
                            sinks, jmat, lvl, swab)
        hp = res[0]
        for i in range(6):
            outs_p[i].append(res[i + 1])

    return (_from_kernel_order(hp, 1), ys.reshape(nseq, seq_len, D_MODEL),
            jnp.stack(outs_p[0]), jnp.stack(outs_p[1]),
            _position_major(jnp.stack(outs_p[2]), 2), _position_major(jnp.stack(outs_p[3]), 2),
            _position_major(jnp.stack(outs_p[4]), N_HEADS), _position_major(jnp.stack(outs_p[5]), N_HEADS),
            conv_s, gla_s, _position_major(swak_s, 2), _position_major(swav_s, 2))
```

```python
import functools

import jax
import jax.numpy as jnp
import numpy as np
from jax import lax
from jax.experimental import pallas as pl
from jax.experimental.pallas import tpu as pltpu

f32 = jnp.float32
bf16 = jnp.bfloat16

D_MODEL = 1024
GROUP_W = 256
HEAD_D = 64
N_HEADS = 4
SWA_KV_W = 128
N_MEM = 256
WINDOW = 128
CONV_W = 3
GATE_RANK = 16
GATE_NORM = 16.0
GLA_BLOCK = 16
ROPE_THETA = 10000.0
PAST_LEN = 8192
EPS = 1e-6
NEG = -1e30
QK_SCALE = HEAD_D ** -0.5

LANES = 128
SUBLANES = 8
VMEM_LIMIT_BYTES = 56 * 1024 * 1024

C_CX, C_CB, C_CC, C_CZ = 0, 256, 512, 768
C_GQ, C_GK, C_GV, C_GZ = 1024, 1280, 1536, 1792
C_SQ, C_SK, C_SV, C_SZ = 2048, 2304, 2432, 2560
C_MQ, C_MZ = 2816, 3072
C_LR = 3328
NP = 3456
IN_WIDTH = 3344
_O_GLR, _O_GZ = 1792, 1808

SUB = 128
TL = 512


def _dot(a, b):
    return jnp.dot(a, b, preferred_element_type=f32)


def _dot_nt(a, b):
    return lax.dot_general(a, b, (((1,), (1,)), ((), ())), preferred_element_type=f32)


def _dot_tn(a, b):
    return lax.dot_general(a, b, (((0,), (0,)), ((), ())), preferred_element_type=f32)


def _rmsnorm(x, g):
    return x * lax.rsqrt(jnp.mean(x * x, axis=-1, keepdims=True) + EPS) * g


def _silu(z):
    return z * (0.5 + 0.5 * jnp.tanh(0.5 * z))


def _log_sigmoid(x):
    return jnp.minimum(x, 0.0) - jnp.log1p(jnp.exp(-jnp.abs(x)))


PROJ_PIECES = tuple([(c, c + 256, c) for c in range(0, _O_GLR, 256)]
                    + [(_O_GZ + c, _O_GZ + c + 256, _O_GLR + c) for c in range(0, IN_WIDTH - _O_GZ, 256)]
                    + [(_O_GLR, _O_GZ, C_LR)])


def _project_piece(h, wt_ref, p_ref, rows, piece):
    w0, w1, c0 = PROJ_PIECES[piece]
    p_ref[rows, c0:c0 + (w1 - w0)] = _dot_nt(h, wt_ref[w0:w1, :])


def _project(h, wt_ref, p_ref, rows):
    for piece in range(len(PROJ_PIECES)):
        _project_piece(h, wt_ref, p_ref, rows, piece)


def _head_sum(x, j_ref):
    hi = x.astype(bf16)
    lo = (x - hi.astype(f32)).astype(bf16)
    return _dot(hi, j_ref[...]) + _dot(lo, j_ref[...])


def _rope(x, cos, sin_signed):
    lane = lax.broadcasted_iota(jnp.int32, x.shape, 1)
    swapped = jnp.where((lane % HEAD_D) < HEAD_D // 2,
                        pltpu.roll(x, LANES - HEAD_D // 2, 1), pltpu.roll(x, HEAD_D // 2, 1))
    return x * cos + swapped * sin_signed


def _stack_heads(x):
    lane_head = lax.broadcasted_iota(jnp.int32, x.shape, 1) // HEAD_D
    return jnp.concatenate([jnp.where(lane_head == h, x, 0.0) for h in range(N_HEADS)], axis=0)


def _unstack_heads(o, n):
    lane_head = lax.broadcasted_iota(jnp.int32, (n, GROUP_W), 1) // HEAD_D
    out = o[0:n]
    for h in range(1, N_HEADS):
        out = jnp.where(lane_head == h, o[h * n:(h + 1) * n], out)
    return out


def _stack_swa_q(q01, q23):
    low = lax.broadcasted_iota(jnp.int32, q01.shape, 1) < HEAD_D
    return jnp.concatenate([jnp.where(low, q01, 0.0), jnp.where(low, pltpu.roll(q01, HEAD_D, 1), 0.0),
                            jnp.where(low, 0.0, pltpu.roll(q23, HEAD_D, 1)), jnp.where(low, 0.0, q23)], axis=0)


def _unstack_swa(o, n):
    low = lax.broadcasted_iota(jnp.int32, (n, SWA_KV_W), 1) < HEAD_D
    c01 = jnp.where(low, o[0:n], pltpu.roll(o[n:2 * n], HEAD_D, 1))
    c23 = jnp.where(low, pltpu.roll(o[2 * n:3 * n], HEAD_D, 1), o[3 * n:4 * n])
    return jnp.concatenate([c01, c23], axis=1)


def _sink_column(sinks, n):
    return jnp.concatenate([jnp.full((n, 1), s, f32) for s in sinks], axis=0)


def _block_cumsum(la, block):
    row = lax.broadcasted_iota(jnp.int32, la.shape, 0) % block
    b = la
    s = 1
    while s < block:
        b = b + jnp.where(row >= s, pltpu.roll(b, s, 0), 0.0)
        s *= 2
    return b


def _gate_log_decay(p_lr, wg_ref, bg_ref):
    pre = _dot(p_lr.astype(bf16), wg_ref[...]) + bg_ref[...]
    return _log_sigmoid(pre) * (1.0 / GATE_NORM)


def _gla_norm_gate(o, gz, gnorm_ref, j_ref):
    ms = _head_sum(o * o, j_ref) * (1.0 / HEAD_D)
    return o * lax.rsqrt(ms + EPS) * gnorm_ref[...] * _silu(gz)


GROUPS = SUB // SUBLANES
LEVEL_GROUPS = (1, 2, 4)


def _row_groups(x):
    return [x[SUBLANES * r:SUBLANES * (r + 1), :] for r in range(x.shape[0] // SUBLANES)]


def _prompt_layer_kernel(*refs):
    t = pl.program_id(1)
    for parity in range(2):
        pl.when(t % 2 == parity)(functools.partial(_prompt_layer_step, parity, *refs))


def _prompt_layer_step(parity, sinks_ref, x_ref, memp_ref, cos_ref, sin_ref, gpre_ref, gpost_ref, wt_ref,
                       wout_ref, wmem_ref, convw_ref, wg_ref, bg_ref, gnorm_ref, j_ref, lvl_ref, swab_ref,
                       y_ref, conv_out, gla_out, swak_out, swav_out, mk_out, mv_out,
                       p_buf, x_prev, mix_scr, cbuf, st_scr, kprev, vprev, vtprev, mkb, mvtb, kv_nat):
    t = pl.program_id(1)
    last_step = pl.num_programs(1) - 1
    nsub = TL // SUB
    p_wr = p_buf.at[parity]
    p_scr = p_buf.at[1 - parity]

    @pl.when(t == 0)
    def _init():
        p_buf[1] = jnp.zeros((TL, NP), f32)
        x_prev[...] = jnp.zeros((TL, D_MODEL), f32)
        cbuf[...] = jnp.zeros((SUBLANES, GROUP_W), f32)
        st_scr[...] = jnp.zeros((GROUP_W, GROUP_W), f32)
        kprev[...] = jnp.zeros((SUB, SWA_KV_W), f32)
        vprev[...] = jnp.zeros((SUB, SWA_KV_W), f32)
        vtprev[...] = jnp.zeros((SWA_KV_W, SUB), f32)
        mkv = _dot(memp_ref[0].astype(bf16), wmem_ref[...])
        mvt = mkv[:, GROUP_W:2 * GROUP_W].T
        mk_out[0] = mkv[:, 0:GROUP_W].T
        mv_out[0] = mvt
        mkb[...] = mkv[:, 0:GROUP_W].astype(bf16)
        mvtb[...] = mvt.astype(bf16)

    x = x_ref[0]
    h = _rmsnorm(x, gpre_ref[...]).astype(bf16)

    sub8 = lax.broadcasted_iota(jnp.int32, (SUBLANES, GROUP_W), 0)
    pending = list(range(len(PROJ_PIECES)))

    def project_next():
        if pending:
            _project_piece(h, wt_ref, p_wr, slice(None), pending.pop(0))

    def sub_tile(j):
        rows = slice(j * SUB, (j + 1) * SUB)
        project_next()

        u = p_scr[rows, C_CC:C_CC + GROUP_W] * p_scr[rows, C_CX:C_CX + GROUP_W]
        last = SUB - SUBLANES
        prev1 = jnp.where(sub8 == 0, cbuf[0:1, :], pltpu.roll(u[last:SUB, :], 1, 0))
        prev2 = jnp.where(sub8 == 0, cbuf[1:2, :], pltpu.roll(u[last - SUBLANES:last, :], 1, 0))
        um1 = jnp.concatenate([prev1, u[0:last, :]], axis=0)
        um2 = jnp.concatenate([prev2, prev1, u[0:last - SUBLANES, :]], axis=0)
        cy = convw_ref[0:1, :] * um2 + convw_ref[1:2, :] * um1 + convw_ref[2:3, :] * u
        cbuf[0:1, :] = u[SUB - 1:SUB, :]
        cbuf[1:2, :] = u[last - 1:last, :]
        a_out = p_scr[rows, C_CB:C_CB + GROUP_W] * cy * _silu(p_scr[rows, C_CZ:C_CZ + GROUP_W])
        mix_scr[rows, 0:GROUP_W] = a_out.astype(bf16)

        qg = _row_groups(p_scr[rows, C_GQ:C_GQ + GROUP_W] * QK_SCALE)
        k = p_scr[rows, C_GK:C_GK + GROUP_W]
        v = p_scr[rows, C_GV:C_GV + GROUP_W]
        kg = _row_groups(k)
        vg = _row_groups(v)
        lag = _row_groups(_gate_log_decay(p_scr[rows, C_LR:C_LR + GATE_RANK], wg_ref, bg_ref))
        bg_ = [lag[0]]
        for r in range(1, GROUPS):
            bg_.append(bg_[-1] + lag[r])
        tot = bg_[GROUPS - 1]
        pw = []
        for r in range(GROUPS):
            for s in range(r):
                pw.append(qg[r] * kg[s] * jnp.exp(bg_[r] - bg_[s]))
            pw.append(qg[r] * kg[r])
        scores = _dot(jnp.concatenate(pw, axis=0).astype(bf16), j_ref[...])
        project_next()
        og = []
        idx = 0
        for r in range(GROUPS):
            acc = None
            for s in range(r + 1):
                term = scores[SUBLANES * idx:SUBLANES * (idx + 1), :] * vg[s]
                acc = term if acc is None else acc + term
                idx += 1
            og.append(acc)
        o = jnp.concatenate(og, axis=0)

        def decayed(group):
            before = jnp.zeros((SUBLANES, GROUP_W), f32)
            after = jnp.zeros((SUBLANES, GROUP_W), f32)
            for s in range(1, group):
                before = before + jnp.where(sub8 % group >= s, pltpu.roll(tot, s, 0), 0.0)
                after = after + jnp.where(sub8 % group < group - s, pltpu.roll(tot, SUBLANES - s, 0), 0.0)
            qd = jnp.concatenate([qg[r] * jnp.exp(bg_[r] + before) for r in range(GROUPS)], axis=0)
            kd = jnp.concatenate([kg[r] * jnp.exp((tot - bg_[r]) + after) for r in range(GROUPS)], axis=0)
            return qd, kd

        attn = None
        for li, group in enumerate(LEVEL_GROUPS):
            qd, kd = decayed(group)
            s = _dot_nt(_stack_heads(qd).astype(bf16), kd.astype(bf16)) * lvl_ref[li]
            attn = s if attn is None else attn + s
        o = o + _unstack_heads(_dot(attn.astype(bf16), v.astype(bf16)), SUB)
        project_next()
        qd, kd = decayed(SUB // GLA_BLOCK)
        st = st_scr[...]
        o = o + _dot_nt(qd.astype(bf16), st.astype(bf16))
        total = jnp.sum(tot, axis=0, keepdims=True)
        upd = _dot_tn(v.astype(bf16), kd.astype(bf16))
        same_head = (lax.broadcasted_iota(jnp.int32, (GROUP_W, GROUP_W), 0) // HEAD_D
                     == lax.broadcasted_iota(jnp.int32, (GROUP_W, GROUP_W), 1) // HEAD_D)
        st_scr[...] = st * jnp.exp(total) + jnp.where(same_head, upd, 0.0)
        b_out = _gla_norm_gate(o, p_scr[rows, C_GZ:C_GZ + GROUP_W], gnorm_ref, j_ref)
        mix_scr[rows, GROUP_W:2 * GROUP_W] = b_out.astype(bf16)

        cos = cos_ref[rows, :]
        sin = sin_ref[rows, :]
        q01 = _rope(p_scr[rows, C_SQ:C_SQ + LANES], cos, sin) * QK_SCALE
        q23 = _rope(p_scr[rows, C_SQ + LANES:C_SQ + 2 * LANES], cos, sin) * QK_SCALE
        kr = _rope(p_scr[rows, C_SK:C_SK + SWA_KV_W], cos, sin)
        vx = p_scr[rows, C_SV:C_SV + SWA_KV_W]
        k_all = jnp.concatenate([kprev[...], kr], axis=0).astype(bf16)
        vt_new = vx.T
        vt_all = jnp.concatenate([vtprev[...], vt_new], axis=1).astype(bf16)
        first = jnp.where(((t - 1) * nsub + j) == 0, 1, 0)
        sc = _dot_nt(k_all, _stack_swa_q(q01, q23).astype(bf16)) + swab_ref[first]
        lane_head = lax.broadcasted_iota(jnp.int32, (1, N_HEADS * SUB), 1) // SUB
        sink = jnp.full((1, N_HEADS * SUB), sinks_ref[0], f32)
        for hh in range(1, N_HEADS):
            sink = jnp.where(lane_head == hh, sinks_ref[hh], sink)
        mx = jnp.maximum(jnp.max(sc, axis=0, keepdims=True), sink)
        e = jnp.exp(sc - mx)
        den = jnp.sum(e, axis=0, keepdims=True) + jnp.exp(sink - mx)
        ot = _dot(vt_all, e.astype(bf16)) * (1.0 / den)
        oc = _unstack_swa(jnp.concatenate([ot[:, SUB * hh:SUB * (hh + 1)].T for hh in range(N_HEADS)], axis=0), SUB)
        mix_scr[rows, 2 * GROUP_W:3 * GROUP_W] = (oc * _silu(p_scr[rows, C_SZ:C_SZ + GROUP_W])).astype(bf16)
        kprev[...] = kr
        vprev[...] = vx
        vtprev[...] = vt_new
        project_next()

        qm = _stack_heads(p_scr[rows, C_MQ:C_MQ + GROUP_W] * QK_SCALE).astype(bf16)
        sm = _dot_nt(mkb[...], qm)
        mm = jnp.max(sm, axis=0, keepdims=True)
        em = jnp.exp(sm - mm)
        dm = jnp.sum(em, axis=0, keepdims=True)
        odt = _dot(mvtb[...], em.astype(bf16)) * (1.0 / dm)
        low = lax.broadcasted_iota(jnp.int32, (SUB, LANES), 1) < HEAD_D
        halves = []
        for pair in range(N_HEADS // 2):
            blk_rows = odt[LANES * pair:LANES * (pair + 1), :]
            even = blk_rows[:, SUB * (2 * pair):SUB * (2 * pair + 1)].T
            odd = blk_rows[:, SUB * (2 * pair + 1):SUB * (2 * pair + 2)].T
            halves.append(jnp.where(low, even, odd))
        od = jnp.concatenate(halves, axis=1)
        mix_scr[rows, 3 * GROUP_W:4 * GROUP_W] = (od * _silu(p_scr[rows, C_MZ:C_MZ + GROUP_W])).astype(bf16)

    for j in range(nsub):
        sub_tile(j)
    while pending:
        project_next()

    mo = _dot(mix_scr[...], wout_ref[...])
    y_ref[0] = x_prev[...] + _rmsnorm(mo, gpost_ref[...])
    x_prev[...] = x

    @pl.when(t == last_step)
    def _final():
        conv_out[0, 0:1, :] = cbuf[1:2, :]
        conv_out[0, 1:2, :] = cbuf[0:1, :]
        for src, dst in ((kprev, swak_out), (vprev, swav_out)):
            for r in range(GROUPS):
                kv_nat[pl.ds(r, SUBLANES, stride=GLA_BLOCK), :] = src[SUBLANES * r:SUBLANES * (r + 1), :]
            dst[0] = kv_nat[...].T
        st = st_scr[...]
        for hh in range(N_HEADS):
            gla_out[0, hh] = st[HEAD_D * hh:HEAD_D * (hh + 1), HEAD_D * hh:HEAD_D * (hh + 1)].T


def _layer_spec(shape, l):
    nd = len(shape)
    return pl.BlockSpec((None,) + tuple(shape), lambda b, t, _l=l, _nd=nd: (_l,) + (0,) * _nd)


def _const_spec(shape):
    nd = len(shape)
    return pl.BlockSpec(shape, lambda b, t, _nd=nd: (0,) * _nd)


def _prompt_layer(l, x, memp, cos, sin, gpre, gpost, wt, wout, wmem, convw, wg, bg, gnorm, sinks, jmat,
                  lvl, swab):
    B, L, _ = x.shape
    assert L % TL == 0
    nt = L // TL
    out_shape = (
        jax.ShapeDtypeStruct((B, L, D_MODEL), f32),
        jax.ShapeDtypeStruct((B, CONV_W - 1, GROUP_W), f32),
        jax.ShapeDtypeStruct((B, N_HEADS, HEAD_D, HEAD_D), f32),
        jax.ShapeDtypeStruct((B, SWA_KV_W, WINDOW), f32),
        jax.ShapeDtypeStruct((B, SWA_KV_W, WINDOW), f32),
        jax.ShapeDtypeStruct((B, GROUP_W, N_MEM), f32),
        jax.ShapeDtypeStruct((B, GROUP_W, N_MEM), f32),
    )
    in_specs = [
        pl.BlockSpec(memory_space=pltpu.SMEM),
        pl.BlockSpec((1, TL, D_MODEL), lambda b, t: (b, jnp.minimum(t, nt - 1), 0)),
        pl.BlockSpec((1, N_MEM, D_MODEL), lambda b, t: (b, 0, 0)),
        pl.BlockSpec((TL, LANES), lambda b, t: (jnp.maximum(t - 1, 0), 0)),
        pl.BlockSpec((TL, LANES), lambda b, t: (jnp.maximum(t - 1, 0), 0)),
        _layer_spec((1, D_MODEL), l), _layer_spec((1, D_MODEL), l),
        _layer_spec((IN_WIDTH, D_MODEL), l),
        _layer_spec((D_MODEL, D_MODEL), l), _layer_spec((D_MODEL, 2 * GROUP_W), l),
        _layer_spec((CONV_W, GROUP_W), l), _layer_spec((GATE_RANK, GROUP_W), l), _layer_spec((1, GROUP_W), l),
        _layer_spec((1, GROUP_W), l), _const_spec((GROUP_W, GROUP_W)),
        _const_spec((len(LEVEL_GROUPS), N_HEADS * SUB, SUB)), _const_spec((2, 2 * SUB, N_HEADS * SUB)),
    ]
    out_specs = (
        pl.BlockSpec((1, TL, D_MODEL), lambda b, t: (b, jnp.maximum(t - 1, 0), 0)),
        pl.BlockSpec((1, CONV_W - 1, GROUP_W), lambda b, t: (b, 0, 0)),
        pl.BlockSpec((1, N_HEADS, HEAD_D, HEAD_D), lambda b, t: (b, 0, 0, 0)),
        pl.BlockSpec((1, SWA_KV_W, WINDOW), lambda b, t: (b, 0, 0)),
        pl.BlockSpec((1, SWA_KV_W, WINDOW), lambda b, t: (b, 0, 0)),
        pl.BlockSpec((1, GROUP_W, N_MEM), lambda b, t: (b, 0, 0)),
        pl.BlockSpec((1, GROUP_W, N_MEM), lambda b, t: (b, 0, 0)),
    )
    scratch = [
        pltpu.VMEM((2, TL, NP), f32),
        pltpu.VMEM((TL, D_MODEL), f32),
        pltpu.VMEM((TL, D_MODEL), bf16),
        pltpu.VMEM((SUBLANES, GROUP_W), f32),
        pltpu.VMEM((GROUP_W, GROUP_W), f32),
        pltpu.VMEM((SUB, SWA_KV_W), f32),
        pltpu.VMEM((SUB, SWA_KV_W), f32),
        pltpu.VMEM((SWA_KV_W, SUB), f32),
        pltpu.VMEM((N_MEM, GROUP_W), bf16),
        pltpu.VMEM((GROUP_W, N_MEM), bf16),
        pltpu.VMEM((WINDOW, SWA_KV_W), f32),
    ]
    return pl.pallas_call(
        _prompt_layer_kernel,
        grid=(B, nt + 1),
        in_specs=in_specs,
        out_specs=out_specs,
        out_shape=out_shape,
        scratch_shapes=scratch,
        compiler_params=pltpu.CompilerParams(dimension_semantics=("arbitrary", "arbitrary"),
                                             vmem_limit_bytes=VMEM_LIMIT_BYTES),
        name="prompt_layer",
    )(sinks[l], x, memp, cos, sin, gpre, gpost, wt, wout, wmem, convw, wg, bg, gnorm, jmat, lvl, swab)


def _subtile_constants():
    row = np.arange(SUB)
    tok = (row % SUBLANES) * GLA_BLOCK + row // SUBLANES
    blk = row % SUBLANES
    levels = []
    for group in LEVEL_GROUPS:
        g = blk // group
        sel = (g[:, None] == g[None, :] + 1) & (g[:, None] % 2 == 1)
        levels.append(np.tile(sel, (N_HEADS, 1)))
    key_tok = np.concatenate([tok - SUB, tok])
    valid = (key_tok[None, :] <= tok[:, None]) & (key_tok[None, :] > tok[:, None] - WINDOW)
    bias = [np.where(valid, 0.0, NEG), np.where(valid & (key_tok[None, :] >= 0), 0.0, NEG)]
    bias = np.stack([np.tile(b, (N_HEADS, 1)).T for b in bias])
    return jnp.asarray(np.stack(levels), dtype=f32), jnp.asarray(bias, dtype=f32)


def _to_kernel_order(x, axis):
    shp = x.shape
    n = shp[axis]
    x = x.reshape(shp[:axis] + (n // SUB, SUBLANES, GLA_BLOCK) + shp[axis + 1:])
    return jnp.swapaxes(x, axis + 1, axis + 2).reshape(shp)


def _from_kernel_order(x, axis):
    shp = x.shape
    n = shp[axis]
    x = x.reshape(shp[:axis] + (n // SUB, GLA_BLOCK, SUBLANES) + shp[axis + 1:])
    return jnp.swapaxes(x, axis + 1, axis + 2).reshape(shp)


def _sample_kernel(seq_len, group, sinks_ref, x_hbm, cos_ref, sin_ref, convb_ref, sgla_ref, kc_ref, vc_ref,
                   mkc_ref, mvc_ref, gpre_ref, gpost_ref, wt_ref, wout_ref, convw_ref, wg_ref, bg_ref,
                   gnorm_ref, j_ref, place_ref,
                   y_hbm, conv_out, gla_out, swak_out, swav_out,
                   hs, p_scr, mix_scr, sem):
    l = pl.program_id(0)
    g_step = pl.program_id(1)
    n_layers = pl.num_programs(0)
    n_steps = pl.num_programs(1)
    n_tok = hs.shape[0]
    n = group * seq_len
    chunks = [slice(c * TL, (c + 1) * TL) for c in range(n_tok // TL)]

    @pl.when((l == 0) & (g_step == 0))
    def _load():
        cp = pltpu.make_async_copy(x_hbm, hs, sem.at[0])
        cp.start()
        cp.wait()

    @pl.when(g_step == 0)
    def _project_all():
        for rows in chunks:
            h = _rmsnorm(hs[rows, :], gpre_ref[...]).astype(bf16)
            _project(h, wt_ref, p_scr, rows)

    rows = pl.ds(pl.multiple_of(g_step * n, n), n)
    row_t = lax.broadcasted_iota(jnp.int32, (n, GROUP_W), 0) % seq_len

    u = p_scr[rows, C_CC:C_CC + GROUP_W] * p_scr[rows, C_CX:C_CX + GROUP_W]
    hist = convb_ref[...]
    um1 = jnp.where(row_t >= 1, pltpu.roll(u, 1, 0), pltpu.roll(hist, n - 1, 0))
    um2 = jnp.where(row_t >= 2, pltpu.roll(u, 2, 0), hist)
    cy = convw_ref[0:1, :] * um2 + convw_ref[1:2, :] * um1 + convw_ref[2:3, :] * u
    a_out = p_scr[rows, C_CB:C_CB + GROUP_W] * cy * _silu(p_scr[rows, C_CZ:C_CZ + GROUP_W])
    mix_scr[rows, 0:GROUP_W] = a_out.astype(bf16)
    for g in range(group):
        conv_out[g] = u[g * seq_len + seq_len - (CONV_W - 1):(g + 1) * seq_len, :]

    qs = p_scr[rows, C_GQ:C_GQ + GROUP_W] * QK_SCALE
    k = p_scr[rows, C_GK:C_GK + GROUP_W]
    v = p_scr[rows, C_GV:C_GV + GROUP_W]
    la = _gate_log_decay(p_scr[rows, C_LR:C_LR + GATE_RANK], wg_ref, bg_ref)
    bc = _block_cumsum(la, seq_len)
    o = _dot((qs * k).astype(bf16), j_ref[...]) * v
    for d in range(1, seq_len):
        dec = jnp.exp(jnp.where(row_t >= d, bc - pltpu.roll(bc, d, 0), NEG))
        pw = qs * pltpu.roll(k, d, 0) * dec
        o = o + _dot(pw.astype(bf16), j_ref[...]) * pltpu.roll(v, d, 0)
    qd = qs * jnp.exp(bc)
    tot = [bc[(g + 1) * seq_len - 1:(g + 1) * seq_len, :] for g in range(group)]
    kd = k * jnp.exp(jnp.concatenate(
        [tot[g] - bc[g * seq_len:(g + 1) * seq_len, :] for g in range(group)], axis=0))
    o_inter = []
    for g in range(group):
        rs = slice(g * seq_len, (g + 1) * seq_len)
        alpha = jnp.exp(tot[g])
        per_head = []
        for hh in range(N_HEADS):
            ls = slice(hh * HEAD_D, (hh + 1) * HEAD_D)
            st0 = sgla_ref[g, hh]
            per_head.append(_dot_nt(qd[rs, ls].astype(bf16), st0.astype(bf16)))
            gla_out[g, hh] = st0 * alpha[:, ls] + _dot_tn(v[rs, ls].astype(bf16), kd[rs, ls].astype(bf16))
        o_inter.append(jnp.concatenate(per_head, axis=1))
    o = o + jnp.concatenate(o_inter, axis=0)
    b_out = _gla_norm_gate(o, p_scr[rows, C_GZ:C_GZ + GROUP_W], gnorm_ref, j_ref)
    mix_scr[rows, GROUP_W:2 * GROUP_W] = b_out.astype(bf16)

    cos = cos_ref[...]
    sin = sin_ref[...]
    q01 = _rope(p_scr[rows, C_SQ:C_SQ + LANES], cos, sin) * QK_SCALE
    q23 = _rope(p_scr[rows, C_SQ + LANES:C_SQ + 2 * LANES], cos, sin) * QK_SCALE
    kr = _rope(p_scr[rows, C_SK:C_SK + SWA_KV_W], cos, sin)
    vx = p_scr[rows, C_SV:C_SV + SWA_KV_W]
    qm = p_scr[rows, C_MQ:C_MQ + GROUP_W] * QK_SCALE
    hq = N_HEADS * seq_len
    sink_seq = _sink_column([sinks_ref[l, hh] for hh in range(N_HEADS)], seq_len)
    sink = jnp.concatenate([sink_seq] * group, axis=0)
    hq_all = group * hq
    qrow = lax.broadcasted_iota(jnp.int32, (hq_all, WINDOW), 0) % seq_len
    ccol = lax.broadcasted_iota(jnp.int32, (hq_all, WINDOW), 1)
    cache_valid = ccol > qrow
    nrow = lax.broadcasted_iota(jnp.int32, (hq_all, seq_len), 0) % seq_len
    ncol = lax.broadcasted_iota(jnp.int32, (hq_all, seq_len), 1)
    new_valid = ncol <= nrow
    keep_old = lax.broadcasted_iota(jnp.int32, (SWA_KV_W, WINDOW), 1) < WINDOW - seq_len

    def exact_split(a):
        hi = a.astype(bf16)
        r1 = a - hi.astype(f32)
        mid = r1.astype(bf16)
        lo = (r1 - mid.astype(f32)).astype(bf16)
        return jnp.concatenate([hi, mid, lo], axis=0)

    def shifted_cache(old_t, new_rows):
        placed = _dot_tn(exact_split(new_rows), place_ref[...])
        return jnp.where(keep_old, pltpu.roll(old_t, WINDOW - seq_len, 1), placed)

    seqs = [slice(g * seq_len, (g + 1) * seq_len) for g in range(group)]
    stk = [slice(g * hq, (g + 1) * hq) for g in range(group)]
    qst = [_stack_swa_q(q01[rs], q23[rs]).astype(bf16) for rs in seqs]
    s_c = jnp.concatenate([_dot(qst[g], kc_ref[g].astype(bf16)) for g in range(group)], axis=0)
    s_n = jnp.concatenate([_dot_nt(qst[g], kr[seqs[g]].astype(bf16)) for g in range(group)], axis=0)
    s_c = jnp.where(cache_valid, s_c, NEG)
    s_n = jnp.where(new_valid, s_n, NEG)
    mx = jnp.maximum(jnp.maximum(jnp.max(s_c, axis=1, keepdims=True), jnp.max(s_n, axis=1, keepdims=True)), sink)
    e_c = jnp.exp(s_c - mx)
    e_n = jnp.exp(s_n - mx)
    den = jnp.sum(e_c, axis=1, keepdims=True) + jnp.sum(e_n, axis=1, keepdims=True) + jnp.exp(sink - mx)
    e_c = e_c.astype(bf16)
    e_n = e_n.astype(bf16)
    ov = jnp.concatenate([_dot_nt(e_c[stk[g]], vc_ref[g].astype(bf16)) + _dot(e_n[stk[g]], vx[seqs[g]].astype(bf16))
                          for g in range(group)], axis=0) * (1.0 / den)
    oc = jnp.concatenate([_unstack_swa(ov[stk[g]], seq_len) for g in range(group)], axis=0)
    for g in range(group):
        swak_out[g] = shifted_cache(kc_ref[g], kr[seqs[g]])
        swav_out[g] = shifted_cache(vc_ref[g], vx[seqs[g]])

    sm = jnp.concatenate([_dot(_stack_heads(qm[seqs[g]]).astype(bf16), mkc_ref[g].astype(bf16))
                          for g in range(group)], axis=0)
    mm = jnp.max(sm, axis=1, keepdims=True)
    em = jnp.exp(sm - mm)
    dm = jnp.sum(em, axis=1, keepdims=True)
    em = em.astype(bf16)
    odv = jnp.concatenate([_dot_nt(em[stk[g]], mvc_ref[g].astype(bf16)) for g in range(group)], axis=0) * (1.0 / dm)
    od = jnp.concatenate([_unstack_heads(odv[stk[g]], seq_len) for g in range(group)], axis=0)
    mix_scr[rows, 2 * GROUP_W:3 * GROUP_W] = (oc * _silu(p_scr[rows, C_SZ:C_SZ + GROUP_W])).astype(bf16)
    mix_scr[rows, 3 * GROUP_W:4 * GROUP_W] = (od * _silu(p_scr[rows, C_MZ:C_MZ + GROUP_W])).astype(bf16)

    @pl.when(g_step == n_steps - 1)
    def _residual():
        for rws in chunks:
            mo = _dot(mix_scr[rws, :], wout_ref[...])
            hs[rws, :] = hs[rws, :] + _rmsnorm(mo, gpost_ref[...])

    @pl.when((l == n_layers - 1) & (g_step == n_steps - 1))
    def _store():
        cp = pltpu.make_async_copy(hs, y_hbm, sem.at[0])
        cp.start()
        cp.wait()


def _sample_layers(x, cos, sin, convb, sgla, kc, vc, mkc, mvc, gpre, gpost, wt, wout, convw, wg, bg, gnorm,
                   sinks, jmat, place, seq_len, group):
    depth, nseq = sgla.shape[0], sgla.shape[1]
    n_tok = nseq * seq_len
    assert nseq % group == 0 and seq_len == SUBLANES and n_tok % TL == 0
    n = group * seq_len

    def per_layer(shape, single_buffer=False):
        nd = len(shape)
        kw = dict(pipeline_mode=pl.Buffered(1)) if single_buffer else {}
        return pl.BlockSpec((None,) + tuple(shape), lambda l, g, _nd=nd: (l,) + (0,) * _nd, **kw)

    def per_group(shape):
        nd = len(shape) - 1
        return pl.BlockSpec((None,) + tuple(shape), lambda l, g, _nd=nd: (l, g) + (0,) * _nd)

    def const(shape):
        nd = len(shape)
        return pl.BlockSpec(shape, lambda l, g, _nd=nd: (0,) * _nd)

    in_specs = [
        pl.BlockSpec(memory_space=pltpu.SMEM),
        pl.BlockSpec(memory_space=pl.ANY),
        const((n, LANES)), const((n, LANES)),
        per_group((n, GROUP_W)),
        per_group((group, N_HEADS, HEAD_D, HEAD_D)),
        per_group((group, SWA_KV_W, WINDOW)), per_group((group, SWA_KV_W, WINDOW)),
        per_group((group, GROUP_W, N_MEM)), per_group((group, GROUP_W, N_MEM)),
        per_layer((1, D_MODEL)), per_layer((1, D_MODEL)),
        per_layer((IN_WIDTH, D_MODEL), True),
        per_layer((D_MODEL, D_MODEL), True),
        per_layer((CONV_W, GROUP_W)), per_layer((GATE_RANK, GROUP_W)), per_layer((1, GROUP_W)), per_layer((1, GROUP_W)),
        const((GROUP_W, GROUP_W)), const((3 * seq_len, WINDOW)),
    ]
    out_shape = (
        jax.ShapeDtypeStruct((n_tok, D_MODEL), f32),
        jax.ShapeDtypeStruct((depth, nseq, CONV_W - 1, GROUP_W), f32),
        jax.ShapeDtypeStruct((depth, nseq, N_HEADS, HEAD_D, HEAD_D), f32),
        jax.ShapeDtypeStruct((depth, nseq, SWA_KV_W, WINDOW), f32),
        jax.ShapeDtypeStruct((depth, nseq, SWA_KV_W, WINDOW), f32),
    )
    out_specs = (
        pl.BlockSpec(memory_space=pl.ANY),
        per_group((group, CONV_W - 1, GROUP_W)),
        per_group((group, N_HEADS, HEAD_D, HEAD_D)),
        per_group((group, SWA_KV_W, WINDOW)), per_group((group, SWA_KV_W, WINDOW)),
    )
    scratch = [
        pltpu.VMEM((n_tok, D_MODEL), f32),
        pltpu.VMEM((n_tok, NP), f32),
        pltpu.VMEM((n_tok, D_MODEL), bf16),
        pltpu.SemaphoreType.DMA((1,)),
    ]
    return pl.pallas_call(
        functools.partial(_sample_kernel, seq_len, group),
        grid=(depth, nseq // group),
        in_specs=in_specs,
        out_specs=out_specs,
        out_shape=out_shape,
        scratch_shapes=scratch,
        compiler_params=pltpu.CompilerParams(dimension_semantics=("arbitrary", "arbitrary"),
                                             vmem_limit_bytes=VMEM_LIMIT_BYTES),
        name="sample_layers",
    )(sinks, x, cos, sin, convb, sgla, kc, vc, mkc, mvc, gpre, gpost, wt, wout, convw, wg, bg, gnorm,
      jmat, place)


def _rope_tables(pos):
    half = HEAD_D // 2
    inv = jnp.power(ROPE_THETA, -jnp.arange(half, dtype=f32) / half)
    ang = pos.astype(f32)[:, None] * inv[None, :]
    cos = jnp.tile(jnp.cos(ang), (1, LANES // half))
    sin = jnp.sin(ang)
    sin_signed = jnp.tile(jnp.concatenate([-sin, sin], axis=1), (1, LANES // HEAD_D))
    return cos, sin_signed


def _feature_major(cache):
    d, s, p, h, e = cache.shape
    return jnp.transpose(cache, (0, 1, 3, 4, 2)).reshape(d, s, h * e, p)


def _position_major(cache_t, heads):
    d, s, he, p = cache_t.shape
    return jnp.transpose(cache_t.reshape(d, s, heads, he // heads, p), (0, 1, 4, 2, 3))


def kernel(x_prompt, x_sample, state_conv, state_gla, cache_swa_k, cache_swa_v, cache_mem_k, cache_mem_v,
           mem_prompt, norm_pre, norm_post, w_in, conv_w, gla_w_gate, gla_b_gate, gla_norm, swa_sinks,
           w_mem_kv, w_out):
    depth = w_in.shape[0]
    B, L, _ = x_prompt.shape
    nseq, seq_len, _ = x_sample.shape
    group = 8

    wt = jnp.swapaxes(w_in, 1, 2).astype(bf16)
    wout = w_out.astype(bf16)
    wmem = w_mem_kv.astype(bf16)
    wg = gla_w_gate.astype(bf16)
    head_id = np.arange(GROUP_W) // HEAD_D
    jmat = jnp.asarray(head_id[:, None] == head_id[None, :], dtype=bf16)
    lvl, swab = _subtile_constants()
    place = np.zeros((3 * seq_len, WINDOW), np.float32)
    for piece in range(3):
        place[piece * seq_len + np.arange(seq_len), WINDOW - seq_len + np.arange(seq_len)] = 1.0
    place = jnp.asarray(place, dtype=bf16)
    sinks = swa_sinks.astype(f32)
    gpre = norm_pre[:, None, :]
    gpost = norm_post[:, None, :]
    bg = gla_b_gate[:, None, :]
    gn = gla_norm[:, None, :]

    cos_p, sin_p = _rope_tables(jnp.arange(L, dtype=jnp.int32))
    cos_p = _to_kernel_order(cos_p, 0)
    sin_p = _to_kernel_order(sin_p, 0)
    cos_s, sin_s = _rope_tables(PAST_LEN + jnp.arange(seq_len, dtype=jnp.int32))
    cos_s = jnp.tile(cos_s, (group, 1))
    sin_s = jnp.tile(sin_s, (group, 1))

    convb = jnp.pad(state_conv, ((0, 0), (0, 0), (0, seq_len - (CONV_W - 1)), (0, 0)))
    convb = convb.reshape(depth, nseq * seq_len, GROUP_W)
    ys, conv_s, gla_s, swak_s, swav_s = _sample_layers(
        x_sample.reshape(nseq * seq_len, D_MODEL), cos_s, sin_s, convb, jnp.swapaxes(state_gla, -1, -2),
        _feature_major(cache_swa_k), _feature_major(cache_swa_v),
        _feature_major(cache_mem_k), _feature_major(cache_mem_v),
        gpre, gpost, wt, wout, conv_w, wg, bg, gn, sinks, jmat, place, seq_len, group)

    hp = _to_kernel_order(x_prompt, 1)
    outs_p = [[] for _ in range(6)]
    for l in range(depth):
        res = _prompt_layer(l, hp, mem_prompt, cos_p, sin_p, gpre, gpost, wt, wout, wmem, conv_w, wg, bg, gn,
                            sinks, jmat, lvl, swab)
        hp = res[0]
        for i in range(6):
            outs_p[i].append(res[i + 1])

    return (_from_kernel_order(hp, 1), ys.reshape(nseq, seq_len, D_MODEL),
            jnp.stack(outs_p[0]), jnp.stack(outs_p[1]),
            _position_major(jnp.stack(outs_p[2]), 2), _position_major(jnp.stack(outs_p[3]), 2),
            _position_major(jnp.stack(outs_p[4]), N_HEADS), _position_major(jnp.stack(outs_p[5]), N_HEADS),
            conv_s, jnp.swapaxes(gla_s, -1, -2), _position_major(swak_s, 2), _position_major(swav_s, 2))
```

```python
import functools

import jax
import jax.numpy as jnp
import numpy as np
from jax import lax
from jax.experimental import pallas as pl
from jax.experimental.pallas import tpu as pltpu

f32 = jnp.float32
bf16 = jnp.bfloat16

D_MODEL = 1024
GROUP_W = 256
HEAD_D = 64
N_HEADS = 4
SWA_KV_W = 128
N_MEM = 256
WINDOW = 128
CONV_W = 3
GATE_RANK = 16
GATE_NORM = 16.0
GLA_BLOCK = 16
ROPE_THETA = 10000.0
PAST_LEN = 8192
EPS = 1e-6
NEG = -1e30
QK_SCALE = HEAD_D ** -0.5

LANES = 128
SUBLANES = 8
VMEM_LIMIT_BYTES = 56 * 1024 * 1024

C_CX, C_CB, C_CC, C_CZ = 0, 256, 512, 768
C_GQ, C_GK, C_GV, C_GZ = 1024, 1280, 1536, 1792
C_SQ, C_SK, C_SV, C_SZ = 2048, 2304, 2432, 2560
C_MQ, C_MZ = 2816, 3072
C_LR = 3328
NP = 3456
IN_WIDTH = 3344
_O_GLR, _O_GZ = 1792, 1808

SUB = 128
TL = 512


def _dot(a, b):
    return jnp.dot(a, b, preferred_element_type=f32)


def _dot_nt(a, b):
    return lax.dot_general(a, b, (((1,), (1,)), ((), ())), preferred_element_type=f32)


def _dot_tn(a, b):
    return lax.dot_general(a, b, (((0,), (0,)), ((), ())), preferred_element_type=f32)


def _rmsnorm(x, g):
    return x * lax.rsqrt(jnp.mean(x * x, axis=-1, keepdims=True) + EPS) * g


def _silu(z):
    return z * (0.5 + 0.5 * jnp.tanh(0.5 * z))


def _log_sigmoid(x):
    return jnp.minimum(x, 0.0) - jnp.log1p(jnp.exp(-jnp.abs(x)))


PROJ_PIECES = tuple([(c, c + 256, c) for c in range(0, _O_GLR, 256)]
                    + [(_O_GZ + c, _O_GZ + c + 256, _O_GLR + c) for c in range(0, IN_WIDTH - _O_GZ, 256)]
                    + [(_O_GLR, _O_GZ, C_LR)])


def _project_piece(h, wt_ref, p_ref, rows, piece):
    w0, w1, c0 = PROJ_PIECES[piece]
    p_ref[rows, c0:c0 + (w1 - w0)] = _dot_nt(h, wt_ref[w0:w1, :])


def _project(h, wt_ref, p_ref, rows):
    for piece in range(len(PROJ_PIECES)):
        _project_piece(h, wt_ref, p_ref, rows, piece)


def _head_sum(x, j_ref):
    hi = x.astype(bf16)
    lo = (x - hi.astype(f32)).astype(bf16)
    return _dot(hi, j_ref[...]) + _dot(lo, j_ref[...])


def _rope(x, cos, sin_signed):
    lane = lax.broadcasted_iota(jnp.int32, x.shape, 1)
    swapped = jnp.where((lane % HEAD_D) < HEAD_D // 2,
                        pltpu.roll(x, LANES - HEAD_D // 2, 1), pltpu.roll(x, HEAD_D // 2, 1))
    return x * cos + swapped * sin_signed


def _stack_heads(x):
    lane_head = lax.broadcasted_iota(jnp.int32, x.shape, 1) // HEAD_D
    return jnp.concatenate([jnp.where(lane_head == h, x, 0.0) for h in range(N_HEADS)], axis=0)


def _unstack_heads(o, n):
    lane_head = lax.broadcasted_iota(jnp.int32, (n, GROUP_W), 1) // HEAD_D
    out = o[0:n]
    for h in range(1, N_HEADS):
        out = jnp.where(lane_head == h, o[h * n:(h + 1) * n], out)
    return out


def _stack_swa_q(q01, q23):
    low = lax.broadcasted_iota(jnp.int32, q01.shape, 1) < HEAD_D
    return jnp.concatenate([jnp.where(low, q01, 0.0), jnp.where(low, pltpu.roll(q01, HEAD_D, 1), 0.0),
                            jnp.where(low, 0.0, pltpu.roll(q23, HEAD_D, 1)), jnp.where(low, 0.0, q23)], axis=0)


def _unstack_swa(o, n):
    low = lax.broadcasted_iota(jnp.int32, (n, SWA_KV_W), 1) < HEAD_D
    c01 = jnp.where(low, o[0:n], pltpu.roll(o[n:2 * n], HEAD_D, 1))
    c23 = jnp.where(low, pltpu.roll(o[2 * n:3 * n], HEAD_D, 1), o[3 * n:4 * n])
    return jnp.concatenate([c01, c23], axis=1)


def _sink_column(sinks, n):
    return jnp.concatenate([jnp.full((n, 1), s, f32) for s in sinks], axis=0)


def _block_cumsum(la, block):
    row = lax.broadcasted_iota(jnp.int32, la.shape, 0) % block
    b = la
    s = 1
    while s < block:
        b = b + jnp.where(row >= s, pltpu.roll(b, s, 0), 0.0)
        s *= 2
    return b


def _gate_log_decay(p_lr, wg_ref, bg_ref):
    pre = _dot(p_lr.astype(bf16), wg_ref[...]) + bg_ref[...]
    return _log_sigmoid(pre) * (1.0 / GATE_NORM)


def _gla_norm_gate(o, gz, gnorm_ref, j_ref):
    ms = _head_sum(o * o, j_ref) * (1.0 / HEAD_D)
    return o * lax.rsqrt(ms + EPS) * gnorm_ref[...] * _silu(gz)


GROUPS = SUB // SUBLANES
LEVEL_GROUPS = (1, 2, 4)


def _row_groups(x):
    return [x[SUBLANES * r:SUBLANES * (r + 1), :] for r in range(x.shape[0] // SUBLANES)]


def _prompt_layer_kernel(nt, *refs):
    s = pl.program_id(0)
    for parity in range(2):
        pl.when(s % 2 == parity)(functools.partial(_prompt_layer_step, nt, parity, *refs))


def _prompt_layer_step(nt, parity, sinks_ref, x_ref, memp_ref, cos_ref, sin_ref, gpre_ref, gpost_ref, wt_ref,
                       wout_ref, wmem_ref, convw_ref, wg_ref, bg_ref, gnorm_ref, j_ref, lvl_ref, swab_ref,
                       y_ref, conv_out, gla_out, swak_out, swav_out, mk_out, mv_out,
                       p_buf, x_prev, mix_scr, cbuf, st_scr, kprev, vprev, vtprev, mkb, mvtb, kv_nat):
    s = pl.program_id(0)
    t = jnp.maximum(s - 1, 0) % nt
    nsub = TL // SUB
    p_wr = p_buf.at[parity]
    p_scr = p_buf.at[1 - parity]

    @pl.when(s == 0)
    def _first_step():
        p_buf[1] = jnp.zeros((TL, NP), f32)
        x_prev[...] = jnp.zeros((TL, D_MODEL), f32)

    @pl.when(t == 0)
    def _new_sequence():
        cbuf[...] = jnp.zeros((SUBLANES, GROUP_W), f32)
        st_scr[...] = jnp.zeros((GROUP_W, GROUP_W), f32)
        kprev[...] = jnp.zeros((SUB, SWA_KV_W), f32)
        vprev[...] = jnp.zeros((SUB, SWA_KV_W), f32)
        vtprev[...] = jnp.zeros((SWA_KV_W, SUB), f32)
        mkv = _dot(memp_ref[0].astype(bf16), wmem_ref[...])
        mvt = mkv[:, GROUP_W:2 * GROUP_W].T
        mk_out[0] = mkv[:, 0:GROUP_W].T
        mv_out[0] = mvt
        mkb[...] = mkv[:, 0:GROUP_W].astype(bf16)
        mvtb[...] = mvt.astype(bf16)

    x = x_ref[0]
    h = _rmsnorm(x, gpre_ref[...]).astype(bf16)

    sub8 = lax.broadcasted_iota(jnp.int32, (SUBLANES, GROUP_W), 0)
    pending = list(range(len(PROJ_PIECES)))

    def project_next():
        if pending:
            _project_piece(h, wt_ref, p_wr, slice(None), pending.pop(0))

    def sub_tile(j):
        rows = slice(j * SUB, (j + 1) * SUB)
        project_next()

        u = p_scr[rows, C_CC:C_CC + GROUP_W] * p_scr[rows, C_CX:C_CX + GROUP_W]
        last = SUB - SUBLANES
        prev1 = jnp.where(sub8 == 0, cbuf[0:1, :], pltpu.roll(u[last:SUB, :], 1, 0))
        prev2 = jnp.where(sub8 == 0, cbuf[1:2, :], pltpu.roll(u[last - SUBLANES:last, :], 1, 0))
        um1 = jnp.concatenate([prev1, u[0:last, :]], axis=0)
        um2 = jnp.concatenate([prev2, prev1, u[0:last - SUBLANES, :]], axis=0)
        cy = convw_ref[0:1, :] * um2 + convw_ref[1:2, :] * um1 + convw_ref[2:3, :] * u
        cbuf[0:1, :] = u[SUB - 1:SUB, :]
        cbuf[1:2, :] = u[last - 1:last, :]
        a_out = p_scr[rows, C_CB:C_CB + GROUP_W] * cy * _silu(p_scr[rows, C_CZ:C_CZ + GROUP_W])
        mix_scr[rows, 0:GROUP_W] = a_out.astype(bf16)

        qg = _row_groups(p_scr[rows, C_GQ:C_GQ + GROUP_W] * QK_SCALE)
        k = p_scr[rows, C_GK:C_GK + GROUP_W]
        v = p_scr[rows, C_GV:C_GV + GROUP_W]
        kg = _row_groups(k)
        vg = _row_groups(v)
        lag = _row_groups(_gate_log_decay(p_scr[rows, C_LR:C_LR + GATE_RANK], wg_ref, bg_ref))
        bg_ = [lag[0]]
        for r in range(1, GROUPS):
            bg_.append(bg_[-1] + lag[r])
        tot = bg_[GROUPS - 1]
        pw = []
        for r in range(GROUPS):
            for s in range(r):
                pw.append(qg[r] * kg[s] * jnp.exp(bg_[r] - bg_[s]))
            pw.append(qg[r] * kg[r])
        scores = _dot(jnp.concatenate(pw, axis=0).astype(bf16), j_ref[...])
        project_next()
        og = []
        idx = 0
        for r in range(GROUPS):
            acc = None
            for s in range(r + 1):
                term = scores[SUBLANES * idx:SUBLANES * (idx + 1), :] * vg[s]
                acc = term if acc is None else acc + term
                idx += 1
            og.append(acc)
        o = jnp.concatenate(og, axis=0)

        def decayed(group):
            before = jnp.zeros((SUBLANES, GROUP_W), f32)
            after = jnp.zeros((SUBLANES, GROUP_W), f32)
            for s in range(1, group):
                before = before + jnp.where(sub8 % group >= s, pltpu.roll(tot, s, 0), 0.0)
                after = after + jnp.where(sub8 % group < group - s, pltpu.roll(tot, SUBLANES - s, 0), 0.0)
            qd = jnp.concatenate([qg[r] * jnp.exp(bg_[r] + before) for r in range(GROUPS)], axis=0)
            kd = jnp.concatenate([kg[r] * jnp.exp((tot - bg_[r]) + after) for r in range(GROUPS)], axis=0)
            return qd, kd

        attn = None
        for li, group in enumerate(LEVEL_GROUPS):
            qd, kd = decayed(group)
            s = _dot_nt(_stack_heads(qd).astype(bf16), kd.astype(bf16)) * lvl_ref[li]
            attn = s if attn is None else attn + s
        o = o + _unstack_heads(_dot(attn.astype(bf16), v.astype(bf16)), SUB)
        project_next()
        qd, kd = decayed(SUB // GLA_BLOCK)
        st = st_scr[...]
        o = o + _dot_nt(qd.astype(bf16), st.astype(bf16))
        total = jnp.sum(tot, axis=0, keepdims=True)
        upd = _dot_tn(v.astype(bf16), kd.astype(bf16))
        same_head = (lax.broadcasted_iota(jnp.int32, (GROUP_W, GROUP_W), 0) // HEAD_D
                     == lax.broadcasted_iota(jnp.int32, (GROUP_W, GROUP_W), 1) // HEAD_D)
        st_scr[...] = st * jnp.exp(total) + jnp.where(same_head, upd, 0.0)
        b_out = _gla_norm_gate(o, p_scr[rows, C_GZ:C_GZ + GROUP_W], gnorm_ref, j_ref)
        mix_scr[rows, GROUP_W:2 * GROUP_W] = b_out.astype(bf16)

        cos = cos_ref[rows, :]
        sin = sin_ref[rows, :]
        q01 = _rope(p_scr[rows, C_SQ:C_SQ + LANES], cos, sin) * QK_SCALE
        q23 = _rope(p_scr[rows, C_SQ + LANES:C_SQ + 2 * LANES], cos, sin) * QK_SCALE
        kr = _rope(p_scr[rows, C_SK:C_SK + SWA_KV_W], cos, sin)
        vx = p_scr[rows, C_SV:C_SV + SWA_KV_W]
        k_all = jnp.concatenate([kprev[...], kr], axis=0).astype(bf16)
        vt_new = vx.T
        vt_all = jnp.concatenate([vtprev[...], vt_new], axis=1).astype(bf16)
        first = jnp.where((t * nsub + j) == 0, 1, 0)
        sc = _dot_nt(k_all, _stack_swa_q(q01, q23).astype(bf16)) + swab_ref[first]
        lane_head = lax.broadcasted_iota(jnp.int32, (1, N_HEADS * SUB), 1) // SUB
        sink = jnp.full((1, N_HEADS * SUB), sinks_ref[0], f32)
        for hh in range(1, N_HEADS):
            sink = jnp.where(lane_head == hh, sinks_ref[hh], sink)
        mx = jnp.maximum(jnp.max(sc, axis=0, keepdims=True), sink)
        e = jnp.exp(sc - mx)
        den = jnp.sum(e, axis=0, keepdims=True) + jnp.exp(sink - mx)
        ot = _dot(vt_all, e.astype(bf16)) * (1.0 / den)
        oc = _unstack_swa(jnp.concatenate([ot[:, SUB * hh:SUB * (hh + 1)].T for hh in range(N_HEADS)], axis=0), SUB)
        mix_scr[rows, 2 * GROUP_W:3 * GROUP_W] = (oc * _silu(p_scr[rows, C_SZ:C_SZ + GROUP_W])).astype(bf16)
        kprev[...] = kr
        vprev[...] = vx
        vtprev[...] = vt_new
        project_next()

        qm = _stack_heads(p_scr[rows, C_MQ:C_MQ + GROUP_W] * QK_SCALE).astype(bf16)
        sm = _dot_nt(mkb[...], qm)
        mm = jnp.max(sm, axis=0, keepdims=True)
        em = jnp.exp(sm - mm)
        dm = jnp.sum(em, axis=0, keepdims=True)
        odt = _dot(mvtb[...], em.astype(bf16)) * (1.0 / dm)
        low = lax.broadcasted_iota(jnp.int32, (SUB, LANES), 1) < HEAD_D
        halves = []
        for pair in range(N_HEADS // 2):
            blk_rows = odt[LANES * pair:LANES * (pair + 1), :]
            even = blk_rows[:, SUB * (2 * pair):SUB * (2 * pair + 1)].T
            odd = blk_rows[:, SUB * (2 * pair + 1):SUB * (2 * pair + 2)].T
            halves.append(jnp.where(low, even, odd))
        od = jnp.concatenate(halves, axis=1)
        mix_scr[rows, 3 * GROUP_W:4 * GROUP_W] = (od * _silu(p_scr[rows, C_MZ:C_MZ + GROUP_W])).astype(bf16)

    for j in range(nsub):
        sub_tile(j)
    while pending:
        project_next()

    mo = _dot(mix_scr[...], wout_ref[...])
    y_ref[0] = x_prev[...] + _rmsnorm(mo, gpost_ref[...])
    x_prev[...] = x

    @pl.when((t == nt - 1) & (s > 0))
    def _sequence_done():
        conv_out[0, 0:1, :] = cbuf[1:2, :]
        conv_out[0, 1:2, :] = cbuf[0:1, :]
        for src, dst in ((kprev, swak_out), (vprev, swav_out)):
            for r in range(GROUPS):
                kv_nat[pl.ds(r, SUBLANES, stride=GLA_BLOCK), :] = src[SUBLANES * r:SUBLANES * (r + 1), :]
            dst[0] = kv_nat[...].T
        st = st_scr[...]
        for hh in range(N_HEADS):
            gla_out[0, hh] = st[HEAD_D * hh:HEAD_D * (hh + 1), HEAD_D * hh:HEAD_D * (hh + 1)].T


def _layer_spec(shape, l):
    nd = len(shape)
    return pl.BlockSpec((None,) + tuple(shape), lambda s, _l=l, _nd=nd: (_l,) + (0,) * _nd)


def _const_spec(shape):
    nd = len(shape)
    return pl.BlockSpec(shape, lambda s, _nd=nd: (0,) * _nd)


def _prompt_layer(l, x, memp, cos, sin, gpre, gpost, wt, wout, wmem, convw, wg, bg, gnorm, sinks, jmat,
                  lvl, swab):
    B, L, _ = x.shape
    assert L % TL == 0
    nt = L // TL
    n_tiles = B * nt

    def projected(s):
        tile = jnp.minimum(s, n_tiles - 1)
        return tile // nt, tile % nt

    def mixed(s):
        tile = jnp.maximum(s - 1, 0)
        return tile // nt, tile % nt

    def per_sequence(shape):
        nd = len(shape) - 1
        return pl.BlockSpec(shape, lambda s, _nd=nd: (mixed(s)[0],) + (0,) * _nd)

    out_shape = (
        jax.ShapeDtypeStruct((B, L, D_MODEL), f32),
        jax.ShapeDtypeStruct((B, CONV_W - 1, GROUP_W), f32),
        jax.ShapeDtypeStruct((B, N_HEADS, HEAD_D, HEAD_D), f32),
        jax.ShapeDtypeStruct((B, SWA_KV_W, WINDOW), f32),
        jax.ShapeDtypeStruct((B, SWA_KV_W, WINDOW), f32),
        jax.ShapeDtypeStruct((B, GROUP_W, N_MEM), f32),
        jax.ShapeDtypeStruct((B, GROUP_W, N_MEM), f32),
    )
    in_specs = [
        pl.BlockSpec(memory_space=pltpu.SMEM),
        pl.BlockSpec((1, TL, D_MODEL), lambda s: projected(s) + (0,)),
        per_sequence((1, N_MEM, D_MODEL)),
        pl.BlockSpec((TL, LANES), lambda s: (mixed(s)[1], 0)),
        pl.BlockSpec((TL, LANES), lambda s: (mixed(s)[1], 0)),
        _layer_spec((1, D_MODEL), l), _layer_spec((1, D_MODEL), l),
        _layer_spec((IN_WIDTH, D_MODEL), l),
        _layer_spec((D_MODEL, D_MODEL), l), _layer_spec((D_MODEL, 2 * GROUP_W), l),
        _layer_spec((CONV_W, GROUP_W), l), _layer_spec((GATE_RANK, GROUP_W), l), _layer_spec((1, GROUP_W), l),
        _layer_spec((1, GROUP_W), l), _const_spec((GROUP_W, GROUP_W)),
        _const_spec((len(LEVEL_GROUPS), N_HEADS * SUB, SUB)), _const_spec((2, 2 * SUB, N_HEADS * SUB)),
    ]
    out_specs = (
        pl.BlockSpec((1, TL, D_MODEL), lambda s: mixed(s) + (0,)),
        per_sequence((1, CONV_W - 1, GROUP_W)),
        per_sequence((1, N_HEADS, HEAD_D, HEAD_D)),
        per_sequence((1, SWA_KV_W, WINDOW)),
        per_sequence((1, SWA_KV_W, WINDOW)),
        per_sequence((1, GROUP_W, N_MEM)),
        per_sequence((1, GROUP_W, N_MEM)),
    )
    scratch = [
        pltpu.VMEM((2, TL, NP), f32),
        pltpu.VMEM((TL, D_MODEL), f32),
        pltpu.VMEM((TL, D_MODEL), bf16),
        pltpu.VMEM((SUBLANES, GROUP_W), f32),
        pltpu.VMEM((GROUP_W, GROUP_W), f32),
        pltpu.VMEM((SUB, SWA_KV_W), f32),
        pltpu.VMEM((SUB, SWA_KV_W), f32),
        pltpu.VMEM((SWA_KV_W, SUB), f32),
        pltpu.VMEM((N_MEM, GROUP_W), bf16),
        pltpu.VMEM((GROUP_W, N_MEM), bf16),
        pltpu.VMEM((WINDOW, SWA_KV_W), f32),
    ]
    return pl.pallas_call(
        functools.partial(_prompt_layer_kernel, nt),
        grid=(n_tiles + 1,),
        in_specs=in_specs,
        out_specs=out_specs,
        out_shape=out_shape,
        scratch_shapes=scratch,
        compiler_params=pltpu.CompilerParams(dimension_semantics=("arbitrary",),
                                             vmem_limit_bytes=VMEM_LIMIT_BYTES),
        name="prompt_layer",
    )(sinks[l], x, memp, cos, sin, gpre, gpost, wt, wout, wmem, convw, wg, bg, gnorm, jmat, lvl, swab)


def _subtile_constants():
    row = np.arange(SUB)
    tok = (row % SUBLANES) * GLA_BLOCK + row // SUBLANES
    blk = row % SUBLANES
    levels = []
    for group in LEVEL_GROUPS:
        g = blk // group
        sel = (g[:, None] == g[None, :] + 1) & (g[:, None] % 2 == 1)
        levels.append(np.tile(sel, (N_HEADS, 1)))
    key_tok = np.concatenate([tok - SUB, tok])
    valid = (key_tok[None, :] <= tok[:, None]) & (key_tok[None, :] > tok[:, None] - WINDOW)
    bias = [np.where(valid, 0.0, NEG), np.where(valid & (key_tok[None, :] >= 0), 0.0, NEG)]
    bias = np.stack([np.tile(b, (N_HEADS, 1)).T for b in bias])
    return jnp.asarray(np.stack(levels), dtype=f32), jnp.asarray(bias, dtype=f32)


def _to_kernel_order(x, axis):
    shp = x.shape
    n = shp[axis]
    x = x.reshape(shp[:axis] + (n // SUB, SUBLANES, GLA_BLOCK) + shp[axis + 1:])
    return jnp.swapaxes(x, axis + 1, axis + 2).reshape(shp)


def _from_kernel_order(x, axis):
    shp = x.shape
    n = shp[axis]
    x = x.reshape(shp[:axis] + (n // SUB, GLA_BLOCK, SUBLANES) + shp[axis + 1:])
    return jnp.swapaxes(x, axis + 1, axis + 2).reshape(shp)


def _sample_kernel(seq_len, group, sinks_ref, x_hbm, cos_ref, sin_ref, convb_ref, sgla_ref, kc_ref, vc_ref,
                   mkc_ref, mvc_ref, gpre_ref, gpost_ref, wt_ref, wout_ref, convw_ref, wg_ref, bg_ref,
                   gnorm_ref, j_ref, place_ref,
                   y_hbm, conv_out, gla_out, swak_out, swav_out,
                   hs, p_scr, mix_scr, sem):
    l = pl.program_id(0)
    g_step = pl.program_id(1)
    n_layers = pl.num_programs(0)
    n_steps = pl.num_programs(1)
    n_tok = hs.shape[0]
    n = group * seq_len
    chunks = [slice(c * TL, (c + 1) * TL) for c in range(n_tok // TL)]

    @pl.when((l == 0) & (g_step == 0))
    def _load():
        cp = pltpu.make_async_copy(x_hbm, hs, sem.at[0])
        cp.start()
        cp.wait()

    @pl.when(g_step == 0)
    def _project_all():
        for rows in chunks:
            h = _rmsnorm(hs[rows, :], gpre_ref[...]).astype(bf16)
            _project(h, wt_ref, p_scr, rows)

    rows = pl.ds(pl.multiple_of(g_step * n, n), n)
    row_t = lax.broadcasted_iota(jnp.int32, (n, GROUP_W), 0) % seq_len

    u = p_scr[rows, C_CC:C_CC + GROUP_W] * p_scr[rows, C_CX:C_CX + GROUP_W]
    hist = convb_ref[...]
    um1 = jnp.where(row_t >= 1, pltpu.roll(u, 1, 0), pltpu.roll(hist, n - 1, 0))
    um2 = jnp.where(row_t >= 2, pltpu.roll(u, 2, 0), hist)
    cy = convw_ref[0:1, :] * um2 + convw_ref[1:2, :] * um1 + convw_ref[2:3, :] * u
    a_out = p_scr[rows, C_CB:C_CB + GROUP_W] * cy * _silu(p_scr[rows, C_CZ:C_CZ + GROUP_W])
    mix_scr[rows, 0:GROUP_W] = a_out.astype(bf16)
    for g in range(group):
        conv_out[g] = u[g * seq_len + seq_len - (CONV_W - 1):(g + 1) * seq_len, :]

    qs = p_scr[rows, C_GQ:C_GQ + GROUP_W] * QK_SCALE
    k = p_scr[rows, C_GK:C_GK + GROUP_W]
    v = p_scr[rows, C_GV:C_GV + GROUP_W]
    la = _gate_log_decay(p_scr[rows, C_LR:C_LR + GATE_RANK], wg_ref, bg_ref)
    bc = _block_cumsum(la, seq_len)
    o = _dot((qs * k).astype(bf16), j_ref[...]) * v
    for d in range(1, seq_len):
        dec = jnp.exp(jnp.where(row_t >= d, bc - pltpu.roll(bc, d, 0), NEG))
        pw = qs * pltpu.roll(k, d, 0) * dec
        o = o + _dot(pw.astype(bf16), j_ref[...]) * pltpu.roll(v, d, 0)
    qd = qs * jnp.exp(bc)
    tot = [bc[(g + 1) * seq_len - 1:(g + 1) * seq_len, :] for g in range(group)]
    kd = k * jnp.exp(jnp.concatenate(
        [tot[g] - bc[g * seq_len:(g + 1) * seq_len, :] for g in range(group)], axis=0))
    o_inter = []
    for g in range(group):
        rs = slice(g * seq_len, (g + 1) * seq_len)
        alpha = jnp.exp(tot[g])
        per_head = []
        for hh in range(N_HEADS):
            ls = slice(hh * HEAD_D, (hh + 1) * HEAD_D)
            st0 = sgla_ref[g, hh]
            per_head.append(_dot_nt(qd[rs, ls].astype(bf16), st0.astype(bf16)))
            gla_out[g, hh] = st0 * alpha[:, ls] + _dot_tn(v[rs, ls].astype(bf16), kd[rs, ls].astype(bf16))
        o_inter.append(jnp.concatenate(per_head, axis=1))
    o = o + jnp.concatenate(o_inter, axis=0)
    b_out = _gla_norm_gate(o, p_scr[rows, C_GZ:C_GZ + GROUP_W], gnorm_ref, j_ref)
    mix_scr[rows, GROUP_W:2 * GROUP_W] = b_out.astype(bf16)

    cos = cos_ref[...]
    sin = sin_ref[...]
    q01 = _rope(p_scr[rows, C_SQ:C_SQ + LANES], cos, sin) * QK_SCALE
    q23 = _rope(p_scr[rows, C_SQ + LANES:C_SQ + 2 * LANES], cos, sin) * QK_SCALE
    kr = _rope(p_scr[rows, C_SK:C_SK + SWA_KV_W], cos, sin)
    vx = p_scr[rows, C_SV:C_SV + SWA_KV_W]
    qm = p_scr[rows, C_MQ:C_MQ + GROUP_W] * QK_SCALE
    hq = N_HEADS * seq_len
    sink_seq = _sink_column([sinks_ref[l, hh] for hh in range(N_HEADS)], seq_len)
    sink = jnp.concatenate([sink_seq] * group, axis=0)
    hq_all = group * hq
    qrow = lax.broadcasted_iota(jnp.int32, (hq_all, WINDOW), 0) % seq_len
    ccol = lax.broadcasted_iota(jnp.int32, (hq_all, WINDOW), 1)
    cache_valid = ccol > qrow
    nrow = lax.broadcasted_iota(jnp.int32, (hq_all, seq_len), 0) % seq_len
    ncol = lax.broadcasted_iota(jnp.int32, (hq_all, seq_len), 1)
    new_valid = ncol <= nrow
    keep_old = lax.broadcasted_iota(jnp.int32, (SWA_KV_W, WINDOW), 1) < WINDOW - seq_len

    def exact_split(a):
        hi = a.astype(bf16)
        r1 = a - hi.astype(f32)
        mid = r1.astype(bf16)
        lo = (r1 - mid.astype(f32)).astype(bf16)
        return jnp.concatenate([hi, mid, lo], axis=0)

    def shifted_cache(old_t, new_rows):
        placed = _dot_tn(exact_split(new_rows), place_ref[...])
        return jnp.where(keep_old, pltpu.roll(old_t, WINDOW - seq_len, 1), placed)

    seqs = [slice(g * seq_len, (g + 1) * seq_len) for g in range(group)]
    stk = [slice(g * hq, (g + 1) * hq) for g in range(group)]
    qst = [_stack_swa_q(q01[rs], q23[rs]).astype(bf16) for rs in seqs]
    s_c = jnp.concatenate([_dot(qst[g], kc_ref[g].astype(bf16)) for g in range(group)], axis=0)
    s_n = jnp.concatenate([_dot_nt(qst[g], kr[seqs[g]].astype(bf16)) for g in range(group)], axis=0)
    s_c = jnp.where(cache_valid, s_c, NEG)
    s_n = jnp.where(new_valid, s_n, NEG)
    mx = jnp.maximum(jnp.maximum(jnp.max(s_c, axis=1, keepdims=True), jnp.max(s_n, axis=1, keepdims=True)), sink)
    e_c = jnp.exp(s_c - mx)
    e_n = jnp.exp(s_n - mx)
    den = jnp.sum(e_c, axis=1, keepdims=True) + jnp.sum(e_n, axis=1, keepdims=True) + jnp.exp(sink - mx)
    e_c = e_c.astype(bf16)
    e_n = e_n.astype(bf16)
    ov = jnp.concatenate([_dot_nt(e_c[stk[g]], vc_ref[g].astype(bf16)) + _dot(e_n[stk[g]], vx[seqs[g]].astype(bf16))
                          for g in range(group)], axis=0) * (1.0 / den)
    oc = jnp.concatenate([_unstack_swa(ov[stk[g]], seq_len) for g in range(group)], axis=0)
    for g in range(group):
        swak_out[g] = shifted_cache(kc_ref[g], kr[seqs[g]])
        swav_out[g] = shifted_cache(vc_ref[g], vx[seqs[g]])

    sm = jnp.concatenate([_dot(_stack_heads(qm[seqs[g]]).astype(bf16), mkc_ref[g].astype(bf16))
                          for g in range(group)], axis=0)
    mm = jnp.max(sm, axis=1, keepdims=True)
    em = jnp.exp(sm - mm)
    dm = jnp.sum(em, axis=1, keepdims=True)
    em = em.astype(bf16)
    odv = jnp.concatenate([_dot_nt(em[stk[g]], mvc_ref[g].astype(bf16)) for g in range(group)], axis=0) * (1.0 / dm)
    od = jnp.concatenate([_unstack_heads(odv[stk[g]], seq_len) for g in range(group)], axis=0)
    mix_scr[rows, 2 * GROUP_W:3 * GROUP_W] = (oc * _silu(p_scr[rows, C_SZ:C_SZ + GROUP_W])).astype(bf16)
    mix_scr[rows, 3 * GROUP_W:4 * GROUP_W] = (od * _silu(p_scr[rows, C_MZ:C_MZ + GROUP_W])).astype(bf16)

    @pl.when(g_step == n_steps - 1)
    def _residual():
        for rws in chunks:
            mo = _dot(mix_scr[rws, :], wout_ref[...])
            hs[rws, :] = hs[rws, :] + _rmsnorm(mo, gpost_ref[...])

    @pl.when((l == n_layers - 1) & (g_step == n_steps - 1))
    def _store():
        cp = pltpu.make_async_copy(hs, y_hbm, sem.at[0])
        cp.start()
        cp.wait()


def _sample_layers(x, cos, sin, convb, sgla, kc, vc, mkc, mvc, gpre, gpost, wt, wout, convw, wg, bg, gnorm,
                   sinks, jmat, place, seq_len, group):
    depth, nseq = sgla.shape[0], sgla.shape[1]
    n_tok = nseq * seq_len
    assert nseq % group == 0 and seq_len == SUBLANES and n_tok % TL == 0
    n = group * seq_len

    def per_layer(shape, single_buffer=False):
        nd = len(shape)
        kw = dict(pipeline_mode=pl.Buffered(1)) if single_buffer else {}
        return pl.BlockSpec((None,) + tuple(shape), lambda l, g, _nd=nd: (l,) + (0,) * _nd, **kw)

    def per_group(shape):
        nd = len(shape) - 1
        return pl.BlockSpec((None,) + tuple(shape), lambda l, g, _nd=nd: (l, g) + (0,) * _nd)

    def const(shape):
        nd = len(shape)
        return pl.BlockSpec(shape, lambda l, g, _nd=nd: (0,) * _nd)

    in_specs = [
        pl.BlockSpec(memory_space=pltpu.SMEM),
        pl.BlockSpec(memory_space=pl.ANY),
        const((n, LANES)), const((n, LANES)),
        per_group((n, GROUP_W)),
        per_group((group, N_HEADS, HEAD_D, HEAD_D)),
        per_group((group, SWA_KV_W, WINDOW)), per_group((group, SWA_KV_W, WINDOW)),
        per_group((group, GROUP_W, N_MEM)), per_group((group, GROUP_W, N_MEM)),
        per_layer((1, D_MODEL)), per_layer((1, D_MODEL)),
        per_layer((IN_WIDTH, D_MODEL), True),
        per_layer((D_MODEL, D_MODEL), True),
        per_layer((CONV_W, GROUP_W)), per_layer((GATE_RANK, GROUP_W)), per_layer((1, GROUP_W)), per_layer((1, GROUP_W)),
        const((GROUP_W, GROUP_W)), const((3 * seq_len, WINDOW)),
    ]
    out_shape = (
        jax.ShapeDtypeStruct((n_tok, D_MODEL), f32),
        jax.ShapeDtypeStruct((depth, nseq, CONV_W - 1, GROUP_W), f32),
        jax.ShapeDtypeStruct((depth, nseq, N_HEADS, HEAD_D, HEAD_D), f32),
        jax.ShapeDtypeStruct((depth, nseq, SWA_KV_W, WINDOW), f32),
        jax.ShapeDtypeStruct((depth, nseq, SWA_KV_W, WINDOW), f32),
    )
    out_specs = (
        pl.BlockSpec(memory_space=pl.ANY),
        per_group((group, CONV_W - 1, GROUP_W)),
        per_group((group, N_HEADS, HEAD_D, HEAD_D)),
        per_group((group, SWA_KV_W, WINDOW)), per_group((group, SWA_KV_W, WINDOW)),
    )
    scratch = [
        pltpu.VMEM((n_tok, D_MODEL), f32),
        pltpu.VMEM((n_tok, NP), f32),
        pltpu.VMEM((n_tok, D_MODEL), bf16),
        pltpu.SemaphoreType.DMA((1,)),
    ]
    return pl.pallas_call(
        functools.partial(_sample_kernel, seq_len, group),
        grid=(depth, nseq // group),
        in_specs=in_specs,
        out_specs=out_specs,
        out_shape=out_shape,
        scratch_shapes=scratch,
        compiler_params=pltpu.CompilerParams(dimension_semantics=("arbitrary", "arbitrary"),
                                             vmem_limit_bytes=VMEM_LIMIT_BYTES),
        name="sample_layers",
    )(sinks, x, cos, sin, convb, sgla, kc, vc, mkc, mvc, gpre, gpost, wt, wout, convw, wg, bg, gnorm,
      jmat, place)


def _rope_tables(pos):
    half = HEAD_D // 2
    inv = jnp.power(ROPE_THETA, -jnp.arange(half, dtype=f32) / half)
    ang = pos.astype(f32)[:, None] * inv[None, :]
    cos = jnp.tile(jnp.cos(ang), (1, LANES // half))
    sin = jnp.sin(ang)
    sin_signed = jnp.tile(jnp.concatenate([-sin, sin], axis=1), (1, LANES // HEAD_D))
    return cos, sin_signed


def _feature_major(cache):
    d, s, p, h, e = cache.shape
    return jnp.transpose(cache, (0, 1, 3, 4, 2)).reshape(d, s, h * e, p)


def _position_major(cache_t, heads):
    d, s, he, p = cache_t.shape
    return jnp.transpose(cache_t.reshape(d, s, heads, he // heads, p), (0, 1, 4, 2, 3))


def kernel(x_prompt, x_sample, state_conv, state_gla, cache_swa_k, cache_swa_v, cache_mem_k, cache_mem_v,
           mem_prompt, norm_pre, norm_post, w_in, conv_w, gla_w_gate, gla_b_gate, gla_norm, swa_sinks,
           w_mem_kv, w_out):
    depth = w_in.shape[0]
    B, L, _ = x_prompt.shape
    nseq, seq_len, _ = x_sample.shape
    group = 8

    wt = jnp.swapaxes(w_in, 1, 2).astype(bf16)
    wout = w_out.astype(bf16)
    wmem = w_mem_kv.astype(bf16)
    wg = gla_w_gate.astype(bf16)
    head_id = np.arange(GROUP_W) // HEAD_D
    jmat = jnp.asarray(head_id[:, None] == head_id[None, :], dtype=bf16)
    lvl, swab = _subtile_constants()
    place = np.zeros((3 * seq_len, WINDOW), np.float32)
    for piece in range(3):
        place[piece * seq_len + np.arange(seq_len), WINDOW - seq_len + np.arange(seq_len)] = 1.0
    place = jnp.asarray(place, dtype=bf16)
    sinks = swa_sinks.astype(f32)
    gpre = norm_pre[:, None, :]
    gpost = norm_post[:, None, :]
    bg = gla_b_gate[:, None, :]
    gn = gla_norm[:, None, :]

    cos_p, sin_p = _rope_tables(jnp.arange(L, dtype=jnp.int32))
    cos_p = _to_kernel_order(cos_p, 0)
    sin_p = _to_kernel_order(sin_p, 0)
    cos_s, sin_s = _rope_tables(PAST_LEN + jnp.arange(seq_len, dtype=jnp.int32))
    cos_s = jnp.tile(cos_s, (group, 1))
    sin_s = jnp.tile(sin_s, (group, 1))

    convb = jnp.pad(state_conv, ((0, 0), (0, 0), (0, seq_len - (CONV_W - 1)), (0, 0)))
    convb = convb.reshape(depth, nseq * seq_len, GROUP_W)
    ys, conv_s, gla_s, swak_s, swav_s = _sample_layers(
        x_sample.reshape(nseq * seq_len, D_MODEL), cos_s, sin_s, convb, jnp.swapaxes(state_gla, -1, -2),
        _feature_major(cache_swa_k), _feature_major(cache_swa_v),
        _feature_major(cache_mem_k), _feature_major(cache_mem_v),
        gpre, gpost, wt, wout, conv_w, wg, bg, gn, sinks, jmat, place, seq_len, group)

    hp = _to_kernel_order(x_prompt, 1)
    outs_p = [[] for _ in range(6)]
    for l in range(depth):
        res = _prompt_layer(l, hp, mem_prompt, cos_p, sin_p, gpre, gpost, wt, wout, wmem, conv_w, wg, bg, gn,
                            sinks, jmat, lvl, swab)
        hp = res[0]
        for i in range(6):
            outs_p[i].append(res[i + 1])

    return (_from_kernel_order(hp, 1), ys.reshape(nseq, seq_len, D_MODEL),
            jnp.stack(outs_p[0]), jnp.stack(outs_p[1]),
            _position_major(jnp.stack(outs_p[2]), 2), _position_major(jnp.stack(outs_p[3]), 2),
            _position_major(jnp.stack(outs_p[4]), N_HEADS), _position_major(jnp.stack(outs_p[5]), N_HEADS),
            conv_s, jnp.swapaxes(gla_s, -1, -2), _position_major(swak_s, 2), _position_major(swav_s, 2))
```

```python
import functools

import jax
import jax.numpy as jnp
import numpy as np
from jax import lax
from jax.experimental import pallas as pl
from jax.experimental.pallas import tpu as pltpu

f32 = jnp.float32
bf16 = jnp.bfloat16

D_MODEL = 1024
GROUP_W = 256
HEAD_D = 64
N_HEADS = 4
SWA_KV_W = 128
N_MEM = 256
WINDOW = 128
CONV_W = 3
GATE_RANK = 16
GATE_NORM = 16.0
GLA_BLOCK = 16
ROPE_THETA = 10000.0
PAST_LEN = 8192
EPS = 1e-6
NEG = -1e30
QK_SCALE = HEAD_D ** -0.5

LANES = 128
SUBLANES = 8
VMEM_LIMIT_BYTES = 56 * 1024 * 1024

C_CX, C_CB, C_CC, C_CZ = 0, 256, 512, 768
C_GQ, C_GK, C_GV, C_GZ = 1024, 1280, 1536, 1792
C_SQ, C_SK, C_SV, C_SZ = 2048, 2304, 2432, 2560
C_MQ, C_MZ = 2816, 3072
C_LR = 3328
NP = 3456
IN_WIDTH = 3344
_O_GLR, _O_GZ = 1792, 1808

SUB = 128
TL = 512


def _dot(a, b):
    return jnp.dot(a, b, preferred_element_type=f32)


def _dot_nt(a, b):
    return lax.dot_general(a, b, (((1,), (1,)), ((), ())), preferred_element_type=f32)


def _dot_tn(a, b):
    return lax.dot_general(a, b, (((0,), (0,)), ((), ())), preferred_element_type=f32)


def _rmsnorm(x, g):
    return x * lax.rsqrt(jnp.mean(x * x, axis=-1, keepdims=True) + EPS) * g


def _silu(z):
    return z * (0.5 + 0.5 * jnp.tanh(0.5 * z))


def _log_sigmoid(x):
    return jnp.minimum(x, 0.0) - jnp.log1p(jnp.exp(-jnp.abs(x)))


PROJ_PIECES = tuple([(c, c + 256, c) for c in range(0, _O_GLR, 256)]
                    + [(_O_GZ + c, _O_GZ + c + 256, _O_GLR + c) for c in range(0, IN_WIDTH - _O_GZ, 256)]
                    + [(_O_GLR, _O_GZ, C_LR)])


def _project_piece(h, wt_ref, p_ref, rows, piece):
    w0, w1, c0 = PROJ_PIECES[piece]
    p_ref[rows, c0:c0 + (w1 - w0)] = _dot_nt(h, wt_ref[w0:w1, :])


def _project(h, wt_ref, p_ref, rows):
    for piece in range(len(PROJ_PIECES)):
        _project_piece(h, wt_ref, p_ref, rows, piece)


def _head_sum(x, j_ref):
    hi = x.astype(bf16)
    lo = (x - hi.astype(f32)).astype(bf16)
    return _dot(hi, j_ref[...]) + _dot(lo, j_ref[...])


def _rope(x, cos, sin_signed):
    lane = lax.broadcasted_iota(jnp.int32, x.shape, 1)
    swapped = jnp.where((lane % HEAD_D) < HEAD_D // 2,
                        pltpu.roll(x, LANES - HEAD_D // 2, 1), pltpu.roll(x, HEAD_D // 2, 1))
    return x * cos + swapped * sin_signed


def _stack_heads(x):
    lane_head = lax.broadcasted_iota(jnp.int32, x.shape, 1) // HEAD_D
    return jnp.concatenate([jnp.where(lane_head == h, x, 0.0) for h in range(N_HEADS)], axis=0)


def _unstack_heads(o, n):
    lane_head = lax.broadcasted_iota(jnp.int32, (n, GROUP_W), 1) // HEAD_D
    out = o[0:n]
    for h in range(1, N_HEADS):
        out = jnp.where(lane_head == h, o[h * n:(h + 1) * n], out)
    return out


def _stack_swa_q(q01, q23):
    low = lax.broadcasted_iota(jnp.int32, q01.shape, 1) < HEAD_D
    return jnp.concatenate([jnp.where(low, q01, 0.0), jnp.where(low, pltpu.roll(q01, HEAD_D, 1), 0.0),
                            jnp.where(low, 0.0, pltpu.roll(q23, HEAD_D, 1)), jnp.where(low, 0.0, q23)], axis=0)


def _unstack_swa(o, n):
    low = lax.broadcasted_iota(jnp.int32, (n, SWA_KV_W), 1) < HEAD_D
    c01 = jnp.where(low, o[0:n], pltpu.roll(o[n:2 * n], HEAD_D, 1))
    c23 = jnp.where(low, pltpu.roll(o[2 * n:3 * n], HEAD_D, 1), o[3 * n:4 * n])
    return jnp.concatenate([c01, c23], axis=1)


def _sink_column(sinks, n):
    return jnp.concatenate([jnp.full((n, 1), s, f32) for s in sinks], axis=0)


def _block_cumsum(la, block):
    row = lax.broadcasted_iota(jnp.int32, la.shape, 0) % block
    b = la
    s = 1
    while s < block:
        b = b + jnp.where(row >= s, pltpu.roll(b, s, 0), 0.0)
        s *= 2
    return b


def _gate_log_decay(p_lr, wg_ref, bg_ref):
    pre = _dot(p_lr.astype(bf16), wg_ref[...]) + bg_ref[...]
    return _log_sigmoid(pre) * (1.0 / GATE_NORM)


def _gla_norm_gate(o, gz, gnorm_ref, j_ref):
    ms = _head_sum(o * o, j_ref) * (1.0 / HEAD_D)
    return o * lax.rsqrt(ms + EPS) * gnorm_ref[...] * _silu(gz)


GROUPS = SUB // SUBLANES
LEVEL_GROUPS = (1, 2, 4)


def _row_groups(x):
    return [x[SUBLANES * r:SUBLANES * (r + 1), :] for r in range(x.shape[0] // SUBLANES)]


def _prompt_layer_kernel(nt, *refs):
    s = pl.program_id(0)
    pl.when(s == 0)(functools.partial(_prompt_first_step, *refs))
    for parity in range(2):
        pl.when((s > 0) & (s % 2 == parity))(functools.partial(_prompt_layer_step, nt, parity, *refs))


def _prompt_first_step(sinks_ref, x_ref, memp_ref, cos_ref, sin_ref, gpre_ref, gpost_ref, wt_ref,
                       wout_ref, wmem_ref, convw_ref, wg_ref, bg_ref, gnorm_ref, j_ref, lvl_ref, swab_ref,
                       y_ref, conv_out, gla_out, swak_out, swav_out, mk_out, mv_out,
                       p_buf, x_prev, *unused):
    x = x_ref[0]
    _project(_rmsnorm(x, gpre_ref[...]).astype(bf16), wt_ref, p_buf.at[0], slice(None))
    x_prev[...] = x


def _prompt_layer_step(nt, parity, sinks_ref, x_ref, memp_ref, cos_ref, sin_ref, gpre_ref, gpost_ref, wt_ref,
                       wout_ref, wmem_ref, convw_ref, wg_ref, bg_ref, gnorm_ref, j_ref, lvl_ref, swab_ref,
                       y_ref, conv_out, gla_out, swak_out, swav_out, mk_out, mv_out,
                       p_buf, x_prev, mix_scr, cbuf, st_scr, kprev, vprev, vtprev, mkb, mvtb, kv_nat):
    s = pl.program_id(0)
    t = jnp.maximum(s - 1, 0) % nt
    nsub = TL // SUB
    p_wr = p_buf.at[parity]
    p_scr = p_buf.at[1 - parity]

    @pl.when(t == 0)
    def _new_sequence():
        cbuf[...] = jnp.zeros((SUBLANES, GROUP_W), f32)
        st_scr[...] = jnp.zeros((GROUP_W, GROUP_W), f32)
        kprev[...] = jnp.zeros((SUB, SWA_KV_W), f32)
        vprev[...] = jnp.zeros((SUB, SWA_KV_W), f32)
        vtprev[...] = jnp.zeros((SWA_KV_W, SUB), f32)
        mkv = _dot(memp_ref[0].astype(bf16), wmem_ref[...])
        mvt = mkv[:, GROUP_W:2 * GROUP_W].T
        mk_out[0] = mkv[:, 0:GROUP_W].T
        mv_out[0] = mvt
        mkb[...] = mkv[:, 0:GROUP_W].astype(bf16)
        mvtb[...] = mvt.astype(bf16)

    x = x_ref[0]
    h = _rmsnorm(x, gpre_ref[...]).astype(bf16)

    sub8 = lax.broadcasted_iota(jnp.int32, (SUBLANES, GROUP_W), 0)
    pending = list(range(len(PROJ_PIECES)))

    def project_next():
        if pending:
            _project_piece(h, wt_ref, p_wr, slice(None), pending.pop(0))

    def sub_tile(j):
        rows = slice(j * SUB, (j + 1) * SUB)
        project_next()

        u = p_scr[rows, C_CC:C_CC + GROUP_W] * p_scr[rows, C_CX:C_CX + GROUP_W]
        last = SUB - SUBLANES
        prev1 = jnp.where(sub8 == 0, cbuf[0:1, :], pltpu.roll(u[last:SUB, :], 1, 0))
        prev2 = jnp.where(sub8 == 0, cbuf[1:2, :], pltpu.roll(u[last - SUBLANES:last, :], 1, 0))
        um1 = jnp.concatenate([prev1, u[0:last, :]], axis=0)
        um2 = jnp.concatenate([prev2, prev1, u[0:last - SUBLANES, :]], axis=0)
        cy = convw_ref[0:1, :] * um2 + convw_ref[1:2, :] * um1 + convw_ref[2:3, :] * u
        cbuf[0:1, :] = u[SUB - 1:SUB, :]
        cbuf[1:2, :] = u[last - 1:last, :]
        a_out = p_scr[rows, C_CB:C_CB + GROUP_W] * cy * _silu(p_scr[rows, C_CZ:C_CZ + GROUP_W])
        mix_scr[rows, 0:GROUP_W] = a_out.astype(bf16)

        qg = _row_groups(p_scr[rows, C_GQ:C_GQ + GROUP_W] * QK_SCALE)
        k = p_scr[rows, C_GK:C_GK + GROUP_W]
        v = p_scr[rows, C_GV:C_GV + GROUP_W]
        kg = _row_groups(k)
        vg = _row_groups(v)
        lag = _row_groups(_gate_log_decay(p_scr[rows, C_LR:C_LR + GATE_RANK], wg_ref, bg_ref))
        bg_ = [lag[0]]
        for r in range(1, GROUPS):
            bg_.append(bg_[-1] + lag[r])
        tot = bg_[GROUPS - 1]
        pw = []
        for r in range(GROUPS):
            for s in range(r):
                pw.append(qg[r] * kg[s] * jnp.exp(bg_[r] - bg_[s]))
            pw.append(qg[r] * kg[r])
        scores = _dot(jnp.concatenate(pw, axis=0).astype(bf16), j_ref[...])
        project_next()
        og = []
        idx = 0
        for r in range(GROUPS):
            acc = None
            for s in range(r + 1):
                term = scores[SUBLANES * idx:SUBLANES * (idx + 1), :] * vg[s]
                acc = term if acc is None else acc + term
                idx += 1
            og.append(acc)
        o = jnp.concatenate(og, axis=0)

        def decayed(group):
            before = jnp.zeros((SUBLANES, GROUP_W), f32)
            after = jnp.zeros((SUBLANES, GROUP_W), f32)
            for s in range(1, group):
                before = before + jnp.where(sub8 % group >= s, pltpu.roll(tot, s, 0), 0.0)
                after = after + jnp.where(sub8 % group < group - s, pltpu.roll(tot, SUBLANES - s, 0), 0.0)
            qd = jnp.concatenate([qg[r] * jnp.exp(bg_[r] + before) for r in range(GROUPS)], axis=0)
            kd = jnp.concatenate([kg[r] * jnp.exp((tot - bg_[r]) + after) for r in range(GROUPS)], axis=0)
            return qd, kd

        attn = None
        for li, group in enumerate(LEVEL_GROUPS):
            qd, kd = decayed(group)
            s = _dot_nt(_stack_heads(qd).astype(bf16), kd.astype(bf16)) * lvl_ref[li]
            attn = s if attn is None else attn + s
        o = o + _unstack_heads(_dot(attn.astype(bf16), v.astype(bf16)), SUB)
        project_next()
        qd, kd = decayed(SUB // GLA_BLOCK)
        st = st_scr[...]
        o = o + _dot_nt(qd.astype(bf16), st.astype(bf16))
        total = jnp.sum(tot, axis=0, keepdims=True)
        upd = _dot_tn(v.astype(bf16), kd.astype(bf16))
        same_head = (lax.broadcasted_iota(jnp.int32, (GROUP_W, GROUP_W), 0) // HEAD_D
                     == lax.broadcasted_iota(jnp.int32, (GROUP_W, GROUP_W), 1) // HEAD_D)
        st_scr[...] = st * jnp.exp(total) + jnp.where(same_head, upd, 0.0)
        b_out = _gla_norm_gate(o, p_scr[rows, C_GZ:C_GZ + GROUP_W], gnorm_ref, j_ref)
        mix_scr[rows, GROUP_W:2 * GROUP_W] = b_out.astype(bf16)

        cos = cos_ref[rows, :]
        sin = sin_ref[rows, :]
        q01 = _rope(p_scr[rows, C_SQ:C_SQ + LANES], cos, sin) * QK_SCALE
        q23 = _rope(p_scr[rows, C_SQ + LANES:C_SQ + 2 * LANES], cos, sin) * QK_SCALE
        kr = _rope(p_scr[rows, C_SK:C_SK + SWA_KV_W], cos, sin)
        vx = p_scr[rows, C_SV:C_SV + SWA_KV_W]
        k_all = jnp.concatenate([kprev[...], kr], axis=0).astype(bf16)
        vt_new = vx.T
        vt_all = jnp.concatenate([vtprev[...], vt_new], axis=1).astype(bf16)
        first = jnp.where((t * nsub + j) == 0, 1, 0)
        sc = _dot_nt(k_all, _stack_swa_q(q01, q23).astype(bf16)) + swab_ref[first]
        lane_head = lax.broadcasted_iota(jnp.int32, (1, N_HEADS * SUB), 1) // SUB
        sink = jnp.full((1, N_HEADS * SUB), sinks_ref[0], f32)
        for hh in range(1, N_HEADS):
            sink = jnp.where(lane_head == hh, sinks_ref[hh], sink)
        mx = jnp.maximum(jnp.max(sc, axis=0, keepdims=True), sink)
        e = jnp.exp(sc - mx)
        den = jnp.sum(e, axis=0, keepdims=True) + jnp.exp(sink - mx)
        ot = _dot(vt_all, e.astype(bf16)) * (1.0 / den)
        oc = _unstack_swa(jnp.concatenate([ot[:, SUB * hh:SUB * (hh + 1)].T for hh in range(N_HEADS)], axis=0), SUB)
        mix_scr[rows, 2 * GROUP_W:3 * GROUP_W] = (oc * _silu(p_scr[rows, C_SZ:C_SZ + GROUP_W])).astype(bf16)
        kprev[...] = kr
        vprev[...] = vx
        vtprev[...] = vt_new
        project_next()

        qm = _stack_heads(p_scr[rows, C_MQ:C_MQ + GROUP_W] * QK_SCALE).astype(bf16)
        sm = _dot_nt(mkb[...], qm)
        mm = jnp.max(sm, axis=0, keepdims=True)
        em = jnp.exp(sm - mm)
        dm = jnp.sum(em, axis=0, keepdims=True)
        odt = _dot(mvtb[...], em.astype(bf16)) * (1.0 / dm)
        low = lax.broadcasted_iota(jnp.int32, (SUB, LANES), 1) < HEAD_D
        halves = []
        for pair in range(N_HEADS // 2):
            blk_rows = odt[LANES * pair:LANES * (pair + 1), :]
            even = blk_rows[:, SUB * (2 * pair):SUB * (2 * pair + 1)].T
            odd = blk_rows[:, SUB * (2 * pair + 1):SUB * (2 * pair + 2)].T
            halves.append(jnp.where(low, even, odd))
        od = jnp.concatenate(halves, axis=1)
        mix_scr[rows, 3 * GROUP_W:4 * GROUP_W] = (od * _silu(p_scr[rows, C_MZ:C_MZ + GROUP_W])).astype(bf16)

    for j in range(nsub):
        sub_tile(j)
    while pending:
        project_next()

    mo = _dot(mix_scr[...], wout_ref[...])
    y_ref[0] = x_prev[...] + _rmsnorm(mo, gpost_ref[...])
    x_prev[...] = x

    @pl.when(t == nt - 1)
    def _sequence_done():
        conv_out[0, 0:1, :] = cbuf[1:2, :]
        conv_out[0, 1:2, :] = cbuf[0:1, :]
        for src, dst in ((kprev, swak_out), (vprev, swav_out)):
            for r in range(GROUPS):
                kv_nat[pl.ds(r, SUBLANES, stride=GLA_BLOCK), :] = src[SUBLANES * r:SUBLANES * (r + 1), :]
            dst[0] = kv_nat[...].T
        st = st_scr[...]
        for hh in range(N_HEADS):
            gla_out[0, hh] = st[HEAD_D * hh:HEAD_D * (hh + 1), HEAD_D * hh:HEAD_D * (hh + 1)].T


def _layer_spec(shape, l):
    nd = len(shape)
    return pl.BlockSpec((None,) + tuple(shape), lambda s, _l=l, _nd=nd: (_l,) + (0,) * _nd)


def _const_spec(shape):
    nd = len(shape)
    return pl.BlockSpec(shape, lambda s, _nd=nd: (0,) * _nd)


def _prompt_layer(l, x, memp, cos, sin, gpre, gpost, wt, wout, wmem, convw, wg, bg, gnorm, sinks, jmat,
                  lvl, swab):
    B, L, _ = x.shape
    assert L % TL == 0
    nt = L // TL
    n_tiles = B * nt

    def projected(s):
        tile = jnp.minimum(s, n_tiles - 1)
        return tile // nt, tile % nt

    def mixed(s):
        tile = jnp.maximum(s - 1, 0)
        return tile // nt, tile % nt

    def per_sequence(shape):
        nd = len(shape) - 1
        return pl.BlockSpec(shape, lambda s, _nd=nd: (mixed(s)[0],) + (0,) * _nd)

    out_shape = (
        jax.ShapeDtypeStruct((B, L, D_MODEL), f32),
        jax.ShapeDtypeStruct((B, CONV_W - 1, GROUP_W), f32),
        jax.ShapeDtypeStruct((B, N_HEADS, HEAD_D, HEAD_D), f32),
        jax.ShapeDtypeStruct((B, SWA_KV_W, WINDOW), f32),
        jax.ShapeDtypeStruct((B, SWA_KV_W, WINDOW), f32),
        jax.ShapeDtypeStruct((B, GROUP_W, N_MEM), f32),
        jax.ShapeDtypeStruct((B, GROUP_W, N_MEM), f32),
    )
    in_specs = [
        pl.BlockSpec(memory_space=pltpu.SMEM),
        pl.BlockSpec((1, TL, D_MODEL), lambda s: projected(s) + (0,)),
        per_sequence((1, N_MEM, D_MODEL)),
        pl.BlockSpec((TL, LANES), lambda s: (mixed(s)[1], 0)),
        pl.BlockSpec((TL, LANES), lambda s: (mixed(s)[1], 0)),
        _layer_spec((1, D_MODEL), l), _layer_spec((1, D_MODEL), l),
        _layer_spec((IN_WIDTH, D_MODEL), l),
        _layer_spec((D_MODEL, D_MODEL), l), _layer_spec((D_MODEL, 2 * GROUP_W), l),
        _layer_spec((CONV_W, GROUP_W), l), _layer_spec((GATE_RANK, GROUP_W), l), _layer_spec((1, GROUP_W), l),
        _layer_spec((1, GROUP_W), l), _const_spec((GROUP_W, GROUP_W)),
        _const_spec((len(LEVEL_GROUPS), N_HEADS * SUB, SUB)), _const_spec((2, 2 * SUB, N_HEADS * SUB)),
    ]
    out_specs = (
        pl.BlockSpec((1, TL, D_MODEL), lambda s: mixed(s) + (0,)),
        per_sequence((1, CONV_W - 1, GROUP_W)),
        per_sequence((1, N_HEADS, HEAD_D, HEAD_D)),
        per_sequence((1, SWA_KV_W, WINDOW)),
        per_sequence((1, SWA_KV_W, WINDOW)),
        per_sequence((1, GROUP_W, N_MEM)),
        per_sequence((1, GROUP_W, N_MEM)),
    )
    scratch = [
        pltpu.VMEM((2, TL, NP), f32),
        pltpu.VMEM((TL, D_MODEL), f32),
        pltpu.VMEM((TL, D_MODEL), bf16),
        pltpu.VMEM((SUBLANES, GROUP_W), f32),
        pltpu.VMEM((GROUP_W, GROUP_W), f32),
        pltpu.VMEM((SUB, SWA_KV_W), f32),
        pltpu.VMEM((SUB, SWA_KV_W), f32),
        pltpu.VMEM((SWA_KV_W, SUB), f32),
        pltpu.VMEM((N_MEM, GROUP_W), bf16),
        pltpu.VMEM((GROUP_W, N_MEM), bf16),
        pltpu.VMEM((WINDOW, SWA_KV_W), f32),
    ]
    return pl.pallas_call(
        functools.partial(_prompt_layer_kernel, nt),
        grid=(n_tiles + 1,),
        in_specs=in_specs,
        out_specs=out_specs,
        out_shape=out_shape,
        scratch_shapes=scratch,
        compiler_params=pltpu.CompilerParams(dimension_semantics=("arbitrary",),
                                             vmem_limit_bytes=VMEM_LIMIT_BYTES),
        name="prompt_layer",
    )(sinks[l], x, memp, cos, sin, gpre, gpost, wt, wout, wmem, convw, wg, bg, gnorm, jmat, lvl, swab)


def _subtile_constants():
    row = np.arange(SUB)
    tok = (row % SUBLANES) * GLA_BLOCK + row // SUBLANES
    blk = row % SUBLANES
    levels = []
    for group in LEVEL_GROUPS:
        g = blk // group
        sel = (g[:, None] == g[None, :] + 1) & (g[:, None] % 2 == 1)
        levels.append(np.tile(sel, (N_HEADS, 1)))
    key_tok = np.concatenate([tok - SUB, tok])
    valid = (key_tok[None, :] <= tok[:, None]) & (key_tok[None, :] > tok[:, None] - WINDOW)
    bias = [np.where(valid, 0.0, NEG), np.where(valid & (key_tok[None, :] >= 0), 0.0, NEG)]
    bias = np.stack([np.tile(b, (N_HEADS, 1)).T for b in bias])
    return jnp.asarray(np.stack(levels), dtype=f32), jnp.asarray(bias, dtype=f32)


def _to_kernel_order(x, axis):
    shp = x.shape
    n = shp[axis]
    x = x.reshape(shp[:axis] + (n // SUB, SUBLANES, GLA_BLOCK) + shp[axis + 1:])
    return jnp.swapaxes(x, axis + 1, axis + 2).reshape(shp)


def _from_kernel_order(x, axis):
    shp = x.shape
    n = shp[axis]
    x = x.reshape(shp[:axis] + (n // SUB, GLA_BLOCK, SUBLANES) + shp[axis + 1:])
    return jnp.swapaxes(x, axis + 1, axis + 2).reshape(shp)


def _sample_kernel(seq_len, group, sinks_ref, x_hbm, cos_ref, sin_ref, convb_ref, sgla_ref, kc_ref, vc_ref,
                   mkc_ref, mvc_ref, gpre_ref, gpost_ref, wt_ref, wout_ref, convw_ref, wg_ref, bg_ref,
                   gnorm_ref, j_ref, place_ref,
                   y_hbm, conv_out, gla_out, swak_out, swav_out,
                   hs, p_scr, mix_scr, sem):
    l = pl.program_id(0)
    g_step = pl.program_id(1)
    n_layers = pl.num_programs(0)
    n_steps = pl.num_programs(1)
    n_tok = hs.shape[0]
    n = group * seq_len
    chunks = [slice(c * TL, (c + 1) * TL) for c in range(n_tok // TL)]

    @pl.when((l == 0) & (g_step == 0))
    def _load():
        cp = pltpu.make_async_copy(x_hbm, hs, sem.at[0])
        cp.start()
        cp.wait()

    @pl.when(g_step == 0)
    def _project_all():
        for rows in chunks:
            h = _rmsnorm(hs[rows, :], gpre_ref[...]).astype(bf16)
            _project(h, wt_ref, p_scr, rows)

    rows = pl.ds(pl.multiple_of(g_step * n, n), n)
    row_t = lax.broadcasted_iota(jnp.int32, (n, GROUP_W), 0) % seq_len

    u = p_scr[rows, C_CC:C_CC + GROUP_W] * p_scr[rows, C_CX:C_CX + GROUP_W]
    hist = convb_ref[...]
    um1 = jnp.where(row_t >= 1, pltpu.roll(u, 1, 0), pltpu.roll(hist, n - 1, 0))
    um2 = jnp.where(row_t >= 2, pltpu.roll(u, 2, 0), hist)
    cy = convw_ref[0:1, :] * um2 + convw_ref[1:2, :] * um1 + convw_ref[2:3, :] * u
    a_out = p_scr[rows, C_CB:C_CB + GROUP_W] * cy * _silu(p_scr[rows, C_CZ:C_CZ + GROUP_W])
    mix_scr[rows, 0:GROUP_W] = a_out.astype(bf16)
    for g in range(group):
        conv_out[g] = u[g * seq_len + seq_len - (CONV_W - 1):(g + 1) * seq_len, :]

    qs = p_scr[rows, C_GQ:C_GQ + GROUP_W] * QK_SCALE
    k = p_scr[rows, C_GK:C_GK + GROUP_W]
    v = p_scr[rows, C_GV:C_GV + GROUP_W]
    la = _gate_log_decay(p_scr[rows, C_LR:C_LR + GATE_RANK], wg_ref, bg_ref)
    bc = _block_cumsum(la, seq_len)
    o = _dot((qs * k).astype(bf16), j_ref[...]) * v
    for d in range(1, seq_len):
        dec = jnp.exp(jnp.where(row_t >= d, bc - pltpu.roll(bc, d, 0), NEG))
        pw = qs * pltpu.roll(k, d, 0) * dec
        o = o + _dot(pw.astype(bf16), j_ref[...]) * pltpu.roll(v, d, 0)
    qd = qs * jnp.exp(bc)
    tot = [bc[(g + 1) * seq_len - 1:(g + 1) * seq_len, :] for g in range(group)]
    kd = k * jnp.exp(jnp.concatenate(
        [tot[g] - bc[g * seq_len:(g + 1) * seq_len, :] for g in range(group)], axis=0))
    o_inter = []
    for g in range(group):
        rs = slice(g * seq_len, (g + 1) * seq_len)
        alpha = jnp.exp(tot[g])
        per_head = []
        for hh in range(N_HEADS):
            ls = slice(hh * HEAD_D, (hh + 1) * HEAD_D)
            st0 = sgla_ref[g, hh]
            per_head.append(_dot_nt(qd[rs, ls].astype(bf16), st0.astype(bf16)))
            gla_out[g, hh] = st0 * alpha[:, ls] + _dot_tn(v[rs, ls].astype(bf16), kd[rs, ls].astype(bf16))
        o_inter.append(jnp.concatenate(per_head, axis=1))
    o = o + jnp.concatenate(o_inter, axis=0)
    b_out = _gla_norm_gate(o, p_scr[rows, C_GZ:C_GZ + GROUP_W], gnorm_ref, j_ref)
    mix_scr[rows, GROUP_W:2 * GROUP_W] = b_out.astype(bf16)

    cos = cos_ref[...]
    sin = sin_ref[...]
    q01 = _rope(p_scr[rows, C_SQ:C_SQ + LANES], cos, sin) * QK_SCALE
    q23 = _rope(p_scr[rows, C_SQ + LANES:C_SQ + 2 * LANES], cos, sin) * QK_SCALE
    kr = _rope(p_scr[rows, C_SK:C_SK + SWA_KV_W], cos, sin)
    vx = p_scr[rows, C_SV:C_SV + SWA_KV_W]
    qm = p_scr[rows, C_MQ:C_MQ + GROUP_W] * QK_SCALE
    hq = N_HEADS * seq_len
    sink_seq = _sink_column([sinks_ref[l, hh] for hh in range(N_HEADS)], seq_len)
    sink = jnp.concatenate([sink_seq] * group, axis=0)
    hq_all = group * hq
    qrow = lax.broadcasted_iota(jnp.int32, (hq_all, WINDOW), 0) % seq_len
    ccol = lax.broadcasted_iota(jnp.int32, (hq_all, WINDOW), 1)
    cache_valid = ccol > qrow
    nrow = lax.broadcasted_iota(jnp.int32, (hq_all, seq_len), 0) % seq_len
    ncol = lax.broadcasted_iota(jnp.int32, (hq_all, seq_len), 1)
    new_valid = ncol <= nrow
    keep_old = lax.broadcasted_iota(jnp.int32, (SWA_KV_W, WINDOW), 1) < WINDOW - seq_len

    def exact_split(a):
        hi = a.astype(bf16)
        r1 = a - hi.astype(f32)
        mid = r1.astype(bf16)
        lo = (r1 - mid.astype(f32)).astype(bf16)
        return jnp.concatenate([hi, mid, lo], axis=0)

    def shifted_cache(old_t, new_rows):
        placed = _dot_tn(exact_split(new_rows), place_ref[...])
        return jnp.where(keep_old, pltpu.roll(old_t, WINDOW - seq_len, 1), placed)

    seqs = [slice(g * seq_len, (g + 1) * seq_len) for g in range(group)]
    stk = [slice(g * hq, (g + 1) * hq) for g in range(group)]
    qst = [_stack_swa_q(q01[rs], q23[rs]).astype(bf16) for rs in seqs]
    s_c = jnp.concatenate([_dot(qst[g], kc_ref[g].astype(bf16)) for g in range(group)], axis=0)
    s_n = jnp.concatenate([_dot_nt(qst[g], kr[seqs[g]].astype(bf16)) for g in range(group)], axis=0)
    s_c = jnp.where(cache_valid, s_c, NEG)
    s_n = jnp.where(new_valid, s_n, NEG)
    mx = jnp.maximum(jnp.maximum(jnp.max(s_c, axis=1, keepdims=True), jnp.max(s_n, axis=1, keepdims=True)), sink)
    e_c = jnp.exp(s_c - mx)
    e_n = jnp.exp(s_n - mx)
    den = jnp.sum(e_c, axis=1, keepdims=True) + jnp.sum(e_n, axis=1, keepdims=True) + jnp.exp(sink - mx)
    e_c = e_c.astype(bf16)
    e_n = e_n.astype(bf16)
    ov = jnp.concatenate([_dot_nt(e_c[stk[g]], vc_ref[g].astype(bf16)) + _dot(e_n[stk[g]], vx[seqs[g]].astype(bf16))
                          for g in range(group)], axis=0) * (1.0 / den)
    oc = jnp.concatenate([_unstack_swa(ov[stk[g]], seq_len) for g in range(group)], axis=0)
    for g in range(group):
        swak_out[g] = shifted_cache(kc_ref[g], kr[seqs[g]])
        swav_out[g] = shifted_cache(vc_ref[g], vx[seqs[g]])

    sm = jnp.concatenate([_dot(_stack_heads(qm[seqs[g]]).astype(bf16), mkc_ref[g].astype(bf16))
                          for g in range(group)], axis=0)
    mm = jnp.max(sm, axis=1, keepdims=True)
    em = jnp.exp(sm - mm)
    dm = jnp.sum(em, axis=1, keepdims=True)
    em = em.astype(bf16)
    odv = jnp.concatenate([_dot_nt(em[stk[g]], mvc_ref[g].astype(bf16)) for g in range(group)], axis=0) * (1.0 / dm)
    od = jnp.concatenate([_unstack_heads(odv[stk[g]], seq_len) for g in range(group)], axis=0)
    mix_scr[rows, 2 * GROUP_W:3 * GROUP_W] = (oc * _silu(p_scr[rows, C_SZ:C_SZ + GROUP_W])).astype(bf16)
    mix_scr[rows, 3 * GROUP_W:4 * GROUP_W] = (od * _silu(p_scr[rows, C_MZ:C_MZ + GROUP_W])).astype(bf16)

    @pl.when(g_step == n_steps - 1)
    def _residual():
        for rws in chunks:
            mo = _dot(mix_scr[rws, :], wout_ref[...])
            hs[rws, :] = hs[rws, :] + _rmsnorm(mo, gpost_ref[...])

    @pl.when((l == n_layers - 1) & (g_step == n_steps - 1))
    def _store():
        cp = pltpu.make_async_copy(hs, y_hbm, sem.at[0])
        cp.start()
        cp.wait()


def _sample_layers(x, cos, sin, convb, sgla, kc, vc, mkc, mvc, gpre, gpost, wt, wout, convw, wg, bg, gnorm,
                   sinks, jmat, place, seq_len, group):
    depth, nseq = sgla.shape[0], sgla.shape[1]
    n_tok = nseq * seq_len
    assert nseq % group == 0 and seq_len == SUBLANES and n_tok % TL == 0
    n = group * seq_len

    def per_layer(shape, single_buffer=False):
        nd = len(shape)
        kw = dict(pipeline_mode=pl.Buffered(1)) if single_buffer else {}
        return pl.BlockSpec((None,) + tuple(shape), lambda l, g, _nd=nd: (l,) + (0,) * _nd, **kw)

    def per_group(shape):
        nd = len(shape) - 1
        return pl.BlockSpec((None,) + tuple(shape), lambda l, g, _nd=nd: (l, g) + (0,) * _nd)

    def const(shape):
        nd = len(shape)
        return pl.BlockSpec(shape, lambda l, g, _nd=nd: (0,) * _nd)

    in_specs = [
        pl.BlockSpec(memory_space=pltpu.SMEM),
        pl.BlockSpec(memory_space=pl.ANY),
        const((n, LANES)), const((n, LANES)),
        per_group((n, GROUP_W)),
        per_group((group, N_HEADS, HEAD_D, HEAD_D)),
        per_group((group, SWA_KV_W, WINDOW)), per_group((group, SWA_KV_W, WINDOW)),
        per_group((group, GROUP_W, N_MEM)), per_group((group, GROUP_W, N_MEM)),
        per_layer((1, D_MODEL)), per_layer((1, D_MODEL)),
        per_layer((IN_WIDTH, D_MODEL), True),
        per_layer((D_MODEL, D_MODEL), True),
        per_layer((CONV_W, GROUP_W)), per_layer((GATE_RANK, GROUP_W)), per_layer((1, GROUP_W)), per_layer((1, GROUP_W)),
        const((GROUP_W, GROUP_W)), const((3 * seq_len, WINDOW)),
    ]
    out_shape = (
        jax.ShapeDtypeStruct((n_tok, D_MODEL), f32),
        jax.ShapeDtypeStruct((depth, nseq, CONV_W - 1, GROUP_W), f32),
        jax.ShapeDtypeStruct((depth, nseq, N_HEADS, HEAD_D, HEAD_D), f32),
        jax.ShapeDtypeStruct((depth, nseq, SWA_KV_W, WINDOW), f32),
        jax.ShapeDtypeStruct((depth, nseq, SWA_KV_W, WINDOW), f32),
    )
    out_specs = (
        pl.BlockSpec(memory_space=pl.ANY),
        per_group((group, CONV_W - 1, GROUP_W)),
        per_group((group, N_HEADS, HEAD_D, HEAD_D)),
        per_group((group, SWA_KV_W, WINDOW)), per_group((group, SWA_KV_W, WINDOW)),
    )
    scratch = [
        pltpu.VMEM((n_tok, D_MODEL), f32),
        pltpu.VMEM((n_tok, NP), f32),
        pltpu.VMEM((n_tok, D_MODEL), bf16),
        pltpu.SemaphoreType.DMA((1,)),
    ]
    return pl.pallas_call(
        functools.partial(_sample_kernel, seq_len, group),
        grid=(depth, nseq // group),
        in_specs=in_specs,
        out_specs=out_specs,
        out_shape=out_shape,
        scratch_shapes=scratch,
        compiler_params=pltpu.CompilerParams(dimension_semantics=("arbitrary", "arbitrary"),
                                             vmem_limit_bytes=VMEM_LIMIT_BYTES),
        name="sample_layers",
    )(sinks, x, cos, sin, convb, sgla, kc, vc, mkc, mvc, gpre, gpost, wt, wout, convw, wg, bg, gnorm,
      jmat, place)


def _rope_tables(pos):
    half = HEAD_D // 2
    inv = np.power(ROPE_THETA, -np.arange(half, dtype=np.float64) / half)
    ang = np.asarray(pos, np.float64)[:, None] * inv[None, :]
    cos = np.tile(np.cos(ang), (1, LANES // half))
    sin = np.sin(ang)
    sin_signed = np.tile(np.concatenate([-sin, sin], axis=1), (1, LANES // HEAD_D))
    return cos.astype(np.float32), sin_signed.astype(np.float32)


def _rows_to_kernel_order(a):
    n, w = a.shape
    return a.reshape(n // SUB, SUBLANES, GLA_BLOCK, w).swapaxes(1, 2).reshape(n, w)


def _feature_major(cache):
    d, s, p, h, e = cache.shape
    return jnp.transpose(cache, (0, 1, 3, 4, 2)).reshape(d, s, h * e, p)


def _position_major(cache_t, heads):
    d, s, he, p = cache_t.shape
    return jnp.transpose(cache_t.reshape(d, s, heads, he // heads, p), (0, 1, 4, 2, 3))


def kernel(x_prompt, x_sample, state_conv, state_gla, cache_swa_k, cache_swa_v, cache_mem_k, cache_mem_v,
           mem_prompt, norm_pre, norm_post, w_in, conv_w, gla_w_gate, gla_b_gate, gla_norm, swa_sinks,
           w_mem_kv, w_out):
    depth = w_in.shape[0]
    B, L, _ = x_prompt.shape
    nseq, seq_len, _ = x_sample.shape
    group = 8

    wt = jnp.swapaxes(w_in, 1, 2).astype(bf16)
    wout = w_out.astype(bf16)
    wmem = w_mem_kv.astype(bf16)
    wg = gla_w_gate.astype(bf16)
    head_id = np.arange(GROUP_W) // HEAD_D
    jmat = jnp.asarray(head_id[:, None] == head_id[None, :], dtype=bf16)
    lvl, swab = _subtile_constants()
    place = np.zeros((3 * seq_len, WINDOW), np.float32)
    for piece in range(3):
        place[piece * seq_len + np.arange(seq_len), WINDOW - seq_len + np.arange(seq_len)] = 1.0
    place = jnp.asarray(place, dtype=bf16)
    sinks = swa_sinks.astype(f32)
    gpre = norm_pre[:, None, :]
    gpost = norm_post[:, None, :]
    bg = gla_b_gate[:, None, :]
    gn = gla_norm[:, None, :]

    cos_p, sin_p = (jnp.asarray(_rows_to_kernel_order(a)) for a in _rope_tables(np.arange(L)))
    cos_s, sin_s = (jnp.asarray(np.tile(a, (group, 1))) for a in _rope_tables(PAST_LEN + np.arange(seq_len)))

    convb = jnp.pad(state_conv, ((0, 0), (0, 0), (0, seq_len - (CONV_W - 1)), (0, 0)))
    convb = convb.reshape(depth, nseq * seq_len, GROUP_W)
    ys, conv_s, gla_s, swak_s, swav_s = _sample_layers(
        x_sample.reshape(nseq * seq_len, D_MODEL), cos_s, sin_s, convb, jnp.swapaxes(state_gla, -1, -2),
        _feature_major(cache_swa_k), _feature_major(cache_swa_v),
        _feature_major(cache_mem_k), _feature_major(cache_mem_v),
        gpre, gpost, wt, wout, conv_w, wg, bg, gn, sinks, jmat, place, seq_len, group)

    hp = _to_kernel_order(x_prompt, 1)
    outs_p = [[] for _ in range(6)]
    for l in range(depth):
        res = _prompt_layer(l, hp, mem_prompt, cos_p, sin_p, gpre, gpost, wt, wout, wmem, conv_w, wg, bg, gn,
                            sinks, jmat, lvl, swab)
        hp = res[0]
        for i in range(6):
            outs_p[i].append(res[i + 1])

    return (_from_kernel_order(hp, 1), ys.reshape(nseq, seq_len, D_MODEL),
            jnp.stack(outs_p[0]), jnp.stack(outs_p[1]),
            _position_major(jnp.stack(outs_p[2]), 2), _position_major(jnp.stack(outs_p[3]), 2),
            _position_major(jnp.stack(outs_p[4]), N_HEADS), _position_major(jnp.stack(outs_p[5]), N_HEADS),
            conv_s, jnp.swapaxes(gla_s, -1, -2), _position_major(swak_s, 2), _position_major(swav_s, 2))
```

```python
import functools

import jax
import jax.numpy as jnp
import numpy as np
from jax import lax
from jax.experimental import pallas as pl
from jax.experimental.pallas import tpu as pltpu

f32 = jnp.float32
bf16 = jnp.bfloat16

D_MODEL = 1024
GROUP_W = 256
HEAD_D = 64
N_HEADS = 4
SWA_KV_W = 128
N_MEM = 256
WINDOW = 128
CONV_W = 3
GATE_RANK = 16
GATE_NORM = 16.0
GLA_BLOCK = 16
ROPE_THETA = 10000.0
PAST_LEN = 8192
EPS = 1e-6
NEG = -1e30
QK_SCALE = HEAD_D ** -0.5

LANES = 128
SUBLANES = 8
VMEM_LIMIT_BYTES = 56 * 1024 * 1024

C_CX, C_CB, C_CC, C_CZ = 0, 256, 512, 768
C_GQ, C_GK, C_GV, C_GZ = 1024, 1280, 1536, 1792
C_SQ, C_SK, C_SV, C_SZ = 2048, 2304, 2432, 2560
C_MQ, C_MZ = 2816, 3072
C_LR = 3328
NP = 3456
IN_WIDTH = 3344
_O_GLR, _O_GZ = 1792, 1808

SUB = 128
TL = 512


def _dot(a, b):
    return jnp.dot(a, b, preferred_element_type=f32)


def _dot_nt(a, b):
    return lax.dot_general(a, b, (((1,), (1,)), ((), ())), preferred_element_type=f32)


def _dot_tn(a, b):
    return lax.dot_general(a, b, (((0,), (0,)), ((), ())), preferred_element_type=f32)


def _rmsnorm(x, g):
    return x * lax.rsqrt(jnp.mean(x * x, axis=-1, keepdims=True) + EPS) * g


def _silu(z):
    return z * (0.5 + 0.5 * jnp.tanh(0.5 * z))


def _log_sigmoid(x):
    return jnp.minimum(x, 0.0) - jnp.log1p(jnp.exp(-jnp.abs(x)))


PROJ_PIECES = tuple([(c, c + 256, c) for c in range(0, _O_GLR, 256)]
                    + [(_O_GZ + c, _O_GZ + c + 256, _O_GLR + c) for c in range(0, IN_WIDTH - _O_GZ, 256)]
                    + [(_O_GLR, _O_GZ, C_LR)])


def _project_piece(h, wt_ref, p_ref, rows, piece):
    w0, w1, c0 = PROJ_PIECES[piece]
    p_ref[rows, c0:c0 + (w1 - w0)] = _dot_nt(h, wt_ref[w0:w1, :])


def _project(h, wt_ref, p_ref, rows):
    for piece in range(len(PROJ_PIECES)):
        _project_piece(h, wt_ref, p_ref, rows, piece)


def _head_sum(x, j_ref):
    hi = x.astype(bf16)
    lo = (x - hi.astype(f32)).astype(bf16)
    return _dot(hi, j_ref[...]) + _dot(lo, j_ref[...])


def _rope(x, cos, sin_signed):
    lane = lax.broadcasted_iota(jnp.int32, x.shape, 1)
    swapped = jnp.where((lane % HEAD_D) < HEAD_D // 2,
                        pltpu.roll(x, LANES - HEAD_D // 2, 1), pltpu.roll(x, HEAD_D // 2, 1))
    return x * cos + swapped * sin_signed


def _stack_heads(x):
    lane_head = lax.broadcasted_iota(jnp.int32, x.shape, 1) // HEAD_D
    return jnp.concatenate([jnp.where(lane_head == h, x, 0.0) for h in range(N_HEADS)], axis=0)


def _unstack_heads(o, n):
    lane_head = lax.broadcasted_iota(jnp.int32, (n, GROUP_W), 1) // HEAD_D
    out = o[0:n]
    for h in range(1, N_HEADS):
        out = jnp.where(lane_head == h, o[h * n:(h + 1) * n], out)
    return out


def _stack_swa_q(q01, q23):
    low = lax.broadcasted_iota(jnp.int32, q01.shape, 1) < HEAD_D
    return jnp.concatenate([jnp.where(low, q01, 0.0), jnp.where(low, pltpu.roll(q01, HEAD_D, 1), 0.0),
                            jnp.where(low, 0.0, pltpu.roll(q23, HEAD_D, 1)), jnp.where(low, 0.0, q23)], axis=0)


def _unstack_swa(o, n):
    low = lax.broadcasted_iota(jnp.int32, (n, SWA_KV_W), 1) < HEAD_D
    c01 = jnp.where(low, o[0:n], pltpu.roll(o[n:2 * n], HEAD_D, 1))
    c23 = jnp.where(low, pltpu.roll(o[2 * n:3 * n], HEAD_D, 1), o[3 * n:4 * n])
    return jnp.concatenate([c01, c23], axis=1)


def _sink_column(sinks, n):
    return jnp.concatenate([jnp.full((n, 1), s, f32) for s in sinks], axis=0)


def _block_cumsum(la, block):
    row = lax.broadcasted_iota(jnp.int32, la.shape, 0) % block
    b = la
    s = 1
    while s < block:
        b = b + jnp.where(row >= s, pltpu.roll(b, s, 0), 0.0)
        s *= 2
    return b


def _gate_log_decay(p_lr, wg_ref, bg_ref):
    pre = _dot(p_lr.astype(bf16), wg_ref[...]) + bg_ref[...]
    return _log_sigmoid(pre) * (1.0 / GATE_NORM)


def _gla_norm_gate(o, gz, gnorm_ref, j_ref):
    ms = _head_sum(o * o, j_ref) * (1.0 / HEAD_D)
    return o * lax.rsqrt(ms + EPS) * gnorm_ref[...] * _silu(gz)


GROUPS = SUB // SUBLANES
LEVEL_GROUPS = (1, 2, 4)


def _row_groups(x):
    return [x[SUBLANES * r:SUBLANES * (r + 1), :] for r in range(x.shape[0] // SUBLANES)]


def _prompt_layer_kernel(nt, *refs):
    s = pl.program_id(0)
    pl.when(s == 0)(functools.partial(_prompt_first_step, *refs))
    for parity in range(2):
        pl.when((s > 0) & (s % 2 == parity))(functools.partial(_prompt_layer_step, nt, parity, *refs))


def _prompt_first_step(sinks_ref, x_ref, memp_ref, cos_ref, sin_ref, gpre_ref, gpost_ref, wt_ref,
                       wout_ref, wmem_ref, convw_ref, wg_ref, bg_ref, gnorm_ref, j_ref, lvl_ref, swab_ref,
                       y_ref, conv_out, gla_out, swak_out, swav_out, mk_out, mv_out,
                       p_buf, x_prev, woutb, *unused):
    x = x_ref[0]
    _project(_rmsnorm(x, gpre_ref[...]).astype(bf16), wt_ref, p_buf.at[0], slice(None))
    x_prev[...] = x
    woutb[...] = wout_ref[...].astype(bf16)


def _prompt_layer_step(nt, parity, sinks_ref, x_ref, memp_ref, cos_ref, sin_ref, gpre_ref, gpost_ref, wt_ref,
                       wout_ref, wmem_ref, convw_ref, wg_ref, bg_ref, gnorm_ref, j_ref, lvl_ref, swab_ref,
                       y_ref, conv_out, gla_out, swak_out, swav_out, mk_out, mv_out,
                       p_buf, x_prev, woutb, mix_scr, cbuf, st_scr, kprev, vprev, vtprev, mkb, mvtb, kv_nat):
    s = pl.program_id(0)
    t = jnp.maximum(s - 1, 0) % nt
    nsub = TL // SUB
    p_wr = p_buf.at[parity]
    p_scr = p_buf.at[1 - parity]

    @pl.when(t == 0)
    def _new_sequence():
        cbuf[...] = jnp.zeros((SUBLANES, GROUP_W), f32)
        st_scr[...] = jnp.zeros((GROUP_W, GROUP_W), f32)
        kprev[...] = jnp.zeros((SUB, SWA_KV_W), f32)
        vprev[...] = jnp.zeros((SUB, SWA_KV_W), f32)
        vtprev[...] = jnp.zeros((SWA_KV_W, SUB), f32)
        mkv = _dot(memp_ref[0].astype(bf16), wmem_ref[...].astype(bf16))
        mvt = mkv[:, GROUP_W:2 * GROUP_W].T
        mk_out[0] = mkv[:, 0:GROUP_W].T
        mv_out[0] = mvt
        mkb[...] = mkv[:, 0:GROUP_W].astype(bf16)
        mvtb[...] = mvt.astype(bf16)

    x = x_ref[0]
    h = _rmsnorm(x, gpre_ref[...]).astype(bf16)

    sub8 = lax.broadcasted_iota(jnp.int32, (SUBLANES, GROUP_W), 0)
    pending = list(range(len(PROJ_PIECES)))

    def project_next():
        if pending:
            _project_piece(h, wt_ref, p_wr, slice(None), pending.pop(0))

    def sub_tile(j):
        rows = slice(j * SUB, (j + 1) * SUB)
        project_next()

        u = p_scr[rows, C_CC:C_CC + GROUP_W] * p_scr[rows, C_CX:C_CX + GROUP_W]
        last = SUB - SUBLANES
        prev1 = jnp.where(sub8 == 0, cbuf[0:1, :], pltpu.roll(u[last:SUB, :], 1, 0))
        prev2 = jnp.where(sub8 == 0, cbuf[1:2, :], pltpu.roll(u[last - SUBLANES:last, :], 1, 0))
        um1 = jnp.concatenate([prev1, u[0:last, :]], axis=0)
        um2 = jnp.concatenate([prev2, prev1, u[0:last - SUBLANES, :]], axis=0)
        cy = convw_ref[0:1, :] * um2 + convw_ref[1:2, :] * um1 + convw_ref[2:3, :] * u
        cbuf[0:1, :] = u[SUB - 1:SUB, :]
        cbuf[1:2, :] = u[last - 1:last, :]
        a_out = p_scr[rows, C_CB:C_CB + GROUP_W] * cy * _silu(p_scr[rows, C_CZ:C_CZ + GROUP_W])
        mix_scr[rows, 0:GROUP_W] = a_out.astype(bf16)

        qg = _row_groups(p_scr[rows, C_GQ:C_GQ + GROUP_W] * QK_SCALE)
        k = p_scr[rows, C_GK:C_GK + GROUP_W]
        v = p_scr[rows, C_GV:C_GV + GROUP_W]
        kg = _row_groups(k)
        vg = _row_groups(v)
        lag = _row_groups(_gate_log_decay(p_scr[rows, C_LR:C_LR + GATE_RANK], wg_ref, bg_ref))
        bg_ = [lag[0]]
        for r in range(1, GROUPS):
            bg_.append(bg_[-1] + lag[r])
        tot = bg_[GROUPS - 1]
        pw = []
        for r in range(GROUPS):
            for s in range(r):
                pw.append(qg[r] * kg[s] * jnp.exp(bg_[r] - bg_[s]))
            pw.append(qg[r] * kg[r])
        scores = _dot(jnp.concatenate(pw, axis=0).astype(bf16), j_ref[...])
        project_next()
        og = []
        idx = 0
        for r in range(GROUPS):
            acc = None
            for s in range(r + 1):
                term = scores[SUBLANES * idx:SUBLANES * (idx + 1), :] * vg[s]
                acc = term if acc is None else acc + term
                idx += 1
            og.append(acc)
        o = jnp.concatenate(og, axis=0)

        def decayed(group):
            before = jnp.zeros((SUBLANES, GROUP_W), f32)
            after = jnp.zeros((SUBLANES, GROUP_W), f32)
            for s in range(1, group):
                before = before + jnp.where(sub8 % group >= s, pltpu.roll(tot, s, 0), 0.0)
                after = after + jnp.where(sub8 % group < group - s, pltpu.roll(tot, SUBLANES - s, 0), 0.0)
            qd = jnp.concatenate([qg[r] * jnp.exp(bg_[r] + before) for r in range(GROUPS)], axis=0)
            kd = jnp.concatenate([kg[r] * jnp.exp((tot - bg_[r]) + after) for r in range(GROUPS)], axis=0)
            return qd, kd

        attn = None
        for li, group in enumerate(LEVEL_GROUPS):
            qd, kd = decayed(group)
            s = _dot_nt(_stack_heads(qd).astype(bf16), kd.astype(bf16)) * lvl_ref[li]
            attn = s if attn is None else attn + s
        o = o + _unstack_heads(_dot(attn.astype(bf16), v.astype(bf16)), SUB)
        project_next()
        qd, kd = decayed(SUB // GLA_BLOCK)
        st = st_scr[...]
        o = o + _dot_nt(qd.astype(bf16), st.astype(bf16))
        total = jnp.sum(tot, axis=0, keepdims=True)
        upd = _dot_tn(v.astype(bf16), kd.astype(bf16))
        same_head = (lax.broadcasted_iota(jnp.int32, (GROUP_W, GROUP_W), 0) // HEAD_D
                     == lax.broadcasted_iota(jnp.int32, (GROUP_W, GROUP_W), 1) // HEAD_D)
        st_scr[...] = st * jnp.exp(total) + jnp.where(same_head, upd, 0.0)
        b_out = _gla_norm_gate(o, p_scr[rows, C_GZ:C_GZ + GROUP_W], gnorm_ref, j_ref)
        mix_scr[rows, GROUP_W:2 * GROUP_W] = b_out.astype(bf16)

        cos = cos_ref[rows, :]
        sin = sin_ref[rows, :]
        q01 = _rope(p_scr[rows, C_SQ:C_SQ + LANES], cos, sin) * QK_SCALE
        q23 = _rope(p_scr[rows, C_SQ + LANES:C_SQ + 2 * LANES], cos, sin) * QK_SCALE
        kr = _rope(p_scr[rows, C_SK:C_SK + SWA_KV_W], cos, sin)
        vx = p_scr[rows, C_SV:C_SV + SWA_KV_W]
        k_all = jnp.concatenate([kprev[...], kr], axis=0).astype(bf16)
        vt_new = vx.T
        vt_all = jnp.concatenate([vtprev[...], vt_new], axis=1).astype(bf16)
        first = jnp.where((t * nsub + j) == 0, 1, 0)
        sc = _dot_nt(k_all, _stack_swa_q(q01, q23).astype(bf16)) + swab_ref[first]
        lane_head = lax.broadcasted_iota(jnp.int32, (1, N_HEADS * SUB), 1) // SUB
        sink = jnp.full((1, N_HEADS * SUB), sinks_ref[0], f32)
        for hh in range(1, N_HEADS):
            sink = jnp.where(lane_head == hh, sinks_ref[hh], sink)
        mx = jnp.maximum(jnp.max(sc, axis=0, keepdims=True), sink)
        e = jnp.exp(sc - mx)
        den = jnp.sum(e, axis=0, keepdims=True) + jnp.exp(sink - mx)
        ot = _dot(vt_all, e.astype(bf16)) * (1.0 / den)
        oc = _unstack_swa(jnp.concatenate([ot[:, SUB * hh:SUB * (hh + 1)].T for hh in range(N_HEADS)], axis=0), SUB)
        mix_scr[rows, 2 * GROUP_W:3 * GROUP_W] = (oc * _silu(p_scr[rows, C_SZ:C_SZ + GROUP_W])).astype(bf16)
        kprev[...] = kr
        vprev[...] = vx
        vtprev[...] = vt_new
        project_next()

        qm = _stack_heads(p_scr[rows, C_MQ:C_MQ + GROUP_W] * QK_SCALE).astype(bf16)
        sm = _dot_nt(mkb[...], qm)
        mm = jnp.max(sm, axis=0, keepdims=True)
        em = jnp.exp(sm - mm)
        dm = jnp.sum(em, axis=0, keepdims=True)
        odt = _dot(mvtb[...], em.astype(bf16)) * (1.0 / dm)
        low = lax.broadcasted_iota(jnp.int32, (SUB, LANES), 1) < HEAD_D
        halves = []
        for pair in range(N_HEADS // 2):
            blk_rows = odt[LANES * pair:LANES * (pair + 1), :]
            even = blk_rows[:, SUB * (2 * pair):SUB * (2 * pair + 1)].T
            odd = blk_rows[:, SUB * (2 * pair + 1):SUB * (2 * pair + 2)].T
            halves.append(jnp.where(low, even, odd))
        od = jnp.concatenate(halves, axis=1)
        mix_scr[rows, 3 * GROUP_W:4 * GROUP_W] = (od * _silu(p_scr[rows, C_MZ:C_MZ + GROUP_W])).astype(bf16)

    for j in range(nsub):
        sub_tile(j)
    while pending:
        project_next()

    mo = _dot(mix_scr[...], woutb[...])
    y_ref[0] = x_prev[...] + _rmsnorm(mo, gpost_ref[...])
    x_prev[...] = x

    @pl.when(t == nt - 1)
    def _sequence_done():
        conv_out[0, 0:1, :] = cbuf[1:2, :]
        conv_out[0, 1:2, :] = cbuf[0:1, :]
        for src, dst in ((kprev, swak_out), (vprev, swav_out)):
            for r in range(GROUPS):
                kv_nat[pl.ds(r, SUBLANES, stride=GLA_BLOCK), :] = src[SUBLANES * r:SUBLANES * (r + 1), :]
            dst[0] = kv_nat[...].T
        st = st_scr[...]
        for hh in range(N_HEADS):
            gla_out[0, hh] = st[HEAD_D * hh:HEAD_D * (hh + 1), HEAD_D * hh:HEAD_D * (hh + 1)].T


def _layer_spec(shape, l):
    nd = len(shape)
    return pl.BlockSpec((None,) + tuple(shape), lambda s, _l=l, _nd=nd: (_l,) + (0,) * _nd)


def _const_spec(shape):
    nd = len(shape)
    return pl.BlockSpec(shape, lambda s, _nd=nd: (0,) * _nd)


def _prompt_layer(l, x, memp, cos, sin, gpre, gpost, wt, wout, wmem, convw, wg, bg, gnorm, sinks, jmat,
                  lvl, swab):
    B, L, _ = x.shape
    assert L % TL == 0
    nt = L // TL
    n_tiles = B * nt

    def projected(s):
        tile = jnp.minimum(s, n_tiles - 1)
        return tile // nt, tile % nt

    def mixed(s):
        tile = jnp.maximum(s - 1, 0)
        return tile // nt, tile % nt

    def per_sequence(shape):
        nd = len(shape) - 1
        return pl.BlockSpec(shape, lambda s, _nd=nd: (mixed(s)[0],) + (0,) * _nd)

    out_shape = (
        jax.ShapeDtypeStruct((B, L, D_MODEL), f32),
        jax.ShapeDtypeStruct((B, CONV_W - 1, GROUP_W), f32),
        jax.ShapeDtypeStruct((B, N_HEADS, HEAD_D, HEAD_D), f32),
        jax.ShapeDtypeStruct((B, SWA_KV_W, WINDOW), f32),
        jax.ShapeDtypeStruct((B, SWA_KV_W, WINDOW), f32),
        jax.ShapeDtypeStruct((B, GROUP_W, N_MEM), f32),
        jax.ShapeDtypeStruct((B, GROUP_W, N_MEM), f32),
    )
    in_specs = [
        pl.BlockSpec(memory_space=pltpu.SMEM),
        pl.BlockSpec((1, TL, D_MODEL), lambda s: projected(s) + (0,)),
        per_sequence((1, N_MEM, D_MODEL)),
        pl.BlockSpec((TL, LANES), lambda s: (mixed(s)[1], 0)),
        pl.BlockSpec((TL, LANES), lambda s: (mixed(s)[1], 0)),
        _layer_spec((1, D_MODEL), l), _layer_spec((1, D_MODEL), l),
        _layer_spec((IN_WIDTH, D_MODEL), l),
        _layer_spec((D_MODEL, D_MODEL), l), _layer_spec((D_MODEL, 2 * GROUP_W), l),
        _layer_spec((CONV_W, GROUP_W), l), _layer_spec((GATE_RANK, GROUP_W), l), _layer_spec((1, GROUP_W), l),
        _layer_spec((1, GROUP_W), l), _const_spec((GROUP_W, GROUP_W)),
        _const_spec((len(LEVEL_GROUPS), N_HEADS * SUB, SUB)), _const_spec((2, 2 * SUB, N_HEADS * SUB)),
    ]
    out_specs = (
        pl.BlockSpec((1, TL, D_MODEL), lambda s: mixed(s) + (0,)),
        per_sequence((1, CONV_W - 1, GROUP_W)),
        per_sequence((1, N_HEADS, HEAD_D, HEAD_D)),
        per_sequence((1, SWA_KV_W, WINDOW)),
        per_sequence((1, SWA_KV_W, WINDOW)),
        per_sequence((1, GROUP_W, N_MEM)),
        per_sequence((1, GROUP_W, N_MEM)),
    )
    scratch = [
        pltpu.VMEM((2, TL, NP), f32),
        pltpu.VMEM((TL, D_MODEL), f32),
        pltpu.VMEM((D_MODEL, D_MODEL), bf16),
        pltpu.VMEM((TL, D_MODEL), bf16),
        pltpu.VMEM((SUBLANES, GROUP_W), f32),
        pltpu.VMEM((GROUP_W, GROUP_W), f32),
        pltpu.VMEM((SUB, SWA_KV_W), f32),
        pltpu.VMEM((SUB, SWA_KV_W), f32),
        pltpu.VMEM((SWA_KV_W, SUB), f32),
        pltpu.VMEM((N_MEM, GROUP_W), bf16),
        pltpu.VMEM((GROUP_W, N_MEM), bf16),
        pltpu.VMEM((WINDOW, SWA_KV_W), f32),
    ]
    return pl.pallas_call(
        functools.partial(_prompt_layer_kernel, nt),
        grid=(n_tiles + 1,),
        in_specs=in_specs,
        out_specs=out_specs,
        out_shape=out_shape,
        scratch_shapes=scratch,
        compiler_params=pltpu.CompilerParams(dimension_semantics=("arbitrary",),
                                             vmem_limit_bytes=VMEM_LIMIT_BYTES),
        name="prompt_layer",
    )(sinks[l], x, memp, cos, sin, gpre, gpost, wt, wout, wmem, convw, wg, bg, gnorm, jmat, lvl, swab)


def _subtile_constants():
    row = np.arange(SUB)
    tok = (row % SUBLANES) * GLA_BLOCK + row // SUBLANES
    blk = row % SUBLANES
    levels = []
    for group in LEVEL_GROUPS:
        g = blk // group
        sel = (g[:, None] == g[None, :] + 1) & (g[:, None] % 2 == 1)
        levels.append(np.tile(sel, (N_HEADS, 1)))
    key_tok = np.concatenate([tok - SUB, tok])
    valid = (key_tok[None, :] <= tok[:, None]) & (key_tok[None, :] > tok[:, None] - WINDOW)
    bias = [np.where(valid, 0.0, NEG), np.where(valid & (key_tok[None, :] >= 0), 0.0, NEG)]
    bias = np.stack([np.tile(b, (N_HEADS, 1)).T for b in bias])
    return jnp.asarray(np.stack(levels), dtype=f32), jnp.asarray(bias, dtype=f32)


def _to_kernel_order(x, axis):
    shp = x.shape
    n = shp[axis]
    x = x.reshape(shp[:axis] + (n // SUB, SUBLANES, GLA_BLOCK) + shp[axis + 1:])
    return jnp.swapaxes(x, axis + 1, axis + 2).reshape(shp)


def _from_kernel_order(x, axis):
    shp = x.shape
    n = shp[axis]
    x = x.reshape(shp[:axis] + (n // SUB, GLA_BLOCK, SUBLANES) + shp[axis + 1:])
    return jnp.swapaxes(x, axis + 1, axis + 2).reshape(shp)


def _sample_kernel(seq_len, group, sinks_ref, x_hbm, cos_ref, sin_ref, convb_ref, sgla_ref, kc_ref, vc_ref,
                   mkc_ref, mvc_ref, gpre_ref, gpost_ref, wt_ref, wout_ref, convw_ref, wg_ref, bg_ref,
                   gnorm_ref, j_ref, place_ref,
                   y_hbm, conv_out, gla_out, swak_out, swav_out,
                   hs, p_scr, mix_scr, woutb, sem):
    l = pl.program_id(0)
    g_step = pl.program_id(1)
    n_layers = pl.num_programs(0)
    n_steps = pl.num_programs(1)
    n_tok = hs.shape[0]
    n = group * seq_len
    chunks = [slice(c * TL, (c + 1) * TL) for c in range(n_tok // TL)]

    @pl.when((l == 0) & (g_step == 0))
    def _load():
        cp = pltpu.make_async_copy(x_hbm, hs, sem.at[0])
        cp.start()
        cp.wait()

    @pl.when(g_step == 0)
    def _project_all():
        woutb[...] = wout_ref[...].astype(bf16)
        for rows in chunks:
            h = _rmsnorm(hs[rows, :], gpre_ref[...]).astype(bf16)
            _project(h, wt_ref, p_scr, rows)

    rows = pl.ds(pl.multiple_of(g_step * n, n), n)
    row_t = lax.broadcasted_iota(jnp.int32, (n, GROUP_W), 0) % seq_len

    u = p_scr[rows, C_CC:C_CC + GROUP_W] * p_scr[rows, C_CX:C_CX + GROUP_W]
    hist = convb_ref[...]
    um1 = jnp.where(row_t >= 1, pltpu.roll(u, 1, 0), pltpu.roll(hist, n - 1, 0))
    um2 = jnp.where(row_t >= 2, pltpu.roll(u, 2, 0), hist)
    cy = convw_ref[0:1, :] * um2 + convw_ref[1:2, :] * um1 + convw_ref[2:3, :] * u
    a_out = p_scr[rows, C_CB:C_CB + GROUP_W] * cy * _silu(p_scr[rows, C_CZ:C_CZ + GROUP_W])
    mix_scr[rows, 0:GROUP_W] = a_out.astype(bf16)
    for g in range(group):
        conv_out[g] = u[g * seq_len + seq_len - (CONV_W - 1):(g + 1) * seq_len, :]

    qs = p_scr[rows, C_GQ:C_GQ + GROUP_W] * QK_SCALE
    k = p_scr[rows, C_GK:C_GK + GROUP_W]
    v = p_scr[rows, C_GV:C_GV + GROUP_W]
    la = _gate_log_decay(p_scr[rows, C_LR:C_LR + GATE_RANK], wg_ref, bg_ref)
    bc = _block_cumsum(la, seq_len)
    o = _dot((qs * k).astype(bf16), j_ref[...]) * v
    for d in range(1, seq_len):
        dec = jnp.exp(jnp.where(row_t >= d, bc - pltpu.roll(bc, d, 0), NEG))
        pw = qs * pltpu.roll(k, d, 0) * dec
        o = o + _dot(pw.astype(bf16), j_ref[...]) * pltpu.roll(v, d, 0)
    qd = qs * jnp.exp(bc)
    tot = [bc[(g + 1) * seq_len - 1:(g + 1) * seq_len, :] for g in range(group)]
    kd = k * jnp.exp(jnp.concatenate(
        [tot[g] - bc[g * seq_len:(g + 1) * seq_len, :] for g in range(group)], axis=0))
    o_inter = []
    for g in range(group):
        rs = slice(g * seq_len, (g + 1) * seq_len)
        alpha = jnp.exp(tot[g])
        per_head = []
        for hh in range(N_HEADS):
            ls = slice(hh * HEAD_D, (hh + 1) * HEAD_D)
            st0 = sgla_ref[g, hh]
            per_head.append(_dot_nt(qd[rs, ls].astype(bf16), st0.astype(bf16)))
            gla_out[g, hh] = st0 * alpha[:, ls] + _dot_tn(v[rs, ls].astype(bf16), kd[rs, ls].astype(bf16))
        o_inter.append(jnp.concatenate(per_head, axis=1))
    o = o + jnp.concatenate(o_inter, axis=0)
    b_out = _gla_norm_gate(o, p_scr[rows, C_GZ:C_GZ + GROUP_W], gnorm_ref, j_ref)
    mix_scr[rows, GROUP_W:2 * GROUP_W] = b_out.astype(bf16)

    cos = cos_ref[...]
    sin = sin_ref[...]
    q01 = _rope(p_scr[rows, C_SQ:C_SQ + LANES], cos, sin) * QK_SCALE
    q23 = _rope(p_scr[rows, C_SQ + LANES:C_SQ + 2 * LANES], cos, sin) * QK_SCALE
    kr = _rope(p_scr[rows, C_SK:C_SK + SWA_KV_W], cos, sin)
    vx = p_scr[rows, C_SV:C_SV + SWA_KV_W]
    qm = p_scr[rows, C_MQ:C_MQ + GROUP_W] * QK_SCALE
    hq = N_HEADS * seq_len
    sink_seq = _sink_column([sinks_ref[l, hh] for hh in range(N_HEADS)], seq_len)
    sink = jnp.concatenate([sink_seq] * group, axis=0)
    hq_all = group * hq
    qrow = lax.broadcasted_iota(jnp.int32, (hq_all, WINDOW), 0) % seq_len
    ccol = lax.broadcasted_iota(jnp.int32, (hq_all, WINDOW), 1)
    cache_valid = ccol > qrow
    nrow = lax.broadcasted_iota(jnp.int32, (hq_all, seq_len), 0) % seq_len
    ncol = lax.broadcasted_iota(jnp.int32, (hq_all, seq_len), 1)
    new_valid = ncol <= nrow
    keep_old = lax.broadcasted_iota(jnp.int32, (SWA_KV_W, WINDOW), 1) < WINDOW - seq_len

    def exact_split(a):
        hi = a.astype(bf16)
        r1 = a - hi.astype(f32)
        mid = r1.astype(bf16)
        lo = (r1 - mid.astype(f32)).astype(bf16)
        return jnp.concatenate([hi, mid, lo], axis=0)

    def shifted_cache(old_t, new_rows):
        placed = _dot_tn(exact_split(new_rows), place_ref[...])
        return jnp.where(keep_old, pltpu.roll(old_t, WINDOW - seq_len, 1), placed)

    seqs = [slice(g * seq_len, (g + 1) * seq_len) for g in range(group)]
    stk = [slice(g * hq, (g + 1) * hq) for g in range(group)]
    qst = [_stack_swa_q(q01[rs], q23[rs]).astype(bf16) for rs in seqs]
    s_c = jnp.concatenate([_dot(qst[g], kc_ref[g].astype(bf16)) for g in range(group)], axis=0)
    s_n = jnp.concatenate([_dot_nt(qst[g], kr[seqs[g]].astype(bf16)) for g in range(group)], axis=0)
    s_c = jnp.where(cache_valid, s_c, NEG)
    s_n = jnp.where(new_valid, s_n, NEG)
    mx = jnp.maximum(jnp.maximum(jnp.max(s_c, axis=1, keepdims=True), jnp.max(s_n, axis=1, keepdims=True)), sink)
    e_c = jnp.exp(s_c - mx)
    e_n = jnp.exp(s_n - mx)
    den = jnp.sum(e_c, axis=1, keepdims=True) + jnp.sum(e_n, axis=1, keepdims=True) + jnp.exp(sink - mx)
    e_c = e_c.astype(bf16)
    e_n = e_n.astype(bf16)
    ov = jnp.concatenate([_dot_nt(e_c[stk[g]], vc_ref[g].astype(bf16)) + _dot(e_n[stk[g]], vx[seqs[g]].astype(bf16))
                          for g in range(group)], axis=0) * (1.0 / den)
    oc = jnp.concatenate([_unstack_swa(ov[stk[g]], seq_len) for g in range(group)], axis=0)
    for g in range(group):
        swak_out[g] = shifted_cache(kc_ref[g], kr[seqs[g]])
        swav_out[g] = shifted_cache(vc_ref[g], vx[seqs[g]])

    sm = jnp.concatenate([_dot(_stack_heads(qm[seqs[g]]).astype(bf16), mkc_ref[g].astype(bf16))
                          for g in range(group)], axis=0)
    mm = jnp.max(sm, axis=1, keepdims=True)
    em = jnp.exp(sm - mm)
    dm = jnp.sum(em, axis=1, keepdims=True)
    em = em.astype(bf16)
    odv = jnp.concatenate([_dot_nt(em[stk[g]], mvc_ref[g].astype(bf16)) for g in range(group)], axis=0) * (1.0 / dm)
    od = jnp.concatenate([_unstack_heads(odv[stk[g]], seq_len) for g in range(group)], axis=0)
    mix_scr[rows, 2 * GROUP_W:3 * GROUP_W] = (oc * _silu(p_scr[rows, C_SZ:C_SZ + GROUP_W])).astype(bf16)
    mix_scr[rows, 3 * GROUP_W:4 * GROUP_W] = (od * _silu(p_scr[rows, C_MZ:C_MZ + GROUP_W])).astype(bf16)

    @pl.when(g_step == n_steps - 1)
    def _residual():
        for rws in chunks:
            mo = _dot(mix_scr[rws, :], woutb[...])
            hs[rws, :] = hs[rws, :] + _rmsnorm(mo, gpost_ref[...])

    @pl.when((l == n_layers - 1) & (g_step == n_steps - 1))
    def _store():
        cp = pltpu.make_async_copy(hs, y_hbm, sem.at[0])
        cp.start()
        cp.wait()


def _sample_layers(x, cos, sin, convb, sgla, kc, vc, mkc, mvc, gpre, gpost, wt, wout, convw, wg, bg, gnorm,
                   sinks, jmat, place, seq_len, group):
    depth, nseq = sgla.shape[0], sgla.shape[1]
    n_tok = nseq * seq_len
    assert nseq % group == 0 and seq_len == SUBLANES and n_tok % TL == 0
    n = group * seq_len

    def per_layer(shape, single_buffer=False):
        nd = len(shape)
        kw = dict(pipeline_mode=pl.Buffered(1)) if single_buffer else {}
        return pl.BlockSpec((None,) + tuple(shape), lambda l, g, _nd=nd: (l,) + (0,) * _nd, **kw)

    def per_group(shape):
        nd = len(shape) - 1
        return pl.BlockSpec((None,) + tuple(shape), lambda l, g, _nd=nd: (l, g) + (0,) * _nd)

    def const(shape):
        nd = len(shape)
        return pl.BlockSpec(shape, lambda l, g, _nd=nd: (0,) * _nd)

    in_specs = [
        pl.BlockSpec(memory_space=pltpu.SMEM),
        pl.BlockSpec(memory_space=pl.ANY),
        const((n, LANES)), const((n, LANES)),
        per_group((n, GROUP_W)),
        per_group((group, N_HEADS, HEAD_D, HEAD_D)),
        per_group((group, SWA_KV_W, WINDOW)), per_group((group, SWA_KV_W, WINDOW)),
        per_group((group, GROUP_W, N_MEM)), per_group((group, GROUP_W, N_MEM)),
        per_layer((1, D_MODEL)), per_layer((1, D_MODEL)),
        per_layer((IN_WIDTH, D_MODEL), True),
        per_layer((D_MODEL, D_MODEL), True),
        per_layer((CONV_W, GROUP_W)), per_layer((GATE_RANK, GROUP_W)), per_layer((1, GROUP_W)), per_layer((1, GROUP_W)),
        const((GROUP_W, GROUP_W)), const((3 * seq_len, WINDOW)),
    ]
    out_shape = (
        jax.ShapeDtypeStruct((n_tok, D_MODEL), f32),
        jax.ShapeDtypeStruct((depth, nseq, CONV_W - 1, GROUP_W), f32),
        jax.ShapeDtypeStruct((depth, nseq, N_HEADS, HEAD_D, HEAD_D), f32),
        jax.ShapeDtypeStruct((depth, nseq, SWA_KV_W, WINDOW), f32),
        jax.ShapeDtypeStruct((depth, nseq, SWA_KV_W, WINDOW), f32),
    )
    out_specs = (
        pl.BlockSpec(memory_space=pl.ANY),
        per_group((group, CONV_W - 1, GROUP_W)),
        per_group((group, N_HEADS, HEAD_D, HEAD_D)),
        per_group((group, SWA_KV_W, WINDOW)), per_group((group, SWA_KV_W, WINDOW)),
    )
    scratch = [
        pltpu.VMEM((n_tok, D_MODEL), f32),
        pltpu.VMEM((n_tok, NP), f32),
        pltpu.VMEM((n_tok, D_MODEL), bf16),
        pltpu.VMEM((D_MODEL, D_MODEL), bf16),
        pltpu.SemaphoreType.DMA((1,)),
    ]
    return pl.pallas_call(
        functools.partial(_sample_kernel, seq_len, group),
        grid=(depth, nseq // group),
        in_specs=in_specs,
        out_specs=out_specs,
        out_shape=out_shape,
        scratch_shapes=scratch,
        compiler_params=pltpu.CompilerParams(dimension_semantics=("arbitrary", "arbitrary"),
                                             vmem_limit_bytes=VMEM_LIMIT_BYTES),
        name="sample_layers",
    )(sinks, x, cos, sin, convb, sgla, kc, vc, mkc, mvc, gpre, gpost, wt, wout, convw, wg, bg, gnorm,
      jmat, place)


def _rope_tables(pos):
    half = HEAD_D // 2
    inv = np.power(ROPE_THETA, -np.arange(half, dtype=np.float64) / half)
    ang = np.asarray(pos, np.float64)[:, None] * inv[None, :]
    cos = np.tile(np.cos(ang), (1, LANES // half))
    sin = np.sin(ang)
    sin_signed = np.tile(np.concatenate([-sin, sin], axis=1), (1, LANES // HEAD_D))
    return cos.astype(np.float32), sin_signed.astype(np.float32)


def _rows_to_kernel_order(a):
    n, w = a.shape
    return a.reshape(n // SUB, SUBLANES, GLA_BLOCK, w).swapaxes(1, 2).reshape(n, w)


def _feature_major(cache):
    d, s, p, h, e = cache.shape
    return jnp.transpose(cache, (0, 1, 3, 4, 2)).reshape(d, s, h * e, p)


def _position_major(cache_t, heads):
    d, s, he, p = cache_t.shape
    return jnp.transpose(cache_t.reshape(d, s, heads, he // heads, p), (0, 1, 4, 2, 3))


def kernel(x_prompt, x_sample, state_conv, state_gla, cache_swa_k, cache_swa_v, cache_mem_k, cache_mem_v,
           mem_prompt, norm_pre, norm_post, w_in, conv_w, gla_w_gate, gla_b_gate, gla_norm, swa_sinks,
           w_mem_kv, w_out):
    depth = w_in.shape[0]
    B, L, _ = x_prompt.shape
    nseq, seq_len, _ = x_sample.shape
    group = 8

    wt = jnp.swapaxes(w_in, 1, 2).astype(bf16)
    wout = w_out
    wmem = w_mem_kv
    wg = gla_w_gate.astype(bf16)
    head_id = np.arange(GROUP_W) // HEAD_D
    jmat = jnp.asarray(head_id[:, None] == head_id[None, :], dtype=bf16)
    lvl, swab = _subtile_constants()
    place = np.zeros((3 * seq_len, WINDOW), np.float32)
    for piece in range(3):
        place[piece * seq_len + np.arange(seq_len), WINDOW - seq_len + np.arange(seq_len)] = 1.0
    place = jnp.asarray(place, dtype=bf16)
    sinks = swa_sinks.astype(f32)
    gpre = norm_pre[:, None, :]
    gpost = norm_post[:, None, :]
    bg = gla_b_gate[:, None, :]
    gn = gla_norm[:, None, :]

    cos_p, sin_p = (jnp.asarray(_rows_to_kernel_order(a)) for a in _rope_tables(np.arange(L)))
    cos_s, sin_s = (jnp.asarray(np.tile(a, (group, 1))) for a in _rope_tables(PAST_LEN + np.arange(seq_len)))

    hp = _to_kernel_order(x_prompt, 1)
    outs_p = [[] for _ in range(6)]
    for l in range(depth):
        res = _prompt_layer(l, hp, mem_prompt, cos_p, sin_p, gpre, gpost, wt, wout, wmem, conv_w, wg, bg, gn,
                            sinks, jmat, lvl, swab)
        hp = res[0]
        for i in range(6):
            outs_p[i].append(res[i + 1])

    xs, hp = lax.optimization_barrier((x_sample.reshape(nseq * seq_len, D_MODEL), hp))
    convb = jnp.pad(state_conv, ((0, 0), (0, 0), (0, seq_len - (CONV_W - 1)), (0, 0)))
    convb = convb.reshape(depth, nseq * seq_len, GROUP_W)
    ys, conv_s, gla_s, swak_s, swav_s = _sample_layers(
        xs, cos_s, sin_s, convb, jnp.swapaxes(state_gla, -1, -2),
        _feature_major(cache_swa_k), _feature_major(cache_swa_v),
        _feature_major(cache_mem_k), _feature_major(cache_mem_v),
        gpre, gpost, wt, wout, conv_w, wg, bg, gn, sinks, jmat, place, seq_len, group)

    return (_from_kernel_order(hp, 1), ys.reshape(nseq, seq_len, D_MODEL),
            jnp.stack(outs_p[0]), jnp.stack(outs_p[1]),
            _position_major(jnp.stack(outs_p[2]), 2), _position_major(jnp.stack(outs_p[3]), 2),
            _position_major(jnp.stack(outs_p[4]), N_HEADS), _position_major(jnp.stack(outs_p[5]), N_HEADS),
            conv_s, jnp.swapaxes(gla_s, -1, -2), _position_major(swak_s, 2), _position_major(swav_s, 2))
```

```python
import functools

import jax
import jax.numpy as jnp
import numpy as np
from jax import lax
from jax.experimental import pallas as pl
from jax.experimental.pallas import tpu as pltpu

f32 = jnp.float32
bf16 = jnp.bfloat16

D_MODEL = 1024
GROUP_W = 256
HEAD_D = 64
N_HEADS = 4
SWA_KV_W = 128
N_MEM = 256
WINDOW = 128
CONV_W = 3
GATE_RANK = 16
GATE_NORM = 16.0
GLA_BLOCK = 16
ROPE_THETA = 10000.0
PAST_LEN = 8192
EPS = 1e-6
NEG = -1e30
QK_SCALE = HEAD_D ** -0.5

LANES = 128
SUBLANES = 8
VMEM_LIMIT_BYTES = 56 * 1024 * 1024

C_CX, C_CB, C_CC, C_CZ = 0, 256, 512, 768
C_GQ, C_GK, C_GV, C_GZ = 1024, 1280, 1536, 1792
C_SQ, C_SK, C_SV, C_SZ = 2048, 2304, 2432, 2560
C_MQ, C_MZ = 2816, 3072
C_LR = 3328
NP = 3456
IN_WIDTH = 3344
_O_GLR, _O_GZ = 1792, 1808

SUB = 128
TL = 512


def _dot(a, b):
    return jnp.dot(a, b, preferred_element_type=f32)


def _dot_nt(a, b):
    return lax.dot_general(a, b, (((1,), (1,)), ((), ())), preferred_element_type=f32)


def _dot_tn(a, b):
    return lax.dot_general(a, b, (((0,), (0,)), ((), ())), preferred_element_type=f32)


def _rmsnorm(x, g):
    return x * lax.rsqrt(jnp.mean(x * x, axis=-1, keepdims=True) + EPS) * g


def _silu(z):
    return z * (0.5 + 0.5 * jnp.tanh(0.5 * z))


def _log_sigmoid(x):
    return jnp.minimum(x, 0.0) - jnp.log1p(jnp.exp(-jnp.abs(x)))


PROJ_PIECES = tuple([(c, c + 256, c) for c in range(0, _O_GLR, 256)]
                    + [(_O_GZ + c, _O_GZ + c + 256, _O_GLR + c) for c in range(0, IN_WIDTH - _O_GZ, 256)]
                    + [(_O_GLR, _O_GZ, C_LR)])


def _project_piece(h, wt_ref, p_ref, rows, piece):
    w0, w1, c0 = PROJ_PIECES[piece]
    p_ref[rows, c0:c0 + (w1 - w0)] = _dot_nt(h, wt_ref[w0:w1, :])


def _project(h, wt_ref, p_ref, rows):
    for piece in range(len(PROJ_PIECES)):
        _project_piece(h, wt_ref, p_ref, rows, piece)


def _head_sum(x, j_ref):
    hi = x.astype(bf16)
    lo = (x - hi.astype(f32)).astype(bf16)
    return _dot(hi, j_ref[...]) + _dot(lo, j_ref[...])


def _rope(x, cos, sin_signed):
    lane = lax.broadcasted_iota(jnp.int32, x.shape, 1)
    swapped = jnp.where((lane % HEAD_D) < HEAD_D // 2,
                        pltpu.roll(x, LANES - HEAD_D // 2, 1), pltpu.roll(x, HEAD_D // 2, 1))
    return x * cos + swapped * sin_signed


def _stack_heads(x):
    lane_head = lax.broadcasted_iota(jnp.int32, x.shape, 1) // HEAD_D
    return jnp.concatenate([jnp.where(lane_head == h, x, 0.0) for h in range(N_HEADS)], axis=0)


def _unstack_heads(o, n):
    lane_head = lax.broadcasted_iota(jnp.int32, (n, GROUP_W), 1) // HEAD_D
    out = o[0:n]
    for h in range(1, N_HEADS):
        out = jnp.where(lane_head == h, o[h * n:(h + 1) * n], out)
    return out


def _stack_swa_q(q01, q23):
    low = lax.broadcasted_iota(jnp.int32, q01.shape, 1) < HEAD_D
    return jnp.concatenate([jnp.where(low, q01, 0.0), jnp.where(low, pltpu.roll(q01, HEAD_D, 1), 0.0),
                            jnp.where(low, 0.0, pltpu.roll(q23, HEAD_D, 1)), jnp.where(low, 0.0, q23)], axis=0)


def _unstack_swa(o, n):
    low = lax.broadcasted_iota(jnp.int32, (n, SWA_KV_W), 1) < HEAD_D
    c01 = jnp.where(low, o[0:n], pltpu.roll(o[n:2 * n], HEAD_D, 1))
    c23 = jnp.where(low, pltpu.roll(o[2 * n:3 * n], HEAD_D, 1), o[3 * n:4 * n])
    return jnp.concatenate([c01, c23], axis=1)


def _sink_column(sinks, n):
    return jnp.concatenate([jnp.full((n, 1), s, f32) for s in sinks], axis=0)


def _block_cumsum(la, block):
    row = lax.broadcasted_iota(jnp.int32, la.shape, 0) % block
    b = la
    s = 1
    while s < block:
        b = b + jnp.where(row >= s, pltpu.roll(b, s, 0), 0.0)
        s *= 2
    return b


def _gate_log_decay(p_lr, wg_ref, bg_ref):
    pre = _dot(p_lr.astype(bf16), wg_ref[...]) + bg_ref[...]
    return _log_sigmoid(pre) * (1.0 / GATE_NORM)


def _gla_norm_gate(o, gz, gnorm_ref, j_ref):
    ms = _head_sum(o * o, j_ref) * (1.0 / HEAD_D)
    return o * lax.rsqrt(ms + EPS) * gnorm_ref[...] * _silu(gz)


GROUPS = SUB // SUBLANES
LEVEL_GROUPS = (1, 2, 4)


def _row_groups(x):
    return [x[SUBLANES * r:SUBLANES * (r + 1), :] for r in range(x.shape[0] // SUBLANES)]


def _prompt_layer_kernel(nt, *refs):
    s = pl.program_id(0)
    pl.when(s == 0)(functools.partial(_prompt_first_step, *refs))
    for parity in range(2):
        pl.when((s > 0) & (s % 2 == parity))(functools.partial(_prompt_layer_step, nt, parity, *refs))


def _prompt_first_step(sinks_ref, x_ref, memp_ref, cos_ref, sin_ref, gpre_ref, gpost_ref, wt_ref,
                       wout_ref, wmem_ref, convw_ref, wg_ref, bg_ref, gnorm_ref, j_ref, lvl_ref, swab_ref,
                       y_ref, conv_out, gla_out, swak_out, swav_out, mk_out, mv_out,
                       p_buf, x_prev, woutb, *unused):
    x = x_ref[0]
    _project(_rmsnorm(x, gpre_ref[...]).astype(bf16), wt_ref, p_buf.at[0], slice(None))
    x_prev[...] = x
    woutb[...] = wout_ref[...].astype(bf16)


def _prompt_layer_step(nt, parity, sinks_ref, x_ref, memp_ref, cos_ref, sin_ref, gpre_ref, gpost_ref, wt_ref,
                       wout_ref, wmem_ref, convw_ref, wg_ref, bg_ref, gnorm_ref, j_ref, lvl_ref, swab_ref,
                       y_ref, conv_out, gla_out, swak_out, swav_out, mk_out, mv_out,
                       p_buf, x_prev, woutb, mix_scr, cbuf, st_scr, kprev, vprev, vtprev, mkb, mvtb, kv_nat):
    s = pl.program_id(0)
    t = jnp.maximum(s - 1, 0) % nt
    nsub = TL // SUB
    p_wr = p_buf.at[parity]
    p_scr = p_buf.at[1 - parity]

    @pl.when(t == 0)
    def _new_sequence():
        cbuf[...] = jnp.zeros((SUBLANES, GROUP_W), f32)
        st_scr[...] = jnp.zeros((GROUP_W, GROUP_W), f32)
        kprev[...] = jnp.zeros((SUB, SWA_KV_W), f32)
        vprev[...] = jnp.zeros((SUB, SWA_KV_W), f32)
        vtprev[...] = jnp.zeros((SWA_KV_W, SUB), f32)
        mkv = _dot(memp_ref[0].astype(bf16), wmem_ref[...].astype(bf16))
        mvt = mkv[:, GROUP_W:2 * GROUP_W].T
        mk_out[0] = mkv[:, 0:GROUP_W].T
        mv_out[0] = mvt
        mkb[...] = mkv[:, 0:GROUP_W].astype(bf16)
        mvtb[...] = mvt.astype(bf16)

    x = x_ref[0]
    h = _rmsnorm(x, gpre_ref[...]).astype(bf16)

    sub8 = lax.broadcasted_iota(jnp.int32, (SUBLANES, GROUP_W), 0)
    pending = list(range(len(PROJ_PIECES)))

    def project_next():
        if pending:
            _project_piece(h, wt_ref, p_wr, slice(None), pending.pop(0))

    def sub_tile(j):
        rows = slice(j * SUB, (j + 1) * SUB)
        project_next()

        u = p_scr[rows, C_CC:C_CC + GROUP_W] * p_scr[rows, C_CX:C_CX + GROUP_W]
        last = SUB - SUBLANES
        prev1 = jnp.where(sub8 == 0, cbuf[0:1, :], pltpu.roll(u[last:SUB, :], 1, 0))
        prev2 = jnp.where(sub8 == 0, cbuf[1:2, :], pltpu.roll(u[last - SUBLANES:last, :], 1, 0))
        um1 = jnp.concatenate([prev1, u[0:last, :]], axis=0)
        um2 = jnp.concatenate([prev2, prev1, u[0:last - SUBLANES, :]], axis=0)
        cy = convw_ref[0:1, :] * um2 + convw_ref[1:2, :] * um1 + convw_ref[2:3, :] * u
        cbuf[0:1, :] = u[SUB - 1:SUB, :]
        cbuf[1:2, :] = u[last - 1:last, :]
        a_out = p_scr[rows, C_CB:C_CB + GROUP_W] * cy * _silu(p_scr[rows, C_CZ:C_CZ + GROUP_W])
        mix_scr[rows, 0:GROUP_W] = a_out.astype(bf16)

        qg = _row_groups(p_scr[rows, C_GQ:C_GQ + GROUP_W] * QK_SCALE)
        k = p_scr[rows, C_GK:C_GK + GROUP_W]
        v = p_scr[rows, C_GV:C_GV + GROUP_W]
        kg = _row_groups(k)
        vg = _row_groups(v)
        lag = _row_groups(_gate_log_decay(p_scr[rows, C_LR:C_LR + GATE_RANK], wg_ref, bg_ref))
        bg_ = [lag[0]]
        for r in range(1, GROUPS):
            bg_.append(bg_[-1] + lag[r])
        tot = bg_[GROUPS - 1]
        pw = []
        for r in range(GROUPS):
            for s in range(r):
                pw.append(qg[r] * kg[s] * jnp.exp(bg_[r] - bg_[s]))
            pw.append(qg[r] * kg[r])
        scores = _dot(jnp.concatenate(pw, axis=0).astype(bf16), j_ref[...])
        project_next()
        og = []
        idx = 0
        for r in range(GROUPS):
            acc = None
            for s in range(r + 1):
                term = scores[SUBLANES * idx:SUBLANES * (idx + 1), :] * vg[s]
                acc = term if acc is None else acc + term
                idx += 1
            og.append(acc)
        o = jnp.concatenate(og, axis=0)

        def decayed(group):
            before = jnp.zeros((SUBLANES, GROUP_W), f32)
            after = jnp.zeros((SUBLANES, GROUP_W), f32)
            for s in range(1, group):
                before = before + jnp.where(sub8 % group >= s, pltpu.roll(tot, s, 0), 0.0)
                after = after + jnp.where(sub8 % group < group - s, pltpu.roll(tot, SUBLANES - s, 0), 0.0)
            qd = jnp.concatenate([qg[r] * jnp.exp(bg_[r] + before) for r in range(GROUPS)], axis=0)
            kd = jnp.concatenate([kg[r] * jnp.exp((tot - bg_[r]) + after) for r in range(GROUPS)], axis=0)
            return qd, kd

        attn = None
        for li, group in enumerate(LEVEL_GROUPS):
            qd, kd = decayed(group)
            s = _dot_nt(_stack_heads(qd).astype(bf16), kd.astype(bf16)) * lvl_ref[li]
            attn = s if attn is None else attn + s
        o = o + _unstack_heads(_dot(attn.astype(bf16), v.astype(bf16)), SUB)
        project_next()
        qd, kd = decayed(SUB // GLA_BLOCK)
        st = st_scr[...]
        o = o + _dot_nt(qd.astype(bf16), st.astype(bf16))
        total = jnp.sum(tot, axis=0, keepdims=True)
        upd = _dot_tn(v.astype(bf16), kd.astype(bf16))
        same_head = (lax.broadcasted_iota(jnp.int32, (GROUP_W, GROUP_W), 0) // HEAD_D
                     == lax.broadcasted_iota(jnp.int32, (GROUP_W, GROUP_W), 1) // HEAD_D)
        st_scr[...] = st * jnp.exp(total) + jnp.where(same_head, upd, 0.0)
        b_out = _gla_norm_gate(o, p_scr[rows, C_GZ:C_GZ + GROUP_W], gnorm_ref, j_ref)
        mix_scr[rows, GROUP_W:2 * GROUP_W] = b_out.astype(bf16)

        cos = cos_ref[rows, :]
        sin = sin_ref[rows, :]
        q01 = _rope(p_scr[rows, C_SQ:C_SQ + LANES], cos, sin) * QK_SCALE
        q23 = _rope(p_scr[rows, C_SQ + LANES:C_SQ + 2 * LANES], cos, sin) * QK_SCALE
        kr = _rope(p_scr[rows, C_SK:C_SK + SWA_KV_W], cos, sin)
        vx = p_scr[rows, C_SV:C_SV + SWA_KV_W]
        k_all = jnp.concatenate([kprev[...], kr], axis=0).astype(bf16)
        vt_new = vx.T
        vt_all = jnp.concatenate([vtprev[...], vt_new], axis=1).astype(bf16)
        first = jnp.where((t * nsub + j) == 0, 1, 0)
        sc = _dot_nt(k_all, _stack_swa_q(q01, q23).astype(bf16)) + swab_ref[first]
        lane_head = lax.broadcasted_iota(jnp.int32, (1, N_HEADS * SUB), 1) // SUB
        sink = jnp.full((1, N_HEADS * SUB), sinks_ref[0], f32)
        for hh in range(1, N_HEADS):
            sink = jnp.where(lane_head == hh, sinks_ref[hh], sink)
        mx = jnp.maximum(jnp.max(sc, axis=0, keepdims=True), sink)
        e = jnp.exp(sc - mx)
        den = jnp.sum(e, axis=0, keepdims=True) + jnp.exp(sink - mx)
        ot = _dot(vt_all, e.astype(bf16)) * (1.0 / den)
        oc = _unstack_swa(jnp.concatenate([ot[:, SUB * hh:SUB * (hh + 1)].T for hh in range(N_HEADS)], axis=0), SUB)
        mix_scr[rows, 2 * GROUP_W:3 * GROUP_W] = (oc * _silu(p_scr[rows, C_SZ:C_SZ + GROUP_W])).astype(bf16)
        kprev[...] = kr
        vprev[...] = vx
        vtprev[...] = vt_new
        project_next()

        qm = _stack_heads(p_scr[rows, C_MQ:C_MQ + GROUP_W] * QK_SCALE).astype(bf16)
        sm = _dot_nt(mkb[...], qm)
        mm = jnp.max(sm, axis=0, keepdims=True)
        em = jnp.exp(sm - mm)
        dm = jnp.sum(em, axis=0, keepdims=True)
        odt = _dot(mvtb[...], em.astype(bf16)) * (1.0 / dm)
        low = lax.broadcasted_iota(jnp.int32, (SUB, LANES), 1) < HEAD_D
        halves = []
        for pair in range(N_HEADS // 2):
            blk_rows = odt[LANES * pair:LANES * (pair + 1), :]
            even = blk_rows[:, SUB * (2 * pair):SUB * (2 * pair + 1)].T
            odd = blk_rows[:, SUB * (2 * pair + 1):SUB * (2 * pair + 2)].T
            halves.append(jnp.where(low, even, odd))
        od = jnp.concatenate(halves, axis=1)
        mix_scr[rows, 3 * GROUP_W:4 * GROUP_W] = (od * _silu(p_scr[rows, C_MZ:C_MZ + GROUP_W])).astype(bf16)

    for j in range(nsub):
        sub_tile(j)
    while pending:
        project_next()

    mo = _dot(mix_scr[...], woutb[...])
    y_ref[0] = x_prev[...] + _rmsnorm(mo, gpost_ref[...])
    x_prev[...] = x

    @pl.when(t == nt - 1)
    def _sequence_done():
        conv_out[0, 0:1, :] = cbuf[1:2, :]
        conv_out[0, 1:2, :] = cbuf[0:1, :]
        for src, dst in ((kprev, swak_out), (vprev, swav_out)):
            for r in range(GROUPS):
                kv_nat[pl.ds(r, SUBLANES, stride=GLA_BLOCK), :] = src[SUBLANES * r:SUBLANES * (r + 1), :]
            dst[0] = kv_nat[...].T
        st = st_scr[...]
        for hh in range(N_HEADS):
            gla_out[0, hh] = st[HEAD_D * hh:HEAD_D * (hh + 1), HEAD_D * hh:HEAD_D * (hh + 1)].T


def _layer_spec(shape, l):
    nd = len(shape)
    return pl.BlockSpec((None,) + tuple(shape), lambda s, _l=l, _nd=nd: (_l,) + (0,) * _nd)


def _const_spec(shape):
    nd = len(shape)
    return pl.BlockSpec(shape, lambda s, _nd=nd: (0,) * _nd)


def _prompt_layer(l, x, memp, cos, sin, gpre, gpost, wt, wout, wmem, convw, wg, bg, gnorm, sinks, jmat,
                  lvl, swab):
    B, L, _ = x.shape
    assert L % TL == 0
    nt = L // TL
    n_tiles = B * nt

    def projected(s):
        tile = jnp.minimum(s, n_tiles - 1)
        return tile // nt, tile % nt

    def mixed(s):
        tile = jnp.maximum(s - 1, 0)
        return tile // nt, tile % nt

    def per_sequence(shape):
        nd = len(shape) - 1
        return pl.BlockSpec(shape, lambda s, _nd=nd: (mixed(s)[0],) + (0,) * _nd)

    out_shape = (
        jax.ShapeDtypeStruct((B, L, D_MODEL), f32),
        jax.ShapeDtypeStruct((B, CONV_W - 1, GROUP_W), f32),
        jax.ShapeDtypeStruct((B, N_HEADS, HEAD_D, HEAD_D), f32),
        jax.ShapeDtypeStruct((B, SWA_KV_W, WINDOW), f32),
        jax.ShapeDtypeStruct((B, SWA_KV_W, WINDOW), f32),
        jax.ShapeDtypeStruct((B, GROUP_W, N_MEM), f32),
        jax.ShapeDtypeStruct((B, GROUP_W, N_MEM), f32),
    )
    in_specs = [
        pl.BlockSpec(memory_space=pltpu.SMEM),
        pl.BlockSpec((1, TL, D_MODEL), lambda s: projected(s) + (0,)),
        per_sequence((1, N_MEM, D_MODEL)),
        pl.BlockSpec((TL, LANES), lambda s: (mixed(s)[1], 0)),
        pl.BlockSpec((TL, LANES), lambda s: (mixed(s)[1], 0)),
        _layer_spec((1, D_MODEL), l), _layer_spec((1, D_MODEL), l),
        _layer_spec((IN_WIDTH, D_MODEL), l),
        _layer_spec((D_MODEL, D_MODEL), l), _layer_spec((D_MODEL, 2 * GROUP_W), l),
        _layer_spec((CONV_W, GROUP_W), l), _layer_spec((GATE_RANK, GROUP_W), l), _layer_spec((1, GROUP_W), l),
        _layer_spec((1, GROUP_W), l), _const_spec((GROUP_W, GROUP_W)),
        _const_spec((len(LEVEL_GROUPS), N_HEADS * SUB, SUB)), _const_spec((2, 2 * SUB, N_HEADS * SUB)),
    ]
    out_specs = (
        pl.BlockSpec((1, TL, D_MODEL), lambda s: mixed(s) + (0,)),
        per_sequence((1, CONV_W - 1, GROUP_W)),
        per_sequence((1, N_HEADS, HEAD_D, HEAD_D)),
        per_sequence((1, SWA_KV_W, WINDOW)),
        per_sequence((1, SWA_KV_W, WINDOW)),
        per_sequence((1, GROUP_W, N_MEM)),
        per_sequence((1, GROUP_W, N_MEM)),
    )
    scratch = [
        pltpu.VMEM((2, TL, NP), f32),
        pltpu.VMEM((TL, D_MODEL), f32),
        pltpu.VMEM((D_MODEL, D_MODEL), bf16),
        pltpu.VMEM((TL, D_MODEL), bf16),
        pltpu.VMEM((SUBLANES, GROUP_W), f32),
        pltpu.VMEM((GROUP_W, GROUP_W), f32),
        pltpu.VMEM((SUB, SWA_KV_W), f32),
        pltpu.VMEM((SUB, SWA_KV_W), f32),
        pltpu.VMEM((SWA_KV_W, SUB), f32),
        pltpu.VMEM((N_MEM, GROUP_W), bf16),
        pltpu.VMEM((GROUP_W, N_MEM), bf16),
        pltpu.VMEM((WINDOW, SWA_KV_W), f32),
    ]
    return pl.pallas_call(
        functools.partial(_prompt_layer_kernel, nt),
        grid=(n_tiles + 1,),
        in_specs=in_specs,
        out_specs=out_specs,
        out_shape=out_shape,
        scratch_shapes=scratch,
        compiler_params=pltpu.CompilerParams(dimension_semantics=("arbitrary",),
                                             vmem_limit_bytes=VMEM_LIMIT_BYTES),
        name="prompt_layer",
    )(sinks[l], x, memp, cos, sin, gpre, gpost, wt, wout, wmem, convw, wg, bg, gnorm, jmat, lvl, swab)


def _subtile_constants():
    row = np.arange(SUB)
    tok = (row % SUBLANES) * GLA_BLOCK + row // SUBLANES
    blk = row % SUBLANES
    levels = []
    for group in LEVEL_GROUPS:
        g = blk // group
        sel = (g[:, None] == g[None, :] + 1) & (g[:, None] % 2 == 1)
        levels.append(np.tile(sel, (N_HEADS, 1)))
    key_tok = np.concatenate([tok - SUB, tok])
    valid = (key_tok[None, :] <= tok[:, None]) & (key_tok[None, :] > tok[:, None] - WINDOW)
    bias = [np.where(valid, 0.0, NEG), np.where(valid & (key_tok[None, :] >= 0), 0.0, NEG)]
    bias = np.stack([np.tile(b, (N_HEADS, 1)).T for b in bias])
    return jnp.asarray(np.stack(levels), dtype=f32), jnp.asarray(bias, dtype=f32)


def _to_kernel_order(x, axis):
    shp = x.shape
    n = shp[axis]
    x = x.reshape(shp[:axis] + (n // SUB, SUBLANES, GLA_BLOCK) + shp[axis + 1:])
    return jnp.swapaxes(x, axis + 1, axis + 2).reshape(shp)


def _from_kernel_order(x, axis):
    shp = x.shape
    n = shp[axis]
    x = x.reshape(shp[:axis] + (n // SUB, GLA_BLOCK, SUBLANES) + shp[axis + 1:])
    return jnp.swapaxes(x, axis + 1, axis + 2).reshape(shp)


def _sample_kernel(seq_len, group, sinks_ref, x_hbm, cos_ref, sin_ref, convb_ref, sgla_ref, kc_ref, vc_ref,
                   mkc_ref, mvc_ref, gpre_ref, gpost_ref, wt_ref, wout_ref, convw_ref, wg_ref, bg_ref,
                   gnorm_ref, j_ref, place_ref,
                   y_hbm, conv_out, gla_out, swak_out, swav_out,
                   hs, p_scr, mix_scr, woutb, sem):
    l = pl.program_id(0)
    g_step = pl.program_id(1)
    n_layers = pl.num_programs(0)
    n_steps = pl.num_programs(1)
    n_tok = hs.shape[0]
    n = group * seq_len
    chunks = [slice(c * TL, (c + 1) * TL) for c in range(n_tok // TL)]

    @pl.when((l == 0) & (g_step == 0))
    def _load():
        cp = pltpu.make_async_copy(x_hbm, hs, sem.at[0])
        cp.start()
        cp.wait()

    @pl.when(g_step == 0)
    def _project_all():
        woutb[...] = wout_ref[...].astype(bf16)
        for rows in chunks:
            h = _rmsnorm(hs[rows, :], gpre_ref[...]).astype(bf16)
            _project(h, wt_ref, p_scr, rows)

    rows = pl.ds(pl.multiple_of(g_step * n, n), n)
    row_t = lax.broadcasted_iota(jnp.int32, (n, GROUP_W), 0) % seq_len

    u = p_scr[rows, C_CC:C_CC + GROUP_W] * p_scr[rows, C_CX:C_CX + GROUP_W]
    hist = convb_ref[...]
    um1 = jnp.where(row_t >= 1, pltpu.roll(u, 1, 0), pltpu.roll(hist, n - 1, 0))
    um2 = jnp.where(row_t >= 2, pltpu.roll(u, 2, 0), hist)
    cy = convw_ref[0:1, :] * um2 + convw_ref[1:2, :] * um1 + convw_ref[2:3, :] * u
    a_out = p_scr[rows, C_CB:C_CB + GROUP_W] * cy * _silu(p_scr[rows, C_CZ:C_CZ + GROUP_W])
    mix_scr[rows, 0:GROUP_W] = a_out.astype(bf16)
    for g in range(group):
        conv_out[g] = u[g * seq_len + seq_len - (CONV_W - 1):(g + 1) * seq_len, :]

    qs = p_scr[rows, C_GQ:C_GQ + GROUP_W] * QK_SCALE
    k = p_scr[rows, C_GK:C_GK + GROUP_W]
    v = p_scr[rows, C_GV:C_GV + GROUP_W]
    la = _gate_log_decay(p_scr[rows, C_LR:C_LR + GATE_RANK], wg_ref, bg_ref)
    bc = _block_cumsum(la, seq_len)
    o = _dot((qs * k).astype(bf16), j_ref[...]) * v
    for d in range(1, seq_len):
        dec = jnp.exp(jnp.where(row_t >= d, bc - pltpu.roll(bc, d, 0), NEG))
        pw = qs * pltpu.roll(k, d, 0) * dec
        o = o + _dot(pw.astype(bf16), j_ref[...]) * pltpu.roll(v, d, 0)
    qd = qs * jnp.exp(bc)
    tot = [bc[(g + 1) * seq_len - 1:(g + 1) * seq_len, :] for g in range(group)]
    kd = k * jnp.exp(jnp.concatenate(
        [tot[g] - bc[g * seq_len:(g + 1) * seq_len, :] for g in range(group)], axis=0))
    o_inter = []
    for g in range(group):
        rs = slice(g * seq_len, (g + 1) * seq_len)
        alpha = jnp.exp(tot[g])
        per_head = []
        for hh in range(N_HEADS):
            ls = slice(hh * HEAD_D, (hh + 1) * HEAD_D)
            st0 = sgla_ref[g, hh]
            per_head.append(_dot_nt(qd[rs, ls].astype(bf16), st0.astype(bf16)))
            gla_out[g, hh] = st0 * alpha[:, ls] + _dot_tn(v[rs, ls].astype(bf16), kd[rs, ls].astype(bf16))
        o_inter.append(jnp.concatenate(per_head, axis=1))
    o = o + jnp.concatenate(o_inter, axis=0)
    b_out = _gla_norm_gate(o, p_scr[rows, C_GZ:C_GZ + GROUP_W], gnorm_ref, j_ref)
    mix_scr[rows, GROUP_W:2 * GROUP_W] = b_out.astype(bf16)

    cos = cos_ref[...]
    sin = sin_ref[...]
    q01 = _rope(p_scr[rows, C_SQ:C_SQ + LANES], cos, sin) * QK_SCALE
    q23 = _rope(p_scr[rows, C_SQ + LANES:C_SQ + 2 * LANES], cos, sin) * QK_SCALE
    kr = _rope(p_scr[rows, C_SK:C_SK + SWA_KV_W], cos, sin)
    vx = p_scr[rows, C_SV:C_SV + SWA_KV_W]
    qm = p_scr[rows, C_MQ:C_MQ + GROUP_W] * QK_SCALE
    hq = N_HEADS * seq_len
    sink_seq = _sink_column([sinks_ref[l, hh] for hh in range(N_HEADS)], seq_len)
    sink = jnp.concatenate([sink_seq] * group, axis=0)
    hq_all = group * hq
    qrow = lax.broadcasted_iota(jnp.int32, (hq_all, WINDOW), 0) % seq_len
    ccol = lax.broadcasted_iota(jnp.int32, (hq_all, WINDOW), 1)
    cache_valid = ccol > qrow
    nrow = lax.broadcasted_iota(jnp.int32, (hq_all, seq_len), 0) % seq_len
    ncol = lax.broadcasted_iota(jnp.int32, (hq_all, seq_len), 1)
    new_valid = ncol <= nrow
    keep_old = lax.broadcasted_iota(jnp.int32, (SWA_KV_W, WINDOW), 1) < WINDOW - seq_len

    def exact_split(a):
        hi = a.astype(bf16)
        r1 = a - hi.astype(f32)
        mid = r1.astype(bf16)
        lo = (r1 - mid.astype(f32)).astype(bf16)
        return jnp.concatenate([hi, mid, lo], axis=0)

    def shifted_cache(old_t, new_rows):
        placed = _dot_tn(exact_split(new_rows), place_ref[...])
        return jnp.where(keep_old, pltpu.roll(old_t, WINDOW - seq_len, 1), placed)

    seqs = [slice(g * seq_len, (g + 1) * seq_len) for g in range(group)]
    stk = [slice(g * hq, (g + 1) * hq) for g in range(group)]
    qst = [_stack_swa_q(q01[rs], q23[rs]).astype(bf16) for rs in seqs]
    s_c = jnp.concatenate([_dot(qst[g], kc_ref[g].astype(bf16)) for g in range(group)], axis=0)
    s_n = jnp.concatenate([_dot_nt(qst[g], kr[seqs[g]].astype(bf16)) for g in range(group)], axis=0)
    s_c = jnp.where(cache_valid, s_c, NEG)
    s_n = jnp.where(new_valid, s_n, NEG)
    mx = jnp.maximum(jnp.maximum(jnp.max(s_c, axis=1, keepdims=True), jnp.max(s_n, axis=1, keepdims=True)), sink)
    e_c = jnp.exp(s_c - mx)
    e_n = jnp.exp(s_n - mx)
    den = jnp.sum(e_c, axis=1, keepdims=True) + jnp.sum(e_n, axis=1, keepdims=True) + jnp.exp(sink - mx)
    e_c = e_c.astype(bf16)
    e_n = e_n.astype(bf16)
    ov = jnp.concatenate([_dot_nt(e_c[stk[g]], vc_ref[g].astype(bf16)) + _dot(e_n[stk[g]], vx[seqs[g]].astype(bf16))
                          for g in range(group)], axis=0) * (1.0 / den)
    oc = jnp.concatenate([_unstack_swa(ov[stk[g]], seq_len) for g in range(group)], axis=0)
    for g in range(group):
        swak_out[g] = shifted_cache(kc_ref[g], kr[seqs[g]])
        swav_out[g] = shifted_cache(vc_ref[g], vx[seqs[g]])

    sm = jnp.concatenate([_dot(_stack_heads(qm[seqs[g]]).astype(bf16), mkc_ref[g].astype(bf16))
                          for g in range(group)], axis=0)
    mm = jnp.max(sm, axis=1, keepdims=True)
    em = jnp.exp(sm - mm)
    dm = jnp.sum(em, axis=1, keepdims=True)
    em = em.astype(bf16)
    odv = jnp.concatenate([_dot_nt(em[stk[g]], mvc_ref[g].astype(bf16)) for g in range(group)], axis=0) * (1.0 / dm)
    od = jnp.concatenate([_unstack_heads(odv[stk[g]], seq_len) for g in range(group)], axis=0)
    mix_scr[rows, 2 * GROUP_W:3 * GROUP_W] = (oc * _silu(p_scr[rows, C_SZ:C_SZ + GROUP_W])).astype(bf16)
    mix_scr[rows, 3 * GROUP_W:4 * GROUP_W] = (od * _silu(p_scr[rows, C_MZ:C_MZ + GROUP_W])).astype(bf16)

    @pl.when(g_step == n_steps - 1)
    def _residual():
        for rws in chunks:
            mo = _dot(mix_scr[rws, :], woutb[...])
            hs[rws, :] = hs[rws, :] + _rmsnorm(mo, gpost_ref[...])

    @pl.when((l == n_layers - 1) & (g_step == n_steps - 1))
    def _store():
        cp = pltpu.make_async_copy(hs, y_hbm, sem.at[0])
        cp.start()
        cp.wait()


def _sample_layers(x, cos, sin, convb, sgla, kc, vc, mkc, mvc, gpre, gpost, wt, wout, convw, wg, bg, gnorm,
                   sinks, jmat, place, seq_len, group):
    depth, nseq = sgla.shape[0], sgla.shape[1]
    n_tok = nseq * seq_len
    assert nseq % group == 0 and seq_len == SUBLANES and n_tok % TL == 0
    n = group * seq_len

    def per_layer(shape, single_buffer=False):
        nd = len(shape)
        kw = dict(pipeline_mode=pl.Buffered(1)) if single_buffer else {}
        return pl.BlockSpec((None,) + tuple(shape), lambda l, g, _nd=nd: (l,) + (0,) * _nd, **kw)

    def per_group(shape):
        nd = len(shape) - 1
        return pl.BlockSpec((None,) + tuple(shape), lambda l, g, _nd=nd: (l, g) + (0,) * _nd)

    def const(shape):
        nd = len(shape)
        return pl.BlockSpec(shape, lambda l, g, _nd=nd: (0,) * _nd)

    in_specs = [
        pl.BlockSpec(memory_space=pltpu.SMEM),
        pl.BlockSpec(memory_space=pl.ANY),
        const((n, LANES)), const((n, LANES)),
        per_group((n, GROUP_W)),
        per_group((group, N_HEADS, HEAD_D, HEAD_D)),
        per_group((group, SWA_KV_W, WINDOW)), per_group((group, SWA_KV_W, WINDOW)),
        per_group((group, GROUP_W, N_MEM)), per_group((group, GROUP_W, N_MEM)),
        per_layer((1, D_MODEL)), per_layer((1, D_MODEL)),
        per_layer((IN_WIDTH, D_MODEL), True),
        per_layer((D_MODEL, D_MODEL), True),
        per_layer((CONV_W, GROUP_W)), per_layer((GATE_RANK, GROUP_W)), per_layer((1, GROUP_W)), per_layer((1, GROUP_W)),
        const((GROUP_W, GROUP_W)), const((3 * seq_len, WINDOW)),
    ]
    out_shape = (
        jax.ShapeDtypeStruct((n_tok, D_MODEL), f32),
        jax.ShapeDtypeStruct((depth, nseq, CONV_W - 1, GROUP_W), f32),
        jax.ShapeDtypeStruct((depth, nseq, N_HEADS, HEAD_D, HEAD_D), f32),
        jax.ShapeDtypeStruct((depth, nseq, SWA_KV_W, WINDOW), f32),
        jax.ShapeDtypeStruct((depth, nseq, SWA_KV_W, WINDOW), f32),
    )
    out_specs = (
        pl.BlockSpec(memory_space=pl.ANY),
        per_group((group, CONV_W - 1, GROUP_W)),
        per_group((group, N_HEADS, HEAD_D, HEAD_D)),
        per_group((group, SWA_KV_W, WINDOW)), per_group((group, SWA_KV_W, WINDOW)),
    )
    scratch = [
        pltpu.VMEM((n_tok, D_MODEL), f32),
        pltpu.VMEM((n_tok, NP), f32),
        pltpu.VMEM((n_tok, D_MODEL), bf16),
        pltpu.VMEM((D_MODEL, D_MODEL), bf16),
        pltpu.SemaphoreType.DMA((1,)),
    ]
    return pl.pallas_call(
        functools.partial(_sample_kernel, seq_len, group),
        grid=(depth, nseq // group),
        in_specs=in_specs,
        out_specs=out_specs,
        out_shape=out_shape,
        scratch_shapes=scratch,
        compiler_params=pltpu.CompilerParams(dimension_semantics=("arbitrary", "arbitrary"),
                                             vmem_limit_bytes=VMEM_LIMIT_BYTES),
        name="sample_layers",
    )(sinks, x, cos, sin, convb, sgla, kc, vc, mkc, mvc, gpre, gpost, wt, wout, convw, wg, bg, gnorm,
      jmat, place)


def _rope_tables(pos):
    half = HEAD_D // 2
    inv = np.power(ROPE_THETA, -np.arange(half, dtype=np.float64) / half)
    ang = np.asarray(pos, np.float64)[:, None] * inv[None, :]
    cos = np.tile(np.cos(ang), (1, LANES // half))
    sin = np.sin(ang)
    sin_signed = np.tile(np.concatenate([-sin, sin], axis=1), (1, LANES // HEAD_D))
    return cos.astype(np.float32), sin_signed.astype(np.float32)


def _rows_to_kernel_order(a):
    n, w = a.shape
    return a.reshape(n // SUB, SUBLANES, GLA_BLOCK, w).swapaxes(1, 2).reshape(n, w)


def _feature_major(cache):
    d, s, p, h, e = cache.shape
    return jnp.transpose(cache, (0, 1, 3, 4, 2)).reshape(d, s, h * e, p)


def _position_major(cache_t, heads):
    d, s, he, p = cache_t.shape
    return jnp.transpose(cache_t.reshape(d, s, heads, he // heads, p), (0, 1, 4, 2, 3))


def kernel(x_prompt, x_sample, state_conv, state_gla, cache_swa_k, cache_swa_v, cache_mem_k, cache_mem_v,
           mem_prompt, norm_pre, norm_post, w_in, conv_w, gla_w_gate, gla_b_gate, gla_norm, swa_sinks,
           w_mem_kv, w_out):
    depth = w_in.shape[0]
    B, L, _ = x_prompt.shape
    nseq, seq_len, _ = x_sample.shape
    group = 8

    wt = jnp.swapaxes(w_in, 1, 2).astype(bf16)
    wout = w_out
    wmem = w_mem_kv
    wg = gla_w_gate.astype(bf16)
    head_id = np.arange(GROUP_W) // HEAD_D
    jmat = jnp.asarray(head_id[:, None] == head_id[None, :], dtype=bf16)
    lvl, swab = _subtile_constants()
    place = np.zeros((3 * seq_len, WINDOW), np.float32)
    for piece in range(3):
        place[piece * seq_len + np.arange(seq_len), WINDOW - seq_len + np.arange(seq_len)] = 1.0
    place = jnp.asarray(place, dtype=bf16)
    sinks = swa_sinks.astype(f32)
    gpre = norm_pre[:, None, :]
    gpost = norm_post[:, None, :]
    bg = gla_b_gate[:, None, :]
    gn = gla_norm[:, None, :]

    cos_p, sin_p = (jnp.asarray(_rows_to_kernel_order(a)) for a in _rope_tables(np.arange(L)))
    cos_s, sin_s = (jnp.asarray(np.tile(a, (group, 1))) for a in _rope_tables(PAST_LEN + np.arange(seq_len)))

    convb = jnp.pad(state_conv, ((0, 0), (0, 0), (0, seq_len - (CONV_W - 1)), (0, 0)))
    convb = convb.reshape(depth, nseq * seq_len, GROUP_W)
    ys, conv_s, gla_s, swak_s, swav_s = _sample_layers(
        x_sample.reshape(nseq * seq_len, D_MODEL), cos_s, sin_s, convb, jnp.swapaxes(state_gla, -1, -2),
        _feature_major(cache_swa_k), _feature_major(cache_swa_v),
        _feature_major(cache_mem_k), _feature_major(cache_mem_v),
        gpre, gpost, wt, wout, conv_w, wg, bg, gn, sinks, jmat, place, seq_len, group)

    hp = _to_kernel_order(x_prompt, 1)
    outs_p = [[] for _ in range(6)]
    for l in range(depth):
        res = _prompt_layer(l, hp, mem_prompt, cos_p, sin_p, gpre, gpost, wt, wout, wmem, conv_w, wg, bg, gn,
                            sinks, jmat, lvl, swab)
        hp = res[0]
        for i in range(6):
            outs_p[i].append(res[i + 1])

    return (_from_kernel_order(hp, 1), ys.reshape(nseq, seq_len, D_MODEL),
            jnp.stack(outs_p[0]), jnp.stack(outs_p[1]),
            _position_major(jnp.stack(outs_p[2]), 2), _position_major(jnp.stack(outs_p[3]), 2),
            _position_major(jnp.stack(outs_p[4]), N_HEADS), _position_major(jnp.stack(outs_p[5]), N_HEADS),
            conv_s, jnp.swapaxes(gla_s, -1, -2), _position_major(swak_s, 2), _position_major(swav_s, 2))
```

```python
import functools

import jax
import jax.numpy as jnp
import numpy as np
from jax import lax
from jax.experimental import pallas as pl
from jax.experimental.pallas import tpu as pltpu

f32 = jnp.float32
bf16 = jnp.bfloat16

D_MODEL = 1024
GROUP_W = 256
HEAD_D = 64
N_HEADS = 4
SWA_KV_W = 128
N_MEM = 256
WINDOW = 128
CONV_W = 3
GATE_RANK = 16
GATE_NORM = 16.0
GLA_BLOCK = 16
ROPE_THETA = 10000.0
PAST_LEN = 8192
EPS = 1e-6
NEG = -1e30
QK_SCALE = HEAD_D ** -0.5

LANES = 128
SUBLANES = 8
VMEM_LIMIT_BYTES = 56 * 1024 * 1024

C_CX, C_CB, C_CC, C_CZ = 0, 256, 512, 768
C_GQ, C_GK, C_GV, C_GZ = 1024, 1280, 1536, 1792
C_SQ, C_SK, C_SV, C_SZ = 2048, 2304, 2432, 2560
C_MQ, C_MZ = 2816, 3072
C_LR = 3328
NP = 3456
IN_WIDTH = 3344
_O_GLR, _O_GZ = 1792, 1808

SUB = 128
TL = 512


def _dot(a, b):
    return jnp.dot(a, b, preferred_element_type=f32)


def _dot_nt(a, b):
    return lax.dot_general(a, b, (((1,), (1,)), ((), ())), preferred_element_type=f32)


def _dot_tn(a, b):
    return lax.dot_general(a, b, (((0,), (0,)), ((), ())), preferred_element_type=f32)


def _rmsnorm(x, g):
    return x * lax.rsqrt(jnp.mean(x * x, axis=-1, keepdims=True) + EPS) * g


def _silu(z):
    return z * (0.5 + 0.5 * jnp.tanh(0.5 * z))


def _log_sigmoid(x):
    return jnp.minimum(x, 0.0) - jnp.log1p(jnp.exp(-jnp.abs(x)))


PROJ_PIECES = tuple([(c, c + 256, c) for c in range(0, _O_GLR, 256)]
                    + [(_O_GZ + c, _O_GZ + c + 256, _O_GLR + c) for c in range(0, IN_WIDTH - _O_GZ, 256)]
                    + [(_O_GLR, _O_GZ, C_LR)])


def _project_piece(h, wt_ref, p_ref, rows, piece):
    w0, w1, c0 = PROJ_PIECES[piece]
    p_ref[rows, c0:c0 + (w1 - w0)] = _dot_nt(h, wt_ref[w0:w1, :])


def _project(h, wt_ref, p_ref, rows):
    for piece in range(len(PROJ_PIECES)):
        _project_piece(h, wt_ref, p_ref, rows, piece)


def _head_sum(x, j_ref):
    hi = x.astype(bf16)
    lo = (x - hi.astype(f32)).astype(bf16)
    return _dot(hi, j_ref[...]) + _dot(lo, j_ref[...])


def _rope(x, cos, sin_signed):
    lane = lax.broadcasted_iota(jnp.int32, x.shape, 1)
    swapped = jnp.where((lane % HEAD_D) < HEAD_D // 2,
                        pltpu.roll(x, LANES - HEAD_D // 2, 1), pltpu.roll(x, HEAD_D // 2, 1))
    return x * cos + swapped * sin_signed


def _stack_heads(x):
    lane_head = lax.broadcasted_iota(jnp.int32, x.shape, 1) // HEAD_D
    return jnp.concatenate([jnp.where(lane_head == h, x, 0.0) for h in range(N_HEADS)], axis=0)


def _unstack_heads(o, n):
    lane_head = lax.broadcasted_iota(jnp.int32, (n, GROUP_W), 1) // HEAD_D
    out = o[0:n]
    for h in range(1, N_HEADS):
        out = jnp.where(lane_head == h, o[h * n:(h + 1) * n], out)
    return out


def _stack_swa_q(q01, q23):
    low = lax.broadcasted_iota(jnp.int32, q01.shape, 1) < HEAD_D
    return jnp.concatenate([jnp.where(low, q01, 0.0), jnp.where(low, pltpu.roll(q01, HEAD_D, 1), 0.0),
                            jnp.where(low, 0.0, pltpu.roll(q23, HEAD_D, 1)), jnp.where(low, 0.0, q23)], axis=0)


def _unstack_swa(o, n):
    low = lax.broadcasted_iota(jnp.int32, (n, SWA_KV_W), 1) < HEAD_D
    c01 = jnp.where(low, o[0:n], pltpu.roll(o[n:2 * n], HEAD_D, 1))
    c23 = jnp.where(low, pltpu.roll(o[2 * n:3 * n], HEAD_D, 1), o[3 * n:4 * n])
    return jnp.concatenate([c01, c23], axis=1)


def _sink_column(sinks, n):
    return jnp.concatenate([jnp.full((n, 1), s, f32) for s in sinks], axis=0)


def _block_cumsum(la, block):
    row = lax.broadcasted_iota(jnp.int32, la.shape, 0) % block
    b = la
    s = 1
    while s < block:
        b = b + jnp.where(row >= s, pltpu.roll(b, s, 0), 0.0)
        s *= 2
    return b


def _gate_log_decay(p_lr, wg_ref, bg_ref):
    pre = _dot(p_lr.astype(bf16), wg_ref[...]) + bg_ref[...]
    return _log_sigmoid(pre) * (1.0 / GATE_NORM)


def _gla_norm_gate(o, gz, gnorm_ref, j_ref):
    ms = _head_sum(o * o, j_ref) * (1.0 / HEAD_D)
    return o * lax.rsqrt(ms + EPS) * gnorm_ref[...] * _silu(gz)


GROUPS = SUB // SUBLANES
LEVEL_GROUPS = (1, 2, 4)


def _row_groups(x):
    return [x[SUBLANES * r:SUBLANES * (r + 1), :] for r in range(x.shape[0] // SUBLANES)]


def _prompt_layer_kernel(nt, *refs):
    s = pl.program_id(0)
    pl.when(s == 0)(functools.partial(_prompt_first_step, *refs))
    for parity in range(2):
        pl.when((s > 0) & (s % 2 == parity))(functools.partial(_prompt_layer_step, nt, parity, *refs))


def _prompt_first_step(sinks_ref, x_ref, memp_ref, cos_ref, sin_ref, gpre_ref, gpost_ref, wt_ref,
                       wout_ref, wmem_ref, convw_ref, wg_ref, bg_ref, gnorm_ref, j_ref, lvl_ref, swab_ref,
                       y_ref, conv_out, gla_out, swak_out, swav_out, mk_out, mv_out,
                       p_buf, x_prev, woutb, *unused):
    x = x_ref[0]
    _project(_rmsnorm(x, gpre_ref[...]).astype(bf16), wt_ref, p_buf.at[0], slice(None))
    x_prev[...] = x
    woutb[...] = wout_ref[...].astype(bf16)


def _prompt_layer_step(nt, parity, sinks_ref, x_ref, memp_ref, cos_ref, sin_ref, gpre_ref, gpost_ref, wt_ref,
                       wout_ref, wmem_ref, convw_ref, wg_ref, bg_ref, gnorm_ref, j_ref, lvl_ref, swab_ref,
                       y_ref, conv_out, gla_out, swak_out, swav_out, mk_out, mv_out,
                       p_buf, x_prev, woutb, mix_scr, cbuf, st_scr, kprev, vprev, vtprev, mkb, mvtb, kv_nat):
    s = pl.program_id(0)
    t = jnp.maximum(s - 1, 0) % nt
    nsub = TL // SUB
    p_wr = p_buf.at[parity]
    p_scr = p_buf.at[1 - parity]

    @pl.when(t == 0)
    def _new_sequence():
        cbuf[...] = jnp.zeros((SUBLANES, GROUP_W), f32)
        st_scr[...] = jnp.zeros((GROUP_W, GROUP_W), f32)
        kprev[...] = jnp.zeros((SUB, SWA_KV_W), f32)
        vprev[...] = jnp.zeros((SUB, SWA_KV_W), f32)
        vtprev[...] = jnp.zeros((SWA_KV_W, SUB), f32)
        mkv = _dot(memp_ref[0].astype(bf16), wmem_ref[...].astype(bf16))
        mvt = mkv[:, GROUP_W:2 * GROUP_W].T
        mk_out[0] = mkv[:, 0:GROUP_W].T
        mv_out[0] = mvt
        mkb[...] = mkv[:, 0:GROUP_W].astype(bf16)
        mvtb[...] = mvt.astype(bf16)

    x = x_ref[0]
    h = _rmsnorm(x, gpre_ref[...]).astype(bf16)

    sub8 = lax.broadcasted_iota(jnp.int32, (SUBLANES, GROUP_W), 0)
    pending = list(range(len(PROJ_PIECES)))

    def project_next():
        if pending:
            _project_piece(h, wt_ref, p_wr, slice(None), pending.pop(0))

    def sub_tile(j):
        rows = slice(j * SUB, (j + 1) * SUB)
        project_next()

        u = p_scr[rows, C_CC:C_CC + GROUP_W] * p_scr[rows, C_CX:C_CX + GROUP_W]
        last = SUB - SUBLANES
        prev1 = jnp.where(sub8 == 0, cbuf[0:1, :], pltpu.roll(u[last:SUB, :], 1, 0))
        prev2 = jnp.where(sub8 == 0, cbuf[1:2, :], pltpu.roll(u[last - SUBLANES:last, :], 1, 0))
        um1 = jnp.concatenate([prev1, u[0:last, :]], axis=0)
        um2 = jnp.concatenate([prev2, prev1, u[0:last - SUBLANES, :]], axis=0)
        cy = convw_ref[0:1, :] * um2 + convw_ref[1:2, :] * um1 + convw_ref[2:3, :] * u
        cbuf[0:1, :] = u[SUB - 1:SUB, :]
        cbuf[1:2, :] = u[last - 1:last, :]
        a_out = p_scr[rows, C_CB:C_CB + GROUP_W] * cy * _silu(p_scr[rows, C_CZ:C_CZ + GROUP_W])
        mix_scr[rows, 0:GROUP_W] = a_out.astype(bf16)

        qg = _row_groups(p_scr[rows, C_GQ:C_GQ + GROUP_W] * QK_SCALE)
        k = p_scr[rows, C_GK:C_GK + GROUP_W]
        v = p_scr[rows, C_GV:C_GV + GROUP_W]
        kg = _row_groups(k)
        vg = _row_groups(v)
        lag = _row_groups(_gate_log_decay(p_scr[rows, C_LR:C_LR + GATE_RANK], wg_ref, bg_ref))
        bg_ = [lag[0]]
        for r in range(1, GROUPS):
            bg_.append(bg_[-1] + lag[r])
        tot = bg_[GROUPS - 1]
        pw = []
        for r in range(GROUPS):
            for s in range(r):
                pw.append(qg[r] * kg[s] * jnp.exp(bg_[r] - bg_[s]))
            pw.append(qg[r] * kg[r])
        scores = _dot(jnp.concatenate(pw, axis=0).astype(bf16), j_ref[...])
        project_next()
        og = []
        idx = 0
        for r in range(GROUPS):
            acc = None
            for s in range(r + 1):
                term = scores[SUBLANES * idx:SUBLANES * (idx + 1), :] * vg[s]
                acc = term if acc is None else acc + term
                idx += 1
            og.append(acc)
        o = jnp.concatenate(og, axis=0)

        def decayed(group):
            before = jnp.zeros((SUBLANES, GROUP_W), f32)
            after = jnp.zeros((SUBLANES, GROUP_W), f32)
            for s in range(1, group):
                before = before + jnp.where(sub8 % group >= s, pltpu.roll(tot, s, 0), 0.0)
                after = after + jnp.where(sub8 % group < group - s, pltpu.roll(tot, SUBLANES - s, 0), 0.0)
            qd = jnp.concatenate([qg[r] * jnp.exp(bg_[r] + before) for r in range(GROUPS)], axis=0)
            kd = jnp.concatenate([kg[r] * jnp.exp((tot - bg_[r]) + after) for r in range(GROUPS)], axis=0)
            return qd, kd

        attn = None
        for li, group in enumerate(LEVEL_GROUPS):
            qd, kd = decayed(group)
            s = _dot_nt(_stack_heads(qd).astype(bf16), kd.astype(bf16)) * lvl_ref[li]
            attn = s if attn is None else attn + s
        o = o + _unstack_heads(_dot(attn.astype(bf16), v.astype(bf16)), SUB)
        project_next()
        qd, kd = decayed(SUB // GLA_BLOCK)
        st = st_scr[...]
        o = o + _dot_nt(qd.astype(bf16), st.astype(bf16))
        total = jnp.sum(tot, axis=0, keepdims=True)
        upd = _dot_tn(v.astype(bf16), kd.astype(bf16))
        same_head = (lax.broadcasted_iota(jnp.int32, (GROUP_W, GROUP_W), 0) // HEAD_D
                     == lax.broadcasted_iota(jnp.int32, (GROUP_W, GROUP_W), 1) // HEAD_D)
        st_scr[...] = st * jnp.exp(total) + jnp.where(same_head, upd, 0.0)
        b_out = _gla_norm_gate(o, p_scr[rows, C_GZ:C_GZ + GROUP_W], gnorm_ref, j_ref)
        mix_scr[rows, GROUP_W:2 * GROUP_W] = b_out.astype(bf16)

        cos = cos_ref[rows, :]
        sin = sin_ref[rows, :]
        q01 = _rope(p_scr[rows, C_SQ:C_SQ + LANES], cos, sin) * QK_SCALE
        q23 = _rope(p_scr[rows, C_SQ + LANES:C_SQ + 2 * LANES], cos, sin) * QK_SCALE
        kr = _rope(p_scr[rows, C_SK:C_SK + SWA_KV_W], cos, sin)
        vx = p_scr[rows, C_SV:C_SV + SWA_KV_W]
        k_all = jnp.concatenate([kprev[...], kr], axis=0).astype(bf16)
        vt_new = vx.T
        vt_all = jnp.concatenate([vtprev[...], vt_new], axis=1).astype(bf16)
        first = jnp.where((t * nsub + j) == 0, 1, 0)
        sc = _dot_nt(k_all, _stack_swa_q(q01, q23).astype(bf16)) + swab_ref[first]
        lane_head = lax.broadcasted_iota(jnp.int32, (1, N_HEADS * SUB), 1) // SUB
        sink = jnp.full((1, N_HEADS * SUB), sinks_ref[0], f32)
        for hh in range(1, N_HEADS):
            sink = jnp.where(lane_head == hh, sinks_ref[hh], sink)
        mx = jnp.maximum(jnp.max(sc, axis=0, keepdims=True), sink)
        e = jnp.exp(sc - mx)
        den = jnp.sum(e, axis=0, keepdims=True) + jnp.exp(sink - mx)
        ot = _dot(vt_all, e.astype(bf16)) * (1.0 / den)
        oc = _unstack_swa(jnp.concatenate([ot[:, SUB * hh:SUB * (hh + 1)].T for hh in range(N_HEADS)], axis=0), SUB)
        mix_scr[rows, 2 * GROUP_W:3 * GROUP_W] = (oc * _silu(p_scr[rows, C_SZ:C_SZ + GROUP_W])).astype(bf16)
        kprev[...] = kr
        vprev[...] = vx
        vtprev[...] = vt_new
        project_next()

        qm = _stack_heads(p_scr[rows, C_MQ:C_MQ + GROUP_W] * QK_SCALE).astype(bf16)
        sm = _dot_nt(mkb[...], qm)
        mm = jnp.max(sm, axis=0, keepdims=True)
        em = jnp.exp(sm - mm)
        dm = jnp.sum(em, axis=0, keepdims=True)
        odt = _dot(mvtb[...], em.astype(bf16)) * (1.0 / dm)
        low = lax.broadcasted_iota(jnp.int32, (SUB, LANES), 1) < HEAD_D
        halves = []
        for pair in range(N_HEADS // 2):
            blk_rows = odt[LANES * pair:LANES * (pair + 1), :]
            even = blk_rows[:, SUB * (2 * pair):SUB * (2 * pair + 1)].T
            odd = blk_rows[:, SUB * (2 * pair + 1):SUB * (2 * pair + 2)].T
            halves.append(jnp.where(low, even, odd))
        od = jnp.concatenate(halves, axis=1)
        mix_scr[rows, 3 * GROUP_W:4 * GROUP_W] = (od * _silu(p_scr[rows, C_MZ:C_MZ + GROUP_W])).astype(bf16)

    for j in range(nsub):
        sub_tile(j)
    while pending:
        project_next()

    mo = _dot(mix_scr[...], woutb[...])
    y_ref[0] = x_prev[...] + _rmsnorm(mo, gpost_ref[...])
    x_prev[...] = x

    @pl.when(t == nt - 1)
    def _sequence_done():
        conv_out[0, 0:1, :] = cbuf[1:2, :]
        conv_out[0, 1:2, :] = cbuf[0:1, :]
        for src, dst in ((kprev, swak_out), (vprev, swav_out)):
            for r in range(GROUPS):
                kv_nat[pl.ds(r, SUBLANES, stride=GLA_BLOCK), :] = src[SUBLANES * r:SUBLANES * (r + 1), :]
            dst[0] = kv_nat[...].T
        st = st_scr[...]
        for hh in range(N_HEADS):
            gla_out[0, hh] = st[HEAD_D * hh:HEAD_D * (hh + 1), HEAD_D * hh:HEAD_D * (hh + 1)].T


def _layer_spec(shape, l):
    nd = len(shape)
    return pl.BlockSpec((None,) + tuple(shape), lambda s, _l=l, _nd=nd: (_l,) + (0,) * _nd)


def _const_spec(shape):
    nd = len(shape)
    return pl.BlockSpec(shape, lambda s, _nd=nd: (0,) * _nd)


def _prompt_layer(l, x, memp, cos, sin, gpre, gpost, wt, wout, wmem, convw, wg, bg, gnorm, sinks, jmat,
                  lvl, swab):
    B, L, _ = x.shape
    assert L % TL == 0
    nt = L // TL
    n_tiles = B * nt

    def projected(s):
        tile = jnp.minimum(s, n_tiles - 1)
        return tile // nt, tile % nt

    def mixed(s):
        tile = jnp.maximum(s - 1, 0)
        return tile // nt, tile % nt

    def per_sequence(shape):
        nd = len(shape) - 1
        return pl.BlockSpec(shape, lambda s, _nd=nd: (mixed(s)[0],) + (0,) * _nd)

    out_shape = (
        jax.ShapeDtypeStruct((B, L, D_MODEL), f32),
        jax.ShapeDtypeStruct((B, CONV_W - 1, GROUP_W), f32),
        jax.ShapeDtypeStruct((B, N_HEADS, HEAD_D, HEAD_D), f32),
        jax.ShapeDtypeStruct((B, SWA_KV_W, WINDOW), f32),
        jax.ShapeDtypeStruct((B, SWA_KV_W, WINDOW), f32),
        jax.ShapeDtypeStruct((B, GROUP_W, N_MEM), f32),
        jax.ShapeDtypeStruct((B, GROUP_W, N_MEM), f32),
    )
    in_specs = [
        pl.BlockSpec(memory_space=pltpu.SMEM),
        pl.BlockSpec((1, TL, D_MODEL), lambda s: projected(s) + (0,)),
        per_sequence((1, N_MEM, D_MODEL)),
        pl.BlockSpec((TL, LANES), lambda s: (mixed(s)[1], 0)),
        pl.BlockSpec((TL, LANES), lambda s: (mixed(s)[1], 0)),
        _layer_spec((1, D_MODEL), l), _layer_spec((1, D_MODEL), l),
        _layer_spec((IN_WIDTH, D_MODEL), l),
        _layer_spec((D_MODEL, D_MODEL), l), _layer_spec((D_MODEL, 2 * GROUP_W), l),
        _layer_spec((CONV_W, GROUP_W), l), _layer_spec((GATE_RANK, GROUP_W), l), _layer_spec((1, GROUP_W), l),
        _layer_spec((1, GROUP_W), l), _const_spec((GROUP_W, GROUP_W)),
        _const_spec((len(LEVEL_GROUPS), N_HEADS * SUB, SUB)), _const_spec((2, 2 * SUB, N_HEADS * SUB)),
    ]
    out_specs = (
        pl.BlockSpec((1, TL, D_MODEL), lambda s: mixed(s) + (0,)),
        per_sequence((1, CONV_W - 1, GROUP_W)),
        per_sequence((1, N_HEADS, HEAD_D, HEAD_D)),
        per_sequence((1, SWA_KV_W, WINDOW)),
        per_sequence((1, SWA_KV_W, WINDOW)),
        per_sequence((1, GROUP_W, N_MEM)),
        per_sequence((1, GROUP_W, N_MEM)),
    )
    scratch = [
        pltpu.VMEM((2, TL, NP), f32),
        pltpu.VMEM((TL, D_MODEL), f32),
        pltpu.VMEM((D_MODEL, D_MODEL), bf16),
        pltpu.VMEM((TL, D_MODEL), bf16),
        pltpu.VMEM((SUBLANES, GROUP_W), f32),
        pltpu.VMEM((GROUP_W, GROUP_W), f32),
        pltpu.VMEM((SUB, SWA_KV_W), f32),
        pltpu.VMEM((SUB, SWA_KV_W), f32),
        pltpu.VMEM((SWA_KV_W, SUB), f32),
        pltpu.VMEM((N_MEM, GROUP_W), bf16),
        pltpu.VMEM((GROUP_W, N_MEM), bf16),
        pltpu.VMEM((WINDOW, SWA_KV_W), f32),
    ]
    return pl.pallas_call(
        functools.partial(_prompt_layer_kernel, nt),
        grid=(n_tiles + 1,),
        in_specs=in_specs,
        out_specs=out_specs,
        out_shape=out_shape,
        scratch_shapes=scratch,
        compiler_params=pltpu.CompilerParams(dimension_semantics=("arbitrary",),
                                             vmem_limit_bytes=VMEM_LIMIT_BYTES),
        name="prompt_layer",
    )(sinks[l], x, memp, cos, sin, gpre, gpost, wt, wout, wmem, convw, wg, bg, gnorm, jmat, lvl, swab)


def _subtile_constants():
    row = np.arange(SUB)
    tok = (row % SUBLANES) * GLA_BLOCK + row // SUBLANES
    blk = row % SUBLANES
    levels = []
    for group in LEVEL_GROUPS:
        g = blk // group
        sel = (g[:, None] == g[None, :] + 1) & (g[:, None] % 2 == 1)
        levels.append(np.tile(sel, (N_HEADS, 1)))
    key_tok = np.concatenate([tok - SUB, tok])
    valid = (key_tok[None, :] <= tok[:, None]) & (key_tok[None, :] > tok[:, None] - WINDOW)
    bias = [np.where(valid, 0.0, NEG), np.where(valid & (key_tok[None, :] >= 0), 0.0, NEG)]
    bias = np.stack([np.tile(b, (N_HEADS, 1)).T for b in bias])
    return jnp.asarray(np.stack(levels), dtype=f32), jnp.asarray(bias, dtype=f32)


def _to_kernel_order(x, axis):
    shp = x.shape
    n = shp[axis]
    x = x.reshape(shp[:axis] + (n // SUB, SUBLANES, GLA_BLOCK) + shp[axis + 1:])
    return jnp.swapaxes(x, axis + 1, axis + 2).reshape(shp)


def _from_kernel_order(x, axis):
    shp = x.shape
    n = shp[axis]
    x = x.reshape(shp[:axis] + (n // SUB, GLA_BLOCK, SUBLANES) + shp[axis + 1:])
    return jnp.swapaxes(x, axis + 1, axis + 2).reshape(shp)


def _sample_kernel(seq_len, group, sinks_ref, x_hbm, cos_ref, sin_ref, convb_ref, sgla_ref, kc_ref, vc_ref,
                   mkc_ref, mvc_ref, gpre_ref, gpost_ref, wt_ref, wout_ref, convw_ref, wg_ref, bg_ref,
                   gnorm_ref, j_ref, place_ref,
                   y_hbm, conv_out, gla_out, swak_out, swav_out,
                   hs, p_scr, mix_scr, woutb, sem):
    l = pl.program_id(0)
    g_step = pl.program_id(1)
    n_layers = pl.num_programs(0)
    n_steps = pl.num_programs(1)
    n_tok = hs.shape[0]
    n = group * seq_len
    chunks = [slice(c * TL, (c + 1) * TL) for c in range(n_tok // TL)]

    @pl.when((l == 0) & (g_step == 0))
    def _load():
        cp = pltpu.make_async_copy(x_hbm, hs, sem.at[0])
        cp.start()
        cp.wait()

    @pl.when(g_step == 0)
    def _project_all():
        woutb[...] = wout_ref[...].astype(bf16)
        for rows in chunks:
            h = _rmsnorm(hs[rows, :], gpre_ref[...]).astype(bf16)
            _project(h, wt_ref, p_scr, rows)

    rows = pl.ds(pl.multiple_of(g_step * n, n), n)
    row_t = lax.broadcasted_iota(jnp.int32, (n, GROUP_W), 0) % seq_len

    u = p_scr[rows, C_CC:C_CC + GROUP_W] * p_scr[rows, C_CX:C_CX + GROUP_W]
    hist = convb_ref[...]
    um1 = jnp.where(row_t >= 1, pltpu.roll(u, 1, 0), pltpu.roll(hist, n - 1, 0))
    um2 = jnp.where(row_t >= 2, pltpu.roll(u, 2, 0), hist)
    cy = convw_ref[0:1, :] * um2 + convw_ref[1:2, :] * um1 + convw_ref[2:3, :] * u
    a_out = p_scr[rows, C_CB:C_CB + GROUP_W] * cy * _silu(p_scr[rows, C_CZ:C_CZ + GROUP_W])
    mix_scr[rows, 0:GROUP_W] = a_out.astype(bf16)
    for g in range(group):
        conv_out[g] = u[g * seq_len + seq_len - (CONV_W - 1):(g + 1) * seq_len, :]

    qs = p_scr[rows, C_GQ:C_GQ + GROUP_W] * QK_SCALE
    k = p_scr[rows, C_GK:C_GK + GROUP_W]
    v = p_scr[rows, C_GV:C_GV + GROUP_W]
    la = _gate_log_decay(p_scr[rows, C_LR:C_LR + GATE_RANK], wg_ref, bg_ref)
    bc = _block_cumsum(la, seq_len)
    o = _dot((qs * k).astype(bf16), j_ref[...]) * v
    for d in range(1, seq_len):
        dec = jnp.exp(jnp.where(row_t >= d, bc - pltpu.roll(bc, d, 0), NEG))
        pw = qs * pltpu.roll(k, d, 0) * dec
        o = o + _dot(pw.astype(bf16), j_ref[...]) * pltpu.roll(v, d, 0)
    qd = qs * jnp.exp(bc)
    tot = [bc[(g + 1) * seq_len - 1:(g + 1) * seq_len, :] for g in range(group)]
    kd = k * jnp.exp(jnp.concatenate(
        [tot[g] - bc[g * seq_len:(g + 1) * seq_len, :] for g in range(group)], axis=0))
    low = lax.broadcasted_iota(jnp.int32, (seq_len, LANES), 1) < HEAD_D
    low_st = lax.broadcasted_iota(jnp.int32, (HEAD_D, LANES), 1) < HEAD_D
    o_inter = []
    for g in range(group):
        rs = slice(g * seq_len, (g + 1) * seq_len)
        alpha = jnp.exp(tot[g])
        per_pair = []
        for pair in range(N_HEADS // 2):
            ls = slice(pair * LANES, (pair + 1) * LANES)
            st0 = sgla_ref[g, pair]
            q_pair = qd[rs, ls]
            q2 = jnp.concatenate([jnp.where(low, q_pair, 0.0), jnp.where(low, 0.0, q_pair)], axis=0)
            o2 = _dot_nt(q2.astype(bf16), st0.astype(bf16))
            per_pair.append(jnp.concatenate([o2[0:seq_len], o2[seq_len:2 * seq_len]], axis=1))
            kd_pair = kd[rs, ls].astype(bf16)
            upd = [_dot_tn(v[rs, (2 * pair + hh) * HEAD_D:(2 * pair + hh + 1) * HEAD_D].astype(bf16), kd_pair)
                   for hh in range(2)]
            gla_out[g, pair] = st0 * alpha[:, ls] + jnp.where(low_st, upd[0], upd[1])
        o_inter.append(jnp.concatenate(per_pair, axis=1))
    o = o + jnp.concatenate(o_inter, axis=0)
    b_out = _gla_norm_gate(o, p_scr[rows, C_GZ:C_GZ + GROUP_W], gnorm_ref, j_ref)
    mix_scr[rows, GROUP_W:2 * GROUP_W] = b_out.astype(bf16)

    cos = cos_ref[...]
    sin = sin_ref[...]
    q01 = _rope(p_scr[rows, C_SQ:C_SQ + LANES], cos, sin) * QK_SCALE
    q23 = _rope(p_scr[rows, C_SQ + LANES:C_SQ + 2 * LANES], cos, sin) * QK_SCALE
    kr = _rope(p_scr[rows, C_SK:C_SK + SWA_KV_W], cos, sin)
    vx = p_scr[rows, C_SV:C_SV + SWA_KV_W]
    qm = p_scr[rows, C_MQ:C_MQ + GROUP_W] * QK_SCALE
    hq = N_HEADS * seq_len
    sink_seq = _sink_column([sinks_ref[l, hh] for hh in range(N_HEADS)], seq_len)
    sink = jnp.concatenate([sink_seq] * group, axis=0)
    hq_all = group * hq
    qrow = lax.broadcasted_iota(jnp.int32, (hq_all, WINDOW), 0) % seq_len
    ccol = lax.broadcasted_iota(jnp.int32, (hq_all, WINDOW), 1)
    cache_valid = ccol > qrow
    nrow = lax.broadcasted_iota(jnp.int32, (hq_all, seq_len), 0) % seq_len
    ncol = lax.broadcasted_iota(jnp.int32, (hq_all, seq_len), 1)
    new_valid = ncol <= nrow
    keep_old = lax.broadcasted_iota(jnp.int32, (SWA_KV_W, WINDOW), 1) < WINDOW - seq_len

    def exact_split(a):
        hi = a.astype(bf16)
        r1 = a - hi.astype(f32)
        mid = r1.astype(bf16)
        lo = (r1 - mid.astype(f32)).astype(bf16)
        return jnp.concatenate([hi, mid, lo], axis=0)

    def shifted_cache(old_t, new_rows):
        placed = _dot_tn(exact_split(new_rows), place_ref[...])
        return jnp.where(keep_old, pltpu.roll(old_t, WINDOW - seq_len, 1), placed)

    seqs = [slice(g * seq_len, (g + 1) * seq_len) for g in range(group)]
    stk = [slice(g * hq, (g + 1) * hq) for g in range(group)]
    qst = [_stack_swa_q(q01[rs], q23[rs]).astype(bf16) for rs in seqs]
    s_c = jnp.concatenate([_dot(qst[g], kc_ref[g].astype(bf16)) for g in range(group)], axis=0)
    s_n = jnp.concatenate([_dot_nt(qst[g], kr[seqs[g]].astype(bf16)) for g in range(group)], axis=0)
    s_c = jnp.where(cache_valid, s_c, NEG)
    s_n = jnp.where(new_valid, s_n, NEG)
    mx = jnp.maximum(jnp.maximum(jnp.max(s_c, axis=1, keepdims=True), jnp.max(s_n, axis=1, keepdims=True)), sink)
    e_c = jnp.exp(s_c - mx)
    e_n = jnp.exp(s_n - mx)
    den = jnp.sum(e_c, axis=1, keepdims=True) + jnp.sum(e_n, axis=1, keepdims=True) + jnp.exp(sink - mx)
    e_c = e_c.astype(bf16)
    e_n = e_n.astype(bf16)
    ov = jnp.concatenate([_dot_nt(e_c[stk[g]], vc_ref[g].astype(bf16)) + _dot(e_n[stk[g]], vx[seqs[g]].astype(bf16))
                          for g in range(group)], axis=0) * (1.0 / den)
    oc = jnp.concatenate([_unstack_swa(ov[stk[g]], seq_len) for g in range(group)], axis=0)
    for g in range(group):
        swak_out[g] = shifted_cache(kc_ref[g], kr[seqs[g]])
        swav_out[g] = shifted_cache(vc_ref[g], vx[seqs[g]])

    sm = jnp.concatenate([_dot(_stack_heads(qm[seqs[g]]).astype(bf16), mkc_ref[g].astype(bf16))
                          for g in range(group)], axis=0)
    mm = jnp.max(sm, axis=1, keepdims=True)
    em = jnp.exp(sm - mm)
    dm = jnp.sum(em, axis=1, keepdims=True)
    em = em.astype(bf16)
    odv = jnp.concatenate([_dot_nt(em[stk[g]], mvc_ref[g].astype(bf16)) for g in range(group)], axis=0) * (1.0 / dm)
    od = jnp.concatenate([_unstack_heads(odv[stk[g]], seq_len) for g in range(group)], axis=0)
    mix_scr[rows, 2 * GROUP_W:3 * GROUP_W] = (oc * _silu(p_scr[rows, C_SZ:C_SZ + GROUP_W])).astype(bf16)
    mix_scr[rows, 3 * GROUP_W:4 * GROUP_W] = (od * _silu(p_scr[rows, C_MZ:C_MZ + GROUP_W])).astype(bf16)

    @pl.when(g_step == n_steps - 1)
    def _residual():
        for rws in chunks:
            mo = _dot(mix_scr[rws, :], woutb[...])
            hs[rws, :] = hs[rws, :] + _rmsnorm(mo, gpost_ref[...])

    @pl.when((l == n_layers - 1) & (g_step == n_steps - 1))
    def _store():
        cp = pltpu.make_async_copy(hs, y_hbm, sem.at[0])
        cp.start()
        cp.wait()


def _sample_layers(x, cos, sin, convb, sgla, kc, vc, mkc, mvc, gpre, gpost, wt, wout, convw, wg, bg, gnorm,
                   sinks, jmat, place, seq_len, group):
    depth, nseq = sgla.shape[0], sgla.shape[1]
    n_tok = nseq * seq_len
    assert nseq % group == 0 and seq_len == SUBLANES and n_tok % TL == 0
    n = group * seq_len

    def per_layer(shape, single_buffer=False):
        nd = len(shape)
        kw = dict(pipeline_mode=pl.Buffered(1)) if single_buffer else {}
        return pl.BlockSpec((None,) + tuple(shape), lambda l, g, _nd=nd: (l,) + (0,) * _nd, **kw)

    def per_group(shape):
        nd = len(shape) - 1
        return pl.BlockSpec((None,) + tuple(shape), lambda l, g, _nd=nd: (l, g) + (0,) * _nd)

    def const(shape):
        nd = len(shape)
        return pl.BlockSpec(shape, lambda l, g, _nd=nd: (0,) * _nd)

    in_specs = [
        pl.BlockSpec(memory_space=pltpu.SMEM),
        pl.BlockSpec(memory_space=pl.ANY),
        const((n, LANES)), const((n, LANES)),
        per_group((n, GROUP_W)),
        per_group((group, N_HEADS // 2, HEAD_D, LANES)),
        per_group((group, SWA_KV_W, WINDOW)), per_group((group, SWA_KV_W, WINDOW)),
        per_group((group, GROUP_W, N_MEM)), per_group((group, GROUP_W, N_MEM)),
        per_layer((1, D_MODEL)), per_layer((1, D_MODEL)),
        per_layer((IN_WIDTH, D_MODEL), True),
        per_layer((D_MODEL, D_MODEL), True),
        per_layer((CONV_W, GROUP_W)), per_layer((GATE_RANK, GROUP_W)), per_layer((1, GROUP_W)), per_layer((1, GROUP_W)),
        const((GROUP_W, GROUP_W)), const((3 * seq_len, WINDOW)),
    ]
    out_shape = (
        jax.ShapeDtypeStruct((n_tok, D_MODEL), f32),
        jax.ShapeDtypeStruct((depth, nseq, CONV_W - 1, GROUP_W), f32),
        jax.ShapeDtypeStruct((depth, nseq, N_HEADS // 2, HEAD_D, LANES), f32),
        jax.ShapeDtypeStruct((depth, nseq, SWA_KV_W, WINDOW), f32),
        jax.ShapeDtypeStruct((depth, nseq, SWA_KV_W, WINDOW), f32),
    )
    out_specs = (
        pl.BlockSpec(memory_space=pl.ANY),
        per_group((group, CONV_W - 1, GROUP_W)),
        per_group((group, N_HEADS // 2, HEAD_D, LANES)),
        per_group((group, SWA_KV_W, WINDOW)), per_group((group, SWA_KV_W, WINDOW)),
    )
    scratch = [
        pltpu.VMEM((n_tok, D_MODEL), f32),
        pltpu.VMEM((n_tok, NP), f32),
        pltpu.VMEM((n_tok, D_MODEL), bf16),
        pltpu.VMEM((D_MODEL, D_MODEL), bf16),
        pltpu.SemaphoreType.DMA((1,)),
    ]
    return pl.pallas_call(
        functools.partial(_sample_kernel, seq_len, group),
        grid=(depth, nseq // group),
        in_specs=in_specs,
        out_specs=out_specs,
        out_shape=out_shape,
        scratch_shapes=scratch,
        compiler_params=pltpu.CompilerParams(dimension_semantics=("arbitrary", "arbitrary"),
                                             vmem_limit_bytes=VMEM_LIMIT_BYTES),
        name="sample_layers",
    )(sinks, x, cos, sin, convb, sgla, kc, vc, mkc, mvc, gpre, gpost, wt, wout, convw, wg, bg, gnorm,
      jmat, place)


def _rope_tables(pos):
    half = HEAD_D // 2
    inv = np.power(ROPE_THETA, -np.arange(half, dtype=np.float64) / half)
    ang = np.asarray(pos, np.float64)[:, None] * inv[None, :]
    cos = np.tile(np.cos(ang), (1, LANES // half))
    sin = np.sin(ang)
    sin_signed = np.tile(np.concatenate([-sin, sin], axis=1), (1, LANES // HEAD_D))
    return cos.astype(np.float32), sin_signed.astype(np.float32)


def _rows_to_kernel_order(a):
    n, w = a.shape
    return a.reshape(n // SUB, SUBLANES, GLA_BLOCK, w).swapaxes(1, 2).reshape(n, w)


def _feature_major(cache):
    d, s, p, h, e = cache.shape
    return jnp.transpose(cache, (0, 1, 3, 4, 2)).reshape(d, s, h * e, p)


def _pair_states(state):
    dd, s, h, e, _ = state.shape
    a = jnp.swapaxes(state, -1, -2).reshape(dd, s, h // 2, 2, e, e)
    return jnp.transpose(a, (0, 1, 2, 4, 3, 5)).reshape(dd, s, h // 2, e, 2 * e)


def _unpair_states(paired):
    dd, s, hp, e, _ = paired.shape
    a = jnp.transpose(paired.reshape(dd, s, hp, e, 2, e), (0, 1, 2, 4, 3, 5))
    return jnp.swapaxes(a.reshape(dd, s, 2 * hp, e, e), -1, -2)


def _position_major(cache_t, heads):
    d, s, he, p = cache_t.shape
    return jnp.transpose(cache_t.reshape(d, s, heads, he // heads, p), (0, 1, 4, 2, 3))


def kernel(x_prompt, x_sample, state_conv, state_gla, cache_swa_k, cache_swa_v, cache_mem_k, cache_mem_v,
           mem_prompt, norm_pre, norm_post, w_in, conv_w, gla_w_gate, gla_b_gate, gla_norm, swa_sinks,
           w_mem_kv, w_out):
    depth = w_in.shape[0]
    B, L, _ = x_prompt.shape
    nseq, seq_len, _ = x_sample.shape
    group = 8

    wt = jnp.swapaxes(w_in, 1, 2).astype(bf16)
    wout = w_out
    wmem = w_mem_kv
    wg = gla_w_gate.astype(bf16)
    head_id = np.arange(GROUP_W) // HEAD_D
    jmat = jnp.asarray(head_id[:, None] == head_id[None, :], dtype=bf16)
    lvl, swab = _subtile_constants()
    place = np.zeros((3 * seq_len, WINDOW), np.float32)
    for piece in range(3):
        place[piece * seq_len + np.arange(seq_len), WINDOW - seq_len + np.arange(seq_len)] = 1.0
    place = jnp.asarray(place, dtype=bf16)
    sinks = swa_sinks.astype(f32)
    gpre = norm_pre[:, None, :]
    gpost = norm_post[:, None, :]
    bg = gla_b_gate[:, None, :]
    gn = gla_norm[:, None, :]

    cos_p, sin_p = (jnp.asarray(_rows_to_kernel_order(a)) for a in _rope_tables(np.arange(L)))
    cos_s, sin_s = (jnp.asarray(np.tile(a, (group, 1))) for a in _rope_tables(PAST_LEN + np.arange(seq_len)))

    convb = jnp.pad(state_conv, ((0, 0), (0, 0), (0, seq_len - (CONV_W - 1)), (0, 0)))
    convb = convb.reshape(depth, nseq * seq_len, GROUP_W)
    ys, conv_s, gla_s, swak_s, swav_s = _sample_layers(
        x_sample.reshape(nseq * seq_len, D_MODEL), cos_s, sin_s, convb, _pair_states(state_gla),
        _feature_major(cache_swa_k), _feature_major(cache_swa_v),
        _feature_major(cache_mem_k), _feature_major(cache_mem_v),
        gpre, gpost, wt, wout, conv_w, wg, bg, gn, sinks, jmat, place, seq_len, group)

    hp = _to_kernel_order(x_prompt, 1)
    outs_p = [[] for _ in range(6)]
    for l in range(depth):
        res = _prompt_layer(l, hp, mem_prompt, cos_p, sin_p, gpre, gpost, wt, wout, wmem, conv_w, wg, bg, gn,
                            sinks, jmat, lvl, swab)
        hp = res[0]
        for i in range(6):
            outs_p[i].append(res[i + 1])

    return (_from_kernel_order(hp, 1), ys.reshape(nseq, seq_len, D_MODEL),
            jnp.stack(outs_p[0]), jnp.stack(outs_p[1]),
            _position_major(jnp.stack(outs_p[2]), 2), _position_major(jnp.stack(outs_p[3]), 2),
            _position_major(jnp.stack(outs_p[4]), N_HEADS), _position_major(jnp.stack(outs_p[5]), N_HEADS),
            conv_s, _unpair_states(gla_s), _position_major(swak_s, 2), _position_major(swav_s, 2))
```

```python
import functools

import jax
import jax.numpy as jnp
import numpy as np
from jax import lax
from jax.experimental import pallas as pl
from jax.experimental.pallas import tpu as pltpu

f32 = jnp.float32
bf16 = jnp.bfloat16

D_MODEL = 1024
GROUP_W = 256
HEAD_D = 64
N_HEADS = 4
SWA_KV_W = 128
N_MEM = 256
WINDOW = 128
CONV_W = 3
GATE_RANK = 16
GATE_NORM = 16.0
GLA_BLOCK = 16
ROPE_THETA = 10000.0
PAST_LEN = 8192
EPS = 1e-6
NEG = -1e30
QK_SCALE = HEAD_D ** -0.5

LANES = 128
SUBLANES = 8
VMEM_LIMIT_BYTES = 56 * 1024 * 1024

C_CX, C_CB, C_CC, C_CZ = 0, 256, 512, 768
C_GQ, C_GK, C_GV, C_GZ = 1024, 1280, 1536, 1792
C_SQ, C_SK, C_SV, C_SZ = 2048, 2304, 2432, 2560
C_MQ, C_MZ = 2816, 3072
C_LR = 3328
NP = 3456
IN_WIDTH = 3344
_O_GLR, _O_GZ = 1792, 1808

SUB = 128
TL = 512


def _dot(a, b):
    return jnp.dot(a, b, preferred_element_type=f32)


def _dot_nt(a, b):
    return lax.dot_general(a, b, (((1,), (1,)), ((), ())), preferred_element_type=f32)


def _dot_tn(a, b):
    return lax.dot_general(a, b, (((0,), (0,)), ((), ())), preferred_element_type=f32)


def _rmsnorm(x, g):
    return x * lax.rsqrt(jnp.mean(x * x, axis=-1, keepdims=True) + EPS) * g


def _silu(z):
    return z * (0.5 + 0.5 * jnp.tanh(0.5 * z))


def _log_sigmoid(x):
    return jnp.minimum(x, 0.0) - jnp.log1p(jnp.exp(-jnp.abs(x)))


PROJ_PIECES = tuple([(c, c + 256, c) for c in range(0, _O_GLR, 256)]
                    + [(_O_GZ + c, _O_GZ + c + 256, _O_GLR + c) for c in range(0, IN_WIDTH - _O_GZ, 256)]
                    + [(_O_GLR, _O_GZ, C_LR)])


def _project_piece(h, wt_ref, p_ref, rows, piece):
    w0, w1, c0 = PROJ_PIECES[piece]
    p_ref[rows, c0:c0 + (w1 - w0)] = _dot_nt(h, wt_ref[w0:w1, :])


def _project(h, wt_ref, p_ref, rows):
    for piece in range(len(PROJ_PIECES)):
        _project_piece(h, wt_ref, p_ref, rows, piece)


def _head_sum(x, j_ref):
    hi = x.astype(bf16)
    lo = (x - hi.astype(f32)).astype(bf16)
    return _dot(hi, j_ref[...]) + _dot(lo, j_ref[...])


def _rope(x, cos, sin_signed):
    lane = lax.broadcasted_iota(jnp.int32, x.shape, 1)
    swapped = jnp.where((lane % HEAD_D) < HEAD_D // 2,
                        pltpu.roll(x, LANES - HEAD_D // 2, 1), pltpu.roll(x, HEAD_D // 2, 1))
    return x * cos + swapped * sin_signed


def _stack_heads(x):
    lane_head = lax.broadcasted_iota(jnp.int32, x.shape, 1) // HEAD_D
    return jnp.concatenate([jnp.where(lane_head == h, x, 0.0) for h in range(N_HEADS)], axis=0)


def _unstack_heads(o, n):
    lane_head = lax.broadcasted_iota(jnp.int32, (n, GROUP_W), 1) // HEAD_D
    out = o[0:n]
    for h in range(1, N_HEADS):
        out = jnp.where(lane_head == h, o[h * n:(h + 1) * n], out)
    return out


def _stack_swa_q(q01, q23):
    low = lax.broadcasted_iota(jnp.int32, q01.shape, 1) < HEAD_D
    return jnp.concatenate([jnp.where(low, q01, 0.0), jnp.where(low, pltpu.roll(q01, HEAD_D, 1), 0.0),
                            jnp.where(low, 0.0, pltpu.roll(q23, HEAD_D, 1)), jnp.where(low, 0.0, q23)], axis=0)


def _unstack_swa(o, n):
    low = lax.broadcasted_iota(jnp.int32, (n, SWA_KV_W), 1) < HEAD_D
    c01 = jnp.where(low, o[0:n], pltpu.roll(o[n:2 * n], HEAD_D, 1))
    c23 = jnp.where(low, pltpu.roll(o[2 * n:3 * n], HEAD_D, 1), o[3 * n:4 * n])
    return jnp.concatenate([c01, c23], axis=1)


def _sink_column(sinks, n):
    return jnp.concatenate([jnp.full((n, 1), s, f32) for s in sinks], axis=0)


def _block_cumsum(la, block):
    row = lax.broadcasted_iota(jnp.int32, la.shape, 0) % block
    b = la
    s = 1
    while s < block:
        b = b + jnp.where(row >= s, pltpu.roll(b, s, 0), 0.0)
        s *= 2
    return b


def _gate_log_decay(p_lr, wg_ref, bg_ref):
    pre = _dot(p_lr.astype(bf16), wg_ref[...]) + bg_ref[...]
    return _log_sigmoid(pre) * (1.0 / GATE_NORM)


def _gla_norm_gate(o, gz, gnorm_ref, j_ref):
    ms = _head_sum(o * o, j_ref) * (1.0 / HEAD_D)
    return o * lax.rsqrt(ms + EPS) * gnorm_ref[...] * _silu(gz)


GROUPS = SUB // SUBLANES
LEVEL_GROUPS = (1, 2, 4)


def _row_groups(x):
    return [x[SUBLANES * r:SUBLANES * (r + 1), :] for r in range(x.shape[0] // SUBLANES)]


def _prompt_layer_kernel(nt, *refs):
    s = pl.program_id(0)
    pl.when(s == 0)(functools.partial(_prompt_first_step, *refs))
    for parity in range(2):
        pl.when((s > 0) & (s % 2 == parity))(functools.partial(_prompt_layer_step, nt, parity, *refs))


def _prompt_first_step(sinks_ref, x_ref, memp_ref, cos_ref, sin_ref, gpre_ref, gpost_ref, wt_ref,
                       wout_ref, wmem_ref, convw_ref, wg_ref, bg_ref, gnorm_ref, j_ref, lvl_ref, swab_ref,
                       y_ref, conv_out, gla_out, swak_out, swav_out, mk_out, mv_out,
                       p_buf, x_prev, woutb, *unused):
    x = x_ref[0]
    _project(_rmsnorm(x, gpre_ref[...]).astype(bf16), wt_ref, p_buf.at[0], slice(None))
    x_prev[...] = x
    woutb[...] = wout_ref[...].astype(bf16)


def _prompt_layer_step(nt, parity, sinks_ref, x_ref, memp_ref, cos_ref, sin_ref, gpre_ref, gpost_ref, wt_ref,
                       wout_ref, wmem_ref, convw_ref, wg_ref, bg_ref, gnorm_ref, j_ref, lvl_ref, swab_ref,
                       y_ref, conv_out, gla_out, swak_out, swav_out, mk_out, mv_out,
                       p_buf, x_prev, woutb, mix_scr, cbuf, st_scr, kprev, vprev, vtprev, mkb, mvtb, kv_nat):
    s = pl.program_id(0)
    t = jnp.maximum(s - 1, 0) % nt
    nsub = TL // SUB
    p_wr = p_buf.at[parity]
    p_scr = p_buf.at[1 - parity]

    @pl.when(t == 0)
    def _new_sequence():
        cbuf[...] = jnp.zeros((SUBLANES, GROUP_W), f32)
        st_scr[...] = jnp.zeros((GROUP_W, GROUP_W), f32)
        kprev[...] = jnp.zeros((SUB, SWA_KV_W), f32)
        vprev[...] = jnp.zeros((SUB, SWA_KV_W), f32)
        vtprev[...] = jnp.zeros((SWA_KV_W, SUB), f32)
        mkv = _dot(memp_ref[0].astype(bf16), wmem_ref[...].astype(bf16))
        mvt = mkv[:, GROUP_W:2 * GROUP_W].T
        mk_out[0] = mkv[:, 0:GROUP_W].T
        mv_out[0] = mvt
        mkb[...] = mkv[:, 0:GROUP_W].astype(bf16)
        mvtb[...] = mvt.astype(bf16)

    x = x_ref[0]
    h = _rmsnorm(x, gpre_ref[...]).astype(bf16)

    sub8 = lax.broadcasted_iota(jnp.int32, (SUBLANES, GROUP_W), 0)
    pending = list(range(len(PROJ_PIECES)))

    def project_next():
        if pending:
            _project_piece(h, wt_ref, p_wr, slice(None), pending.pop(0))

    def sub_tile(j):
        rows = slice(j * SUB, (j + 1) * SUB)
        project_next()

        u = p_scr[rows, C_CC:C_CC + GROUP_W] * p_scr[rows, C_CX:C_CX + GROUP_W]
        last = SUB - SUBLANES
        prev1 = jnp.where(sub8 == 0, cbuf[0:1, :], pltpu.roll(u[last:SUB, :], 1, 0))
        prev2 = jnp.where(sub8 == 0, cbuf[1:2, :], pltpu.roll(u[last - SUBLANES:last, :], 1, 0))
        um1 = jnp.concatenate([prev1, u[0:last, :]], axis=0)
        um2 = jnp.concatenate([prev2, prev1, u[0:last - SUBLANES, :]], axis=0)
        cy = convw_ref[0:1, :] * um2 + convw_ref[1:2, :] * um1 + convw_ref[2:3, :] * u
        cbuf[0:1, :] = u[SUB - 1:SUB, :]
        cbuf[1:2, :] = u[last - 1:last, :]
        a_out = p_scr[rows, C_CB:C_CB + GROUP_W] * cy * _silu(p_scr[rows, C_CZ:C_CZ + GROUP_W])
        mix_scr[rows, 0:GROUP_W] = a_out.astype(bf16)

        qg = _row_groups(p_scr[rows, C_GQ:C_GQ + GROUP_W] * QK_SCALE)
        k = p_scr[rows, C_GK:C_GK + GROUP_W]
        v = p_scr[rows, C_GV:C_GV + GROUP_W]
        kg = _row_groups(k)
        vg = _row_groups(v)
        lag = _row_groups(_gate_log_decay(p_scr[rows, C_LR:C_LR + GATE_RANK], wg_ref, bg_ref))
        bg_ = [lag[0]]
        for r in range(1, GROUPS):
            bg_.append(bg_[-1] + lag[r])
        tot = bg_[GROUPS - 1]
        pw = []
        for r in range(GROUPS):
            for s in range(r):
                pw.append(qg[r] * kg[s] * jnp.exp(bg_[r] - bg_[s]))
            pw.append(qg[r] * kg[r])
        scores = _dot(jnp.concatenate(pw, axis=0).astype(bf16), j_ref[...])
        project_next()
        og = []
        idx = 0
        for r in range(GROUPS):
            acc = None
            for s in range(r + 1):
                term = scores[SUBLANES * idx:SUBLANES * (idx + 1), :] * vg[s]
                acc = term if acc is None else acc + term
                idx += 1
            og.append(acc)
        o = jnp.concatenate(og, axis=0)

        def decayed(group):
            before = jnp.zeros((SUBLANES, GROUP_W), f32)
            after = jnp.zeros((SUBLANES, GROUP_W), f32)
            for s in range(1, group):
                before = before + jnp.where(sub8 % group >= s, pltpu.roll(tot, s, 0), 0.0)
                after = after + jnp.where(sub8 % group < group - s, pltpu.roll(tot, SUBLANES - s, 0), 0.0)
            qd = jnp.concatenate([qg[r] * jnp.exp(bg_[r] + before) for r in range(GROUPS)], axis=0)
            kd = jnp.concatenate([kg[r] * jnp.exp((tot - bg_[r]) + after) for r in range(GROUPS)], axis=0)
            return qd, kd

        attn = None
        for li, group in enumerate(LEVEL_GROUPS):
            qd, kd = decayed(group)
            s = _dot_nt(_stack_heads(qd).astype(bf16), kd.astype(bf16)) * lvl_ref[li]
            attn = s if attn is None else attn + s
        o = o + _unstack_heads(_dot(attn.astype(bf16), v.astype(bf16)), SUB)
        project_next()
        qd, kd = decayed(SUB // GLA_BLOCK)
        st = st_scr[...]
        o = o + _dot_nt(qd.astype(bf16), st.astype(bf16))
        total = jnp.sum(tot, axis=0, keepdims=True)
        upd = _dot_tn(v.astype(bf16), kd.astype(bf16))
        same_head = (lax.broadcasted_iota(jnp.int32, (GROUP_W, GROUP_W), 0) // HEAD_D
                     == lax.broadcasted_iota(jnp.int32, (GROUP_W, GROUP_W), 1) // HEAD_D)
        st_scr[...] = st * jnp.exp(total) + jnp.where(same_head, upd, 0.0)
        b_out = _gla_norm_gate(o, p_scr[rows, C_GZ:C_GZ + GROUP_W], gnorm_ref, j_ref)
        mix_scr[rows, GROUP_W:2 * GROUP_W] = b_out.astype(bf16)

        cos = cos_ref[rows, :]
        sin = sin_ref[rows, :]
        q01 = _rope(p_scr[rows, C_SQ:C_SQ + LANES], cos, sin) * QK_SCALE
        q23 = _rope(p_scr[rows, C_SQ + LANES:C_SQ + 2 * LANES], cos, sin) * QK_SCALE
        kr = _rope(p_scr[rows, C_SK:C_SK + SWA_KV_W], cos, sin)
        vx = p_scr[rows, C_SV:C_SV + SWA_KV_W]
        k_all = jnp.concatenate([kprev[...], kr], axis=0).astype(bf16)
        vt_new = vx.T
        vt_all = jnp.concatenate([vtprev[...], vt_new], axis=1).astype(bf16)
        first = jnp.where((t * nsub + j) == 0, 1, 0)
        sc = _dot_nt(k_all, _stack_swa_q(q01, q23).astype(bf16)) + swab_ref[first]
        lane_head = lax.broadcasted_iota(jnp.int32, (1, N_HEADS * SUB), 1) // SUB
        sink = jnp.full((1, N_HEADS * SUB), sinks_ref[0], f32)
        for hh in range(1, N_HEADS):
            sink = jnp.where(lane_head == hh, sinks_ref[hh], sink)
        mx = jnp.maximum(jnp.max(sc, axis=0, keepdims=True), sink)
        e = jnp.exp(sc - mx)
        den = jnp.sum(e, axis=0, keepdims=True) + jnp.exp(sink - mx)
        ot = _dot(vt_all, e.astype(bf16)) * (1.0 / den)
        oc = _unstack_swa(jnp.concatenate([ot[:, SUB * hh:SUB * (hh + 1)].T for hh in range(N_HEADS)], axis=0), SUB)
        mix_scr[rows, 2 * GROUP_W:3 * GROUP_W] = (oc * _silu(p_scr[rows, C_SZ:C_SZ + GROUP_W])).astype(bf16)
        kprev[...] = kr
        vprev[...] = vx
        vtprev[...] = vt_new
        project_next()

        qm = _stack_heads(p_scr[rows, C_MQ:C_MQ + GROUP_W] * QK_SCALE).astype(bf16)
        sm = _dot_nt(mkb[...], qm)
        mm = jnp.max(sm, axis=0, keepdims=True)
        em = jnp.exp(sm - mm)
        dm = jnp.sum(em, axis=0, keepdims=True)
        odt = _dot(mvtb[...], em.astype(bf16)) * (1.0 / dm)
        low = lax.broadcasted_iota(jnp.int32, (SUB, LANES), 1) < HEAD_D
        halves = []
        for pair in range(N_HEADS // 2):
            blk_rows = odt[LANES * pair:LANES * (pair + 1), :]
            even = blk_rows[:, SUB * (2 * pair):SUB * (2 * pair + 1)].T
            odd = blk_rows[:, SUB * (2 * pair + 1):SUB * (2 * pair + 2)].T
            halves.append(jnp.where(low, even, odd))
        od = jnp.concatenate(halves, axis=1)
        mix_scr[rows, 3 * GROUP_W:4 * GROUP_W] = (od * _silu(p_scr[rows, C_MZ:C_MZ + GROUP_W])).astype(bf16)

    for j in range(nsub):
        sub_tile(j)
    while pending:
        project_next()

    mo = _dot(mix_scr[...], woutb[...])
    y_ref[0] = x_prev[...] + _rmsnorm(mo, gpost_ref[...])
    x_prev[...] = x

    @pl.when(t == nt - 1)
    def _sequence_done():
        conv_out[0, 0:1, :] = cbuf[1:2, :]
        conv_out[0, 1:2, :] = cbuf[0:1, :]
        for src, dst in ((kprev, swak_out), (vprev, swav_out)):
            for r in range(GROUPS):
                kv_nat[pl.ds(r, SUBLANES, stride=GLA_BLOCK), :] = src[SUBLANES * r:SUBLANES * (r + 1), :]
            dst[0] = kv_nat[...].T
        st = st_scr[...]
        for hh in range(N_HEADS):
            gla_out[0, hh] = st[HEAD_D * hh:HEAD_D * (hh + 1), HEAD_D * hh:HEAD_D * (hh + 1)].T


def _layer_spec(shape, l):
    nd = len(shape)
    return pl.BlockSpec((None,) + tuple(shape), lambda s, _l=l, _nd=nd: (_l,) + (0,) * _nd)


def _const_spec(shape):
    nd = len(shape)
    return pl.BlockSpec(shape, lambda s, _nd=nd: (0,) * _nd)


def _prompt_layer(l, x, memp, cos, sin, gpre, gpost, wt, wout, wmem, convw, wg, bg, gnorm, sinks, jmat,
                  lvl, swab):
    B, L, _ = x.shape
    assert L % TL == 0
    nt = L // TL
    n_tiles = B * nt

    def projected(s):
        tile = jnp.minimum(s, n_tiles - 1)
        return tile // nt, tile % nt

    def mixed(s):
        tile = jnp.maximum(s - 1, 0)
        return tile // nt, tile % nt

    def per_sequence(shape):
        nd = len(shape) - 1
        return pl.BlockSpec(shape, lambda s, _nd=nd: (mixed(s)[0],) + (0,) * _nd)

    out_shape = (
        jax.ShapeDtypeStruct((B, L, D_MODEL), f32),
        jax.ShapeDtypeStruct((B, CONV_W - 1, GROUP_W), f32),
        jax.ShapeDtypeStruct((B, N_HEADS, HEAD_D, HEAD_D), f32),
        jax.ShapeDtypeStruct((B, SWA_KV_W, WINDOW), f32),
        jax.ShapeDtypeStruct((B, SWA_KV_W, WINDOW), f32),
        jax.ShapeDtypeStruct((B, GROUP_W, N_MEM), f32),
        jax.ShapeDtypeStruct((B, GROUP_W, N_MEM), f32),
    )
    in_specs = [
        pl.BlockSpec(memory_space=pltpu.SMEM),
        pl.BlockSpec((1, TL, D_MODEL), lambda s: projected(s) + (0,)),
        per_sequence((1, N_MEM, D_MODEL)),
        pl.BlockSpec((TL, LANES), lambda s: (mixed(s)[1], 0)),
        pl.BlockSpec((TL, LANES), lambda s: (mixed(s)[1], 0)),
        _layer_spec((1, D_MODEL), l), _layer_spec((1, D_MODEL), l),
        _layer_spec((IN_WIDTH, D_MODEL), l),
        _layer_spec((D_MODEL, D_MODEL), l), _layer_spec((D_MODEL, 2 * GROUP_W), l),
        _layer_spec((CONV_W, GROUP_W), l), _layer_spec((GATE_RANK, GROUP_W), l), _layer_spec((1, GROUP_W), l),
        _layer_spec((1, GROUP_W), l), _const_spec((GROUP_W, GROUP_W)),
        _const_spec((len(LEVEL_GROUPS), N_HEADS * SUB, SUB)), _const_spec((2, 2 * SUB, N_HEADS * SUB)),
    ]
    out_specs = (
        pl.BlockSpec((1, TL, D_MODEL), lambda s: mixed(s) + (0,)),
        per_sequence((1, CONV_W - 1, GROUP_W)),
        per_sequence((1, N_HEADS, HEAD_D, HEAD_D)),
        per_sequence((1, SWA_KV_W, WINDOW)),
        per_sequence((1, SWA_KV_W, WINDOW)),
        per_sequence((1, GROUP_W, N_MEM)),
        per_sequence((1, GROUP_W, N_MEM)),
    )
    scratch = [
        pltpu.VMEM((2, TL, NP), f32),
        pltpu.VMEM((TL, D_MODEL), f32),
        pltpu.VMEM((D_MODEL, D_MODEL), bf16),
        pltpu.VMEM((TL, D_MODEL), bf16),
        pltpu.VMEM((SUBLANES, GROUP_W), f32),
        pltpu.VMEM((GROUP_W, GROUP_W), f32),
        pltpu.VMEM((SUB, SWA_KV_W), f32),
        pltpu.VMEM((SUB, SWA_KV_W), f32),
        pltpu.VMEM((SWA_KV_W, SUB), f32),
        pltpu.VMEM((N_MEM, GROUP_W), bf16),
        pltpu.VMEM((GROUP_W, N_MEM), bf16),
        pltpu.VMEM((WINDOW, SWA_KV_W), f32),
    ]
    return pl.pallas_call(
        functools.partial(_prompt_layer_kernel, nt),
        grid=(n_tiles + 1,),
        in_specs=in_specs,
        out_specs=out_specs,
        out_shape=out_shape,
        scratch_shapes=scratch,
        compiler_params=pltpu.CompilerParams(dimension_semantics=("arbitrary",),
                                             vmem_limit_bytes=VMEM_LIMIT_BYTES),
        name="prompt_layer",
    )(sinks[l], x, memp, cos, sin, gpre, gpost, wt, wout, wmem, convw, wg, bg, gnorm, jmat, lvl, swab)


def _subtile_constants():
    row = np.arange(SUB)
    tok = (row % SUBLANES) * GLA_BLOCK + row // SUBLANES
    blk = row % SUBLANES
    levels = []
    for group in LEVEL_GROUPS:
        g = blk // group
        sel = (g[:, None] == g[None, :] + 1) & (g[:, None] % 2 == 1)
        levels.append(np.tile(sel, (N_HEADS, 1)))
    key_tok = np.concatenate([tok - SUB, tok])
    valid = (key_tok[None, :] <= tok[:, None]) & (key_tok[None, :] > tok[:, None] - WINDOW)
    bias = [np.where(valid, 0.0, NEG), np.where(valid & (key_tok[None, :] >= 0), 0.0, NEG)]
    bias = np.stack([np.tile(b, (N_HEADS, 1)).T for b in bias])
    return jnp.asarray(np.stack(levels), dtype=f32), jnp.asarray(bias, dtype=f32)


def _to_kernel_order(x, axis):
    shp = x.shape
    n = shp[axis]
    x = x.reshape(shp[:axis] + (n // SUB, SUBLANES, GLA_BLOCK) + shp[axis + 1:])
    return jnp.swapaxes(x, axis + 1, axis + 2).reshape(shp)


def _from_kernel_order(x, axis):
    shp = x.shape
    n = shp[axis]
    x = x.reshape(shp[:axis] + (n // SUB, GLA_BLOCK, SUBLANES) + shp[axis + 1:])
    return jnp.swapaxes(x, axis + 1, axis + 2).reshape(shp)


def _sample_kernel(seq_len, group, sinks_ref, x_hbm, cos_ref, sin_ref, convb_ref, sgla_ref, kc_ref, vc_ref,
                   mkc_ref, mvc_ref, gpre_ref, gpost_ref, wt_ref, wout_ref, convw_ref, wg_ref, bg_ref,
                   gnorm_ref, j_ref, place_ref,
                   y_hbm, conv_out, gla_out, swak_out, swav_out,
                   hs, p_scr, mix_scr, woutb, sem):
    l = pl.program_id(0)
    g_step = pl.program_id(1)
    n_layers = pl.num_programs(0)
    n_steps = pl.num_programs(1)
    n_tok = hs.shape[0]
    n = group * seq_len
    chunks = [slice(c * TL, (c + 1) * TL) for c in range(n_tok // TL)]

    @pl.when((l == 0) & (g_step == 0))
    def _load():
        cp = pltpu.make_async_copy(x_hbm, hs, sem.at[0])
        cp.start()
        cp.wait()

    @pl.when(g_step == 0)
    def _project_all():
        woutb[...] = wout_ref[...].astype(bf16)
        for rows in chunks:
            h = _rmsnorm(hs[rows, :], gpre_ref[...]).astype(bf16)
            _project(h, wt_ref, p_scr, rows)

    rows = pl.ds(pl.multiple_of(g_step * n, n), n)
    row_t = lax.broadcasted_iota(jnp.int32, (n, GROUP_W), 0) % seq_len

    u = p_scr[rows, C_CC:C_CC + GROUP_W] * p_scr[rows, C_CX:C_CX + GROUP_W]
    hist = convb_ref[...]
    um1 = jnp.where(row_t >= 1, pltpu.roll(u, 1, 0), pltpu.roll(hist, n - 1, 0))
    um2 = jnp.where(row_t >= 2, pltpu.roll(u, 2, 0), hist)
    cy = convw_ref[0:1, :] * um2 + convw_ref[1:2, :] * um1 + convw_ref[2:3, :] * u
    a_out = p_scr[rows, C_CB:C_CB + GROUP_W] * cy * _silu(p_scr[rows, C_CZ:C_CZ + GROUP_W])
    mix_scr[rows, 0:GROUP_W] = a_out.astype(bf16)
    for g in range(group):
        conv_out[g] = u[g * seq_len + seq_len - (CONV_W - 1):(g + 1) * seq_len, :]

    qs = p_scr[rows, C_GQ:C_GQ + GROUP_W] * QK_SCALE
    k = p_scr[rows, C_GK:C_GK + GROUP_W]
    v = p_scr[rows, C_GV:C_GV + GROUP_W]
    la = _gate_log_decay(p_scr[rows, C_LR:C_LR + GATE_RANK], wg_ref, bg_ref)
    bc = _block_cumsum(la, seq_len)
    o = _dot((qs * k).astype(bf16), j_ref[...]) * v
    for d in range(1, seq_len):
        dec = jnp.exp(jnp.where(row_t >= d, bc - pltpu.roll(bc, d, 0), NEG))
        pw = qs * pltpu.roll(k, d, 0) * dec
        o = o + _dot(pw.astype(bf16), j_ref[...]) * pltpu.roll(v, d, 0)
    qd = qs * jnp.exp(bc)
    tot = [bc[(g + 1) * seq_len - 1:(g + 1) * seq_len, :] for g in range(group)]
    kd = k * jnp.exp(jnp.concatenate(
        [tot[g] - bc[g * seq_len:(g + 1) * seq_len, :] for g in range(group)], axis=0))
    low = lax.broadcasted_iota(jnp.int32, (seq_len, LANES), 1) < HEAD_D
    low_st = lax.broadcasted_iota(jnp.int32, (HEAD_D, LANES), 1) < HEAD_D
    o_inter = []
    for g in range(group):
        rs = slice(g * seq_len, (g + 1) * seq_len)
        alpha = jnp.exp(tot[g])
        per_pair = []
        for pair in range(N_HEADS // 2):
            ls = slice(pair * LANES, (pair + 1) * LANES)
            st0 = sgla_ref[g, pair]
            q_pair = qd[rs, ls]
            q2 = jnp.concatenate([jnp.where(low, q_pair, 0.0), jnp.where(low, 0.0, q_pair)], axis=0)
            o2 = _dot_nt(q2.astype(bf16), st0.astype(bf16))
            per_pair.append(jnp.concatenate([o2[0:seq_len], o2[seq_len:2 * seq_len]], axis=1))
            kd_pair = kd[rs, ls].astype(bf16)
            upd = [_dot_tn(v[rs, (2 * pair + hh) * HEAD_D:(2 * pair + hh + 1) * HEAD_D].astype(bf16), kd_pair)
                   for hh in range(2)]
            new_pair = st0 * alpha[:, ls] + jnp.where(low_st, upd[0], upd[1])
            gla_out[g, 2 * pair] = new_pair[:, 0:HEAD_D]
            gla_out[g, 2 * pair + 1] = new_pair[:, HEAD_D:LANES]
        o_inter.append(jnp.concatenate(per_pair, axis=1))
    o = o + jnp.concatenate(o_inter, axis=0)
    b_out = _gla_norm_gate(o, p_scr[rows, C_GZ:C_GZ + GROUP_W], gnorm_ref, j_ref)
    mix_scr[rows, GROUP_W:2 * GROUP_W] = b_out.astype(bf16)

    cos = cos_ref[...]
    sin = sin_ref[...]
    q01 = _rope(p_scr[rows, C_SQ:C_SQ + LANES], cos, sin) * QK_SCALE
    q23 = _rope(p_scr[rows, C_SQ + LANES:C_SQ + 2 * LANES], cos, sin) * QK_SCALE
    kr = _rope(p_scr[rows, C_SK:C_SK + SWA_KV_W], cos, sin)
    vx = p_scr[rows, C_SV:C_SV + SWA_KV_W]
    qm = p_scr[rows, C_MQ:C_MQ + GROUP_W] * QK_SCALE
    hq = N_HEADS * seq_len
    sink_seq = _sink_column([sinks_ref[l, hh] for hh in range(N_HEADS)], seq_len)
    sink = jnp.concatenate([sink_seq] * group, axis=0)
    hq_all = group * hq
    qrow = lax.broadcasted_iota(jnp.int32, (hq_all, WINDOW), 0) % seq_len
    ccol = lax.broadcasted_iota(jnp.int32, (hq_all, WINDOW), 1)
    cache_valid = ccol > qrow
    nrow = lax.broadcasted_iota(jnp.int32, (hq_all, seq_len), 0) % seq_len
    ncol = lax.broadcasted_iota(jnp.int32, (hq_all, seq_len), 1)
    new_valid = ncol <= nrow
    keep_old = lax.broadcasted_iota(jnp.int32, (SWA_KV_W, WINDOW), 1) < WINDOW - seq_len

    def exact_split(a):
        hi = a.astype(bf16)
        r1 = a - hi.astype(f32)
        mid = r1.astype(bf16)
        lo = (r1 - mid.astype(f32)).astype(bf16)
        return jnp.concatenate([hi, mid, lo], axis=0)

    def shifted_cache(old_t, new_rows):
        placed = _dot_tn(exact_split(new_rows), place_ref[...])
        return jnp.where(keep_old, pltpu.roll(old_t, WINDOW - seq_len, 1), placed)

    seqs = [slice(g * seq_len, (g + 1) * seq_len) for g in range(group)]
    stk = [slice(g * hq, (g + 1) * hq) for g in range(group)]
    qst = [_stack_swa_q(q01[rs], q23[rs]).astype(bf16) for rs in seqs]
    s_c = jnp.concatenate([_dot(qst[g], kc_ref[g].astype(bf16)) for g in range(group)], axis=0)
    s_n = jnp.concatenate([_dot_nt(qst[g], kr[seqs[g]].astype(bf16)) for g in range(group)], axis=0)
    s_c = jnp.where(cache_valid, s_c, NEG)
    s_n = jnp.where(new_valid, s_n, NEG)
    mx = jnp.maximum(jnp.maximum(jnp.max(s_c, axis=1, keepdims=True), jnp.max(s_n, axis=1, keepdims=True)), sink)
    e_c = jnp.exp(s_c - mx)
    e_n = jnp.exp(s_n - mx)
    den = jnp.sum(e_c, axis=1, keepdims=True) + jnp.sum(e_n, axis=1, keepdims=True) + jnp.exp(sink - mx)
    e_c = e_c.astype(bf16)
    e_n = e_n.astype(bf16)
    ov = jnp.concatenate([_dot_nt(e_c[stk[g]], vc_ref[g].astype(bf16)) + _dot(e_n[stk[g]], vx[seqs[g]].astype(bf16))
                          for g in range(group)], axis=0) * (1.0 / den)
    oc = jnp.concatenate([_unstack_swa(ov[stk[g]], seq_len) for g in range(group)], axis=0)
    for g in range(group):
        swak_out[g] = shifted_cache(kc_ref[g], kr[seqs[g]])
        swav_out[g] = shifted_cache(vc_ref[g], vx[seqs[g]])

    sm = jnp.concatenate([_dot(_stack_heads(qm[seqs[g]]).astype(bf16), mkc_ref[g].astype(bf16))
                          for g in range(group)], axis=0)
    mm = jnp.max(sm, axis=1, keepdims=True)
    em = jnp.exp(sm - mm)
    dm = jnp.sum(em, axis=1, keepdims=True)
    em = em.astype(bf16)
    odv = jnp.concatenate([_dot_nt(em[stk[g]], mvc_ref[g].astype(bf16)) for g in range(group)], axis=0) * (1.0 / dm)
    od = jnp.concatenate([_unstack_heads(odv[stk[g]], seq_len) for g in range(group)], axis=0)
    mix_scr[rows, 2 * GROUP_W:3 * GROUP_W] = (oc * _silu(p_scr[rows, C_SZ:C_SZ + GROUP_W])).astype(bf16)
    mix_scr[rows, 3 * GROUP_W:4 * GROUP_W] = (od * _silu(p_scr[rows, C_MZ:C_MZ + GROUP_W])).astype(bf16)

    @pl.when(g_step == n_steps - 1)
    def _residual():
        for rws in chunks:
            mo = _dot(mix_scr[rws, :], woutb[...])
            hs[rws, :] = hs[rws, :] + _rmsnorm(mo, gpost_ref[...])

    @pl.when((l == n_layers - 1) & (g_step == n_steps - 1))
    def _store():
        cp = pltpu.make_async_copy(hs, y_hbm, sem.at[0])
        cp.start()
        cp.wait()


def _sample_layers(x, cos, sin, convb, sgla, kc, vc, mkc, mvc, gpre, gpost, wt, wout, convw, wg, bg, gnorm,
                   sinks, jmat, place, seq_len, group):
    depth, nseq = sgla.shape[0], sgla.shape[1]
    n_tok = nseq * seq_len
    assert nseq % group == 0 and seq_len == SUBLANES and n_tok % TL == 0
    n = group * seq_len

    def per_layer(shape, single_buffer=False):
        nd = len(shape)
        kw = dict(pipeline_mode=pl.Buffered(1)) if single_buffer else {}
        return pl.BlockSpec((None,) + tuple(shape), lambda l, g, _nd=nd: (l,) + (0,) * _nd, **kw)

    def per_group(shape):
        nd = len(shape) - 1
        return pl.BlockSpec((None,) + tuple(shape), lambda l, g, _nd=nd: (l, g) + (0,) * _nd)

    def const(shape):
        nd = len(shape)
        return pl.BlockSpec(shape, lambda l, g, _nd=nd: (0,) * _nd)

    in_specs = [
        pl.BlockSpec(memory_space=pltpu.SMEM),
        pl.BlockSpec(memory_space=pl.ANY),
        const((n, LANES)), const((n, LANES)),
        per_group((n, GROUP_W)),
        per_group((group, N_HEADS // 2, HEAD_D, LANES)),
        per_group((group, SWA_KV_W, WINDOW)), per_group((group, SWA_KV_W, WINDOW)),
        per_group((group, GROUP_W, N_MEM)), per_group((group, GROUP_W, N_MEM)),
        per_layer((1, D_MODEL)), per_layer((1, D_MODEL)),
        per_layer((IN_WIDTH, D_MODEL), True),
        per_layer((D_MODEL, D_MODEL), True),
        per_layer((CONV_W, GROUP_W)), per_layer((GATE_RANK, GROUP_W)), per_layer((1, GROUP_W)), per_layer((1, GROUP_W)),
        const((GROUP_W, GROUP_W)), const((3 * seq_len, WINDOW)),
    ]
    out_shape = (
        jax.ShapeDtypeStruct((n_tok, D_MODEL), f32),
        jax.ShapeDtypeStruct((depth, nseq, CONV_W - 1, GROUP_W), f32),
        jax.ShapeDtypeStruct((depth, nseq, N_HEADS, HEAD_D, HEAD_D), f32),
        jax.ShapeDtypeStruct((depth, nseq, SWA_KV_W, WINDOW), f32),
        jax.ShapeDtypeStruct((depth, nseq, SWA_KV_W, WINDOW), f32),
    )
    out_specs = (
        pl.BlockSpec(memory_space=pl.ANY),
        per_group((group, CONV_W - 1, GROUP_W)),
        per_group((group, N_HEADS, HEAD_D, HEAD_D)),
        per_group((group, SWA_KV_W, WINDOW)), per_group((group, SWA_KV_W, WINDOW)),
    )
    scratch = [
        pltpu.VMEM((n_tok, D_MODEL), f32),
        pltpu.VMEM((n_tok, NP), f32),
        pltpu.VMEM((n_tok, D_MODEL), bf16),
        pltpu.VMEM((D_MODEL, D_MODEL), bf16),
        pltpu.SemaphoreType.DMA((1,)),
    ]
    return pl.pallas_call(
        functools.partial(_sample_kernel, seq_len, group),
        grid=(depth, nseq // group),
        in_specs=in_specs,
        out_specs=out_specs,
        out_shape=out_shape,
        scratch_shapes=scratch,
        compiler_params=pltpu.CompilerParams(dimension_semantics=("arbitrary", "arbitrary"),
                                             vmem_limit_bytes=VMEM_LIMIT_BYTES),
        name="sample_layers",
    )(sinks, x, cos, sin, convb, sgla, kc, vc, mkc, mvc, gpre, gpost, wt, wout, convw, wg, bg, gnorm,
      jmat, place)


def _rope_tables(pos):
    half = HEAD_D // 2
    inv = np.power(ROPE_THETA, -np.arange(half, dtype=np.float64) / half)
    ang = np.asarray(pos, np.float64)[:, None] * inv[None, :]
    cos = np.tile(np.cos(ang), (1, LANES // half))
    sin = np.sin(ang)
    sin_signed = np.tile(np.concatenate([-sin, sin], axis=1), (1, LANES // HEAD_D))
    return cos.astype(np.float32), sin_signed.astype(np.float32)


def _rows_to_kernel_order(a):
    n, w = a.shape
    return a.reshape(n // SUB, SUBLANES, GLA_BLOCK, w).swapaxes(1, 2).reshape(n, w)


def _feature_major(cache):
    d, s, p, h, e = cache.shape
    return jnp.transpose(cache, (0, 1, 3, 4, 2)).reshape(d, s, h * e, p)


def _pair_states(state):
    dd, s, h, e, _ = state.shape
    a = jnp.swapaxes(state, -1, -2).reshape(dd, s, h // 2, 2, e, e)
    return jnp.transpose(a, (0, 1, 2, 4, 3, 5)).reshape(dd, s, h // 2, e, 2 * e)


def _position_major(cache_t, heads):
    d, s, he, p = cache_t.shape
    return jnp.transpose(cache_t.reshape(d, s, heads, he // heads, p), (0, 1, 4, 2, 3))


def kernel(x_prompt, x_sample, state_conv, state_gla, cache_swa_k, cache_swa_v, cache_mem_k, cache_mem_v,
           mem_prompt, norm_pre, norm_post, w_in, conv_w, gla_w_gate, gla_b_gate, gla_norm, swa_sinks,
           w_mem_kv, w_out):
    depth = w_in.shape[0]
    B, L, _ = x_prompt.shape
    nseq, seq_len, _ = x_sample.shape
    group = 8

    wt = jnp.swapaxes(w_in, 1, 2).astype(bf16)
    wout = w_out
    wmem = w_mem_kv
    wg = gla_w_gate.astype(bf16)
    head_id = np.arange(GROUP_W) // HEAD_D
    jmat = jnp.asarray(head_id[:, None] == head_id[None, :], dtype=bf16)
    lvl, swab = _subtile_constants()
    place = np.zeros((3 * seq_len, WINDOW), np.float32)
    for piece in range(3):
        place[piece * seq_len + np.arange(seq_len), WINDOW - seq_len + np.arange(seq_len)] = 1.0
    place = jnp.asarray(place, dtype=bf16)
    sinks = swa_sinks.astype(f32)
    gpre = norm_pre[:, None, :]
    gpost = norm_post[:, None, :]
    bg = gla_b_gate[:, None, :]
    gn = gla_norm[:, None, :]

    cos_p, sin_p = (jnp.asarray(_rows_to_kernel_order(a)) for a in _rope_tables(np.arange(L)))
    cos_s, sin_s = (jnp.asarray(np.tile(a, (group, 1))) for a in _rope_tables(PAST_LEN + np.arange(seq_len)))

    convb = jnp.pad(state_conv, ((0, 0), (0, 0), (0, seq_len - (CONV_W - 1)), (0, 0)))
    convb = convb.reshape(depth, nseq * seq_len, GROUP_W)
    ys, conv_s, gla_s, swak_s, swav_s = _sample_layers(
        x_sample.reshape(nseq * seq_len, D_MODEL), cos_s, sin_s, convb, _pair_states(state_gla),
        _feature_major(cache_swa_k), _feature_major(cache_swa_v),
        _feature_major(cache_mem_k), _feature_major(cache_mem_v),
        gpre, gpost, wt, wout, conv_w, wg, bg, gn, sinks, jmat, place, seq_len, group)

    hp = _to_kernel_order(x_prompt, 1)
    outs_p = [[] for _ in range(6)]
    for l in range(depth):
        res = _prompt_layer(l, hp, mem_prompt, cos_p, sin_p, gpre, gpost, wt, wout, wmem, conv_w, wg, bg, gn,
                            sinks, jmat, lvl, swab)
        hp = res[0]
        for i in range(6):
            outs_p[i].append(res[i + 1])

    return (_from_kernel_order(hp, 1), ys.reshape(nseq, seq_len, D_MODEL),
            jnp.stack(outs_p[0]), jnp.stack(outs_p[1]),
            _position_major(jnp.stack(outs_p[2]), 2), _position_major(jnp.stack(outs_p[3]), 2),
            _position_major(jnp.stack(outs_p[4]), N_HEADS), _position_major(jnp.stack(outs_p[5]), N_HEADS),
            conv_s, jnp.swapaxes(gla_s, -1, -2), _position_major(swak_s, 2), _position_major(swav_s, 2))
```

```python
import functools

import jax
import jax.numpy as jnp
import numpy as np
from jax import lax
from jax.experimental import pallas as pl
from jax.experimental.pallas import tpu as pltpu

f32 = jnp.float32
bf16 = jnp.bfloat16

D_MODEL = 1024
GROUP_W = 256
HEAD_D = 64
N_HEADS = 4
SWA_KV_W = 128
N_MEM = 256
WINDOW = 128
CONV_W = 3
GATE_RANK = 16
GATE_NORM = 16.0
GLA_BLOCK = 16
ROPE_THETA = 10000.0
PAST_LEN = 8192
EPS = 1e-6
NEG = -1e30
QK_SCALE = HEAD_D ** -0.5

LANES = 128
SUBLANES = 8
VMEM_LIMIT_BYTES = 56 * 1024 * 1024

C_CX, C_CB, C_CC, C_CZ = 0, 256, 512, 768
C_GQ, C_GK, C_GV, C_GZ = 1024, 1280, 1536, 1792
C_SQ, C_SK, C_SV, C_SZ = 2048, 2304, 2432, 2560
C_MQ, C_MZ = 2816, 3072
C_LR = 3328
NP = 3456
IN_WIDTH = 3344
_O_GLR, _O_GZ = 1792, 1808

SUB = 128
TL = 512


def _dot(a, b):
    return jnp.dot(a, b, preferred_element_type=f32)


def _dot_nt(a, b):
    return lax.dot_general(a, b, (((1,), (1,)), ((), ())), preferred_element_type=f32)


def _dot_tn(a, b):
    return lax.dot_general(a, b, (((0,), (0,)), ((), ())), preferred_element_type=f32)


def _rmsnorm(x, g):
    return x * lax.rsqrt(jnp.mean(x * x, axis=-1, keepdims=True) + EPS) * g


def _silu(z):
    return z * (0.5 + 0.5 * jnp.tanh(0.5 * z))


def _log_sigmoid(x):
    return jnp.minimum(x, 0.0) - jnp.log1p(jnp.exp(-jnp.abs(x)))


PROJ_PIECES = tuple([(c, c + 256, c) for c in range(0, _O_GLR, 256)]
                    + [(_O_GZ + c, _O_GZ + c + 256, _O_GLR + c) for c in range(0, IN_WIDTH - _O_GZ, 256)]
                    + [(_O_GLR, _O_GZ, C_LR)])


def _project_piece(h, wt_ref, p_ref, rows, piece):
    w0, w1, c0 = PROJ_PIECES[piece]
    p_ref[rows, c0:c0 + (w1 - w0)] = _dot_nt(h, wt_ref[w0:w1, :])


def _project(h, wt_ref, p_ref, rows):
    for piece in range(len(PROJ_PIECES)):
        _project_piece(h, wt_ref, p_ref, rows, piece)


def _head_sum(x, j_ref):
    hi = x.astype(bf16)
    lo = (x - hi.astype(f32)).astype(bf16)
    return _dot(hi, j_ref[...]) + _dot(lo, j_ref[...])


def _rope(x, cos, sin_signed):
    lane = lax.broadcasted_iota(jnp.int32, x.shape, 1)
    swapped = jnp.where((lane % HEAD_D) < HEAD_D // 2,
                        pltpu.roll(x, LANES - HEAD_D // 2, 1), pltpu.roll(x, HEAD_D // 2, 1))
    return x * cos + swapped * sin_signed


def _stack_heads(x):
    lane_head = lax.broadcasted_iota(jnp.int32, x.shape, 1) // HEAD_D
    return jnp.concatenate([jnp.where(lane_head == h, x, 0.0) for h in range(N_HEADS)], axis=0)


def _unstack_heads(o, n):
    lane_head = lax.broadcasted_iota(jnp.int32, (n, GROUP_W), 1) // HEAD_D
    out = o[0:n]
    for h in range(1, N_HEADS):
        out = jnp.where(lane_head == h, o[h * n:(h + 1) * n], out)
    return out


def _stack_swa_q(q01, q23):
    low = lax.broadcasted_iota(jnp.int32, q01.shape, 1) < HEAD_D
    return jnp.concatenate([jnp.where(low, q01, 0.0), jnp.where(low, pltpu.roll(q01, HEAD_D, 1), 0.0),
                            jnp.where(low, 0.0, pltpu.roll(q23, HEAD_D, 1)), jnp.where(low, 0.0, q23)], axis=0)


def _unstack_swa(o, n):
    low = lax.broadcasted_iota(jnp.int32, (n, SWA_KV_W), 1) < HEAD_D
    c01 = jnp.where(low, o[0:n], pltpu.roll(o[n:2 * n], HEAD_D, 1))
    c23 = jnp.where(low, pltpu.roll(o[2 * n:3 * n], HEAD_D, 1), o[3 * n:4 * n])
    return jnp.concatenate([c01, c23], axis=1)


def _sink_column(sinks, n):
    return jnp.concatenate([jnp.full((n, 1), s, f32) for s in sinks], axis=0)


def _block_cumsum(la, block):
    row = lax.broadcasted_iota(jnp.int32, la.shape, 0) % block
    b = la
    s = 1
    while s < block:
        b = b + jnp.where(row >= s, pltpu.roll(b, s, 0), 0.0)
        s *= 2
    return b


def _gate_log_decay(p_lr, wg_ref, bg_ref):
    pre = _dot(p_lr.astype(bf16), wg_ref[...]) + bg_ref[...]
    return _log_sigmoid(pre) * (1.0 / GATE_NORM)


def _gla_norm_gate(o, gz, gnorm_ref, j_ref):
    ms = _head_sum(o * o, j_ref) * (1.0 / HEAD_D)
    return o * lax.rsqrt(ms + EPS) * gnorm_ref[...] * _silu(gz)


GROUPS = SUB // SUBLANES
LEVEL_GROUPS = (1, 2, 4)


def _row_groups(x):
    return [x[SUBLANES * r:SUBLANES * (r + 1), :] for r in range(x.shape[0] // SUBLANES)]


def _prompt_layer_kernel(nt, *refs):
    s = pl.program_id(0)
    pl.when(s == 0)(functools.partial(_prompt_first_step, *refs))
    for parity in range(2):
        pl.when((s > 0) & (s % 2 == parity))(functools.partial(_prompt_layer_step, nt, parity, *refs))


def _prompt_first_step(sinks_ref, x_ref, memp_ref, cos_ref, sin_ref, gpre_ref, gpost_ref, wt_ref,
                       wout_ref, wmem_ref, convw_ref, wg_ref, bg_ref, gnorm_ref, j_ref, lvl_ref, swab_ref,
                       a0, a1, a2, a3, a4, a5,
                       y_ref, conv_out, gla_out, swak_out, swav_out, mk_out, mv_out,
                       p_buf, x_prev, woutb, *unused):
    x = x_ref[0]
    _project(_rmsnorm(x, gpre_ref[...]).astype(bf16), wt_ref, p_buf.at[0], slice(None))
    x_prev[...] = x
    woutb[...] = wout_ref[...].astype(bf16)


def _prompt_layer_step(nt, parity, sinks_ref, x_ref, memp_ref, cos_ref, sin_ref, gpre_ref, gpost_ref, wt_ref,
                       wout_ref, wmem_ref, convw_ref, wg_ref, bg_ref, gnorm_ref, j_ref, lvl_ref, swab_ref,
                       a0, a1, a2, a3, a4, a5,
                       y_ref, conv_out, gla_out, swak_out, swav_out, mk_out, mv_out,
                       p_buf, x_prev, woutb, mix_scr, cbuf, st_scr, kprev, vprev, vtprev, mkb, mvtb, kv_nat):
    s = pl.program_id(0)
    t = jnp.maximum(s - 1, 0) % nt
    nsub = TL // SUB
    p_wr = p_buf.at[parity]
    p_scr = p_buf.at[1 - parity]

    @pl.when(t == 0)
    def _new_sequence():
        cbuf[...] = jnp.zeros((SUBLANES, GROUP_W), f32)
        st_scr[...] = jnp.zeros((GROUP_W, GROUP_W), f32)
        kprev[...] = jnp.zeros((SUB, SWA_KV_W), f32)
        vprev[...] = jnp.zeros((SUB, SWA_KV_W), f32)
        vtprev[...] = jnp.zeros((SWA_KV_W, SUB), f32)
        mkv = _dot(memp_ref[0].astype(bf16), wmem_ref[...].astype(bf16))
        mvt = mkv[:, GROUP_W:2 * GROUP_W].T
        mk_out[0] = mkv[:, 0:GROUP_W].T
        mv_out[0] = mvt
        mkb[...] = mkv[:, 0:GROUP_W].astype(bf16)
        mvtb[...] = mvt.astype(bf16)

    x = x_ref[0]
    h = _rmsnorm(x, gpre_ref[...]).astype(bf16)

    sub8 = lax.broadcasted_iota(jnp.int32, (SUBLANES, GROUP_W), 0)
    pending = list(range(len(PROJ_PIECES)))

    def project_next():
        if pending:
            _project_piece(h, wt_ref, p_wr, slice(None), pending.pop(0))

    def sub_tile(j):
        rows = slice(j * SUB, (j + 1) * SUB)
        project_next()

        u = p_scr[rows, C_CC:C_CC + GROUP_W] * p_scr[rows, C_CX:C_CX + GROUP_W]
        last = SUB - SUBLANES
        prev1 = jnp.where(sub8 == 0, cbuf[0:1, :], pltpu.roll(u[last:SUB, :], 1, 0))
        prev2 = jnp.where(sub8 == 0, cbuf[1:2, :], pltpu.roll(u[last - SUBLANES:last, :], 1, 0))
        um1 = jnp.concatenate([prev1, u[0:last, :]], axis=0)
        um2 = jnp.concatenate([prev2, prev1, u[0:last - SUBLANES, :]], axis=0)
        cy = convw_ref[0:1, :] * um2 + convw_ref[1:2, :] * um1 + convw_ref[2:3, :] * u
        cbuf[0:1, :] = u[SUB - 1:SUB, :]
        cbuf[1:2, :] = u[last - 1:last, :]
        a_out = p_scr[rows, C_CB:C_CB + GROUP_W] * cy * _silu(p_scr[rows, C_CZ:C_CZ + GROUP_W])
        mix_scr[rows, 0:GROUP_W] = a_out.astype(bf16)

        qg = _row_groups(p_scr[rows, C_GQ:C_GQ + GROUP_W] * QK_SCALE)
        k = p_scr[rows, C_GK:C_GK + GROUP_W]
        v = p_scr[rows, C_GV:C_GV + GROUP_W]
        kg = _row_groups(k)
        vg = _row_groups(v)
        lag = _row_groups(_gate_log_decay(p_scr[rows, C_LR:C_LR + GATE_RANK], wg_ref, bg_ref))
        bg_ = [lag[0]]
        for r in range(1, GROUPS):
            bg_.append(bg_[-1] + lag[r])
        tot = bg_[GROUPS - 1]
        pw = []
        for r in range(GROUPS):
            for s in range(r):
                pw.append(qg[r] * kg[s] * jnp.exp(bg_[r] - bg_[s]))
            pw.append(qg[r] * kg[r])
        scores = _dot(jnp.concatenate(pw, axis=0).astype(bf16), j_ref[...])
        project_next()
        og = []
        idx = 0
        for r in range(GROUPS):
            acc = None
            for s in range(r + 1):
                term = scores[SUBLANES * idx:SUBLANES * (idx + 1), :] * vg[s]
                acc = term if acc is None else acc + term
                idx += 1
            og.append(acc)
        o = jnp.concatenate(og, axis=0)

        def decayed(group):
            before = jnp.zeros((SUBLANES, GROUP_W), f32)
            after = jnp.zeros((SUBLANES, GROUP_W), f32)
            for s in range(1, group):
                before = before + jnp.where(sub8 % group >= s, pltpu.roll(tot, s, 0), 0.0)
                after = after + jnp.where(sub8 % group < group - s, pltpu.roll(tot, SUBLANES - s, 0), 0.0)
            qd = jnp.concatenate([qg[r] * jnp.exp(bg_[r] + before) for r in range(GROUPS)], axis=0)
            kd = jnp.concatenate([kg[r] * jnp.exp((tot - bg_[r]) + after) for r in range(GROUPS)], axis=0)
            return qd, kd

        attn = None
        for li, group in enumerate(LEVEL_GROUPS):
            qd, kd = decayed(group)
            s = _dot_nt(_stack_heads(qd).astype(bf16), kd.astype(bf16)) * lvl_ref[li]
            attn = s if attn is None else attn + s
        o = o + _unstack_heads(_dot(attn.astype(bf16), v.astype(bf16)), SUB)
        project_next()
        qd, kd = decayed(SUB // GLA_BLOCK)
        st = st_scr[...]
        o = o + _dot_nt(qd.astype(bf16), st.astype(bf16))
        total = jnp.sum(tot, axis=0, keepdims=True)
        upd = _dot_tn(v.astype(bf16), kd.astype(bf16))
        same_head = (lax.broadcasted_iota(jnp.int32, (GROUP_W, GROUP_W), 0) // HEAD_D
                     == lax.broadcasted_iota(jnp.int32, (GROUP_W, GROUP_W), 1) // HEAD_D)
        st_scr[...] = st * jnp.exp(total) + jnp.where(same_head, upd, 0.0)
        b_out = _gla_norm_gate(o, p_scr[rows, C_GZ:C_GZ + GROUP_W], gnorm_ref, j_ref)
        mix_scr[rows, GROUP_W:2 * GROUP_W] = b_out.astype(bf16)

        cos = cos_ref[rows, :]
        sin = sin_ref[rows, :]
        q01 = _rope(p_scr[rows, C_SQ:C_SQ + LANES], cos, sin) * QK_SCALE
        q23 = _rope(p_scr[rows, C_SQ + LANES:C_SQ + 2 * LANES], cos, sin) * QK_SCALE
        kr = _rope(p_scr[rows, C_SK:C_SK + SWA_KV_W], cos, sin)
        vx = p_scr[rows, C_SV:C_SV + SWA_KV_W]
        k_all = jnp.concatenate([kprev[...], kr], axis=0).astype(bf16)
        vt_new = vx.T
        vt_all = jnp.concatenate([vtprev[...], vt_new], axis=1).astype(bf16)
        first = jnp.where((t * nsub + j) == 0, 1, 0)
        sc = _dot_nt(k_all, _stack_swa_q(q01, q23).astype(bf16)) + swab_ref[first]
        lane_head = lax.broadcasted_iota(jnp.int32, (1, N_HEADS * SUB), 1) // SUB
        sink = jnp.full((1, N_HEADS * SUB), sinks_ref[0], f32)
        for hh in range(1, N_HEADS):
            sink = jnp.where(lane_head == hh, sinks_ref[hh], sink)
        mx = jnp.maximum(jnp.max(sc, axis=0, keepdims=True), sink)
        e = jnp.exp(sc - mx)
        den = jnp.sum(e, axis=0, keepdims=True) + jnp.exp(sink - mx)
        ot = _dot(vt_all, e.astype(bf16)) * (1.0 / den)
        oc = _unstack_swa(jnp.concatenate([ot[:, SUB * hh:SUB * (hh + 1)].T for hh in range(N_HEADS)], axis=0), SUB)
        mix_scr[rows, 2 * GROUP_W:3 * GROUP_W] = (oc * _silu(p_scr[rows, C_SZ:C_SZ + GROUP_W])).astype(bf16)
        kprev[...] = kr
        vprev[...] = vx
        vtprev[...] = vt_new
        project_next()

        qm = _stack_heads(p_scr[rows, C_MQ:C_MQ + GROUP_W] * QK_SCALE).astype(bf16)
        sm = _dot_nt(mkb[...], qm)
        mm = jnp.max(sm, axis=0, keepdims=True)
        em = jnp.exp(sm - mm)
        dm = jnp.sum(em, axis=0, keepdims=True)
        odt = _dot(mvtb[...], em.astype(bf16)) * (1.0 / dm)
        low = lax.broadcasted_iota(jnp.int32, (SUB, LANES), 1) < HEAD_D
        halves = []
        for pair in range(N_HEADS // 2):
            blk_rows = odt[LANES * pair:LANES * (pair + 1), :]
            even = blk_rows[:, SUB * (2 * pair):SUB * (2 * pair + 1)].T
            odd = blk_rows[:, SUB * (2 * pair + 1):SUB * (2 * pair + 2)].T
            halves.append(jnp.where(low, even, odd))
        od = jnp.concatenate(halves, axis=1)
        mix_scr[rows, 3 * GROUP_W:4 * GROUP_W] = (od * _silu(p_scr[rows, C_MZ:C_MZ + GROUP_W])).astype(bf16)

    for j in range(nsub):
        sub_tile(j)
    while pending:
        project_next()

    mo = _dot(mix_scr[...], woutb[...])
    y_ref[0] = x_prev[...] + _rmsnorm(mo, gpost_ref[...])
    x_prev[...] = x

    @pl.when(t == nt - 1)
    def _sequence_done():
        conv_out[0, 0:1, :] = cbuf[1:2, :]
        conv_out[0, 1:2, :] = cbuf[0:1, :]
        for src, dst in ((kprev, swak_out), (vprev, swav_out)):
            for r in range(GROUPS):
                kv_nat[pl.ds(r, SUBLANES, stride=GLA_BLOCK), :] = src[SUBLANES * r:SUBLANES * (r + 1), :]
            dst[0] = kv_nat[...].T
        st = st_scr[...]
        for hh in range(N_HEADS):
            gla_out[0, hh] = st[HEAD_D * hh:HEAD_D * (hh + 1), HEAD_D * hh:HEAD_D * (hh + 1)].T


def _layer_spec(shape, l):
    nd = len(shape)
    return pl.BlockSpec((None,) + tuple(shape), lambda s, _l=l, _nd=nd: (_l,) + (0,) * _nd)


def _const_spec(shape):
    nd = len(shape)
    return pl.BlockSpec(shape, lambda s, _nd=nd: (0,) * _nd)


def _prompt_layer(l, x, memp, cos, sin, gpre, gpost, wt, wout, wmem, convw, wg, bg, gnorm, sinks, jmat,
                  lvl, swab, stacked):
    B, L, _ = x.shape
    assert L % TL == 0
    nt = L // TL
    n_tiles = B * nt

    def projected(s):
        tile = jnp.minimum(s, n_tiles - 1)
        return tile // nt, tile % nt

    def mixed(s):
        tile = jnp.maximum(s - 1, 0)
        return tile // nt, tile % nt

    def per_sequence(shape):
        nd = len(shape) - 1
        return pl.BlockSpec(shape, lambda s, _nd=nd: (mixed(s)[0],) + (0,) * _nd)

    def per_layer_sequence(shape):
        nd = len(shape) - 1
        return pl.BlockSpec((None,) + tuple(shape), lambda s, _nd=nd: (l, mixed(s)[0]) + (0,) * _nd)

    out_shape = (jax.ShapeDtypeStruct((B, L, D_MODEL), f32),) + tuple(
        jax.ShapeDtypeStruct(a.shape, a.dtype) for a in stacked)
    in_specs = [
        pl.BlockSpec(memory_space=pltpu.SMEM),
        pl.BlockSpec((1, TL, D_MODEL), lambda s: projected(s) + (0,)),
        per_sequence((1, N_MEM, D_MODEL)),
        pl.BlockSpec((TL, LANES), lambda s: (mixed(s)[1], 0)),
        pl.BlockSpec((TL, LANES), lambda s: (mixed(s)[1], 0)),
        _layer_spec((1, D_MODEL), l), _layer_spec((1, D_MODEL), l),
        _layer_spec((IN_WIDTH, D_MODEL), l),
        _layer_spec((D_MODEL, D_MODEL), l), _layer_spec((D_MODEL, 2 * GROUP_W), l),
        _layer_spec((CONV_W, GROUP_W), l), _layer_spec((GATE_RANK, GROUP_W), l), _layer_spec((1, GROUP_W), l),
        _layer_spec((1, GROUP_W), l), _const_spec((GROUP_W, GROUP_W)),
        _const_spec((len(LEVEL_GROUPS), N_HEADS * SUB, SUB)), _const_spec((2, 2 * SUB, N_HEADS * SUB)),
    ] + [pl.BlockSpec(memory_space=pl.ANY)] * len(stacked)
    first_alias = len(in_specs) - len(stacked)
    out_specs = (
        pl.BlockSpec((1, TL, D_MODEL), lambda s: mixed(s) + (0,)),
        per_layer_sequence((1, CONV_W - 1, GROUP_W)),
        per_layer_sequence((1, N_HEADS, HEAD_D, HEAD_D)),
        per_layer_sequence((1, SWA_KV_W, WINDOW)),
        per_layer_sequence((1, SWA_KV_W, WINDOW)),
        per_layer_sequence((1, GROUP_W, N_MEM)),
        per_layer_sequence((1, GROUP_W, N_MEM)),
    )
    scratch = [
        pltpu.VMEM((2, TL, NP), f32),
        pltpu.VMEM((TL, D_MODEL), f32),
        pltpu.VMEM((D_MODEL, D_MODEL), bf16),
        pltpu.VMEM((TL, D_MODEL), bf16),
        pltpu.VMEM((SUBLANES, GROUP_W), f32),
        pltpu.VMEM((GROUP_W, GROUP_W), f32),
        pltpu.VMEM((SUB, SWA_KV_W), f32),
        pltpu.VMEM((SUB, SWA_KV_W), f32),
        pltpu.VMEM((SWA_KV_W, SUB), f32),
        pltpu.VMEM((N_MEM, GROUP_W), bf16),
        pltpu.VMEM((GROUP_W, N_MEM), bf16),
        pltpu.VMEM((WINDOW, SWA_KV_W), f32),
    ]
    return pl.pallas_call(
        functools.partial(_prompt_layer_kernel, nt),
        grid=(n_tiles + 1,),
        in_specs=in_specs,
        out_specs=out_specs,
        out_shape=out_shape,
        scratch_shapes=scratch,
        input_output_aliases={first_alias + i: 1 + i for i in range(len(stacked))},
        compiler_params=pltpu.CompilerParams(dimension_semantics=("arbitrary",),
                                             vmem_limit_bytes=VMEM_LIMIT_BYTES),
        name="prompt_layer",
    )(sinks[l], x, memp, cos, sin, gpre, gpost, wt, wout, wmem, convw, wg, bg, gnorm, jmat, lvl, swab, *stacked)


def _subtile_constants():
    row = np.arange(SUB)
    tok = (row % SUBLANES) * GLA_BLOCK + row // SUBLANES
    blk = row % SUBLANES
    levels = []
    for group in LEVEL_GROUPS:
        g = blk // group
        sel = (g[:, None] == g[None, :] + 1) & (g[:, None] % 2 == 1)
        levels.append(np.tile(sel, (N_HEADS, 1)))
    key_tok = np.concatenate([tok - SUB, tok])
    valid = (key_tok[None, :] <= tok[:, None]) & (key_tok[None, :] > tok[:, None] - WINDOW)
    bias = [np.where(valid, 0.0, NEG), np.where(valid & (key_tok[None, :] >= 0), 0.0, NEG)]
    bias = np.stack([np.tile(b, (N_HEADS, 1)).T for b in bias])
    return jnp.asarray(np.stack(levels), dtype=f32), jnp.asarray(bias, dtype=f32)


def _to_kernel_order(x, axis):
    shp = x.shape
    n = shp[axis]
    x = x.reshape(shp[:axis] + (n // SUB, SUBLANES, GLA_BLOCK) + shp[axis + 1:])
    return jnp.swapaxes(x, axis + 1, axis + 2).reshape(shp)


def _from_kernel_order(x, axis):
    shp = x.shape
    n = shp[axis]
    x = x.reshape(shp[:axis] + (n // SUB, GLA_BLOCK, SUBLANES) + shp[axis + 1:])
    return jnp.swapaxes(x, axis + 1, axis + 2).reshape(shp)


def _sample_kernel(seq_len, group, sinks_ref, x_hbm, cos_ref, sin_ref, convb_ref, sgla_ref, kc_ref, vc_ref,
                   mkc_ref, mvc_ref, gpre_ref, gpost_ref, wt_ref, wout_ref, convw_ref, wg_ref, bg_ref,
                   gnorm_ref, j_ref, place_ref,
                   y_hbm, conv_out, gla_out, swak_out, swav_out,
                   hs, p_scr, mix_scr, woutb, sem):
    l = pl.program_id(0)
    g_step = pl.program_id(1)
    n_layers = pl.num_programs(0)
    n_steps = pl.num_programs(1)
    n_tok = hs.shape[0]
    n = group * seq_len
    chunks = [slice(c * TL, (c + 1) * TL) for c in range(n_tok // TL)]

    @pl.when((l == 0) & (g_step == 0))
    def _load():
        cp = pltpu.make_async_copy(x_hbm, hs, sem.at[0])
        cp.start()
        cp.wait()

    @pl.when(g_step == 0)
    def _project_all():
        woutb[...] = wout_ref[...].astype(bf16)
        for rows in chunks:
            h = _rmsnorm(hs[rows, :], gpre_ref[...]).astype(bf16)
            _project(h, wt_ref, p_scr, rows)

    rows = pl.ds(pl.multiple_of(g_step * n, n), n)
    row_t = lax.broadcasted_iota(jnp.int32, (n, GROUP_W), 0) % seq_len

    u = p_scr[rows, C_CC:C_CC + GROUP_W] * p_scr[rows, C_CX:C_CX + GROUP_W]
    hist = convb_ref[...]
    um1 = jnp.where(row_t >= 1, pltpu.roll(u, 1, 0), pltpu.roll(hist, n - 1, 0))
    um2 = jnp.where(row_t >= 2, pltpu.roll(u, 2, 0), hist)
    cy = convw_ref[0:1, :] * um2 + convw_ref[1:2, :] * um1 + convw_ref[2:3, :] * u
    a_out = p_scr[rows, C_CB:C_CB + GROUP_W] * cy * _silu(p_scr[rows, C_CZ:C_CZ + GROUP_W])
    mix_scr[rows, 0:GROUP_W] = a_out.astype(bf16)
    for g in range(group):
        conv_out[g] = u[g * seq_len + seq_len - (CONV_W - 1):(g + 1) * seq_len, :]

    qs = p_scr[rows, C_GQ:C_GQ + GROUP_W] * QK_SCALE
    k = p_scr[rows, C_GK:C_GK + GROUP_W]
    v = p_scr[rows, C_GV:C_GV + GROUP_W]
    la = _gate_log_decay(p_scr[rows, C_LR:C_LR + GATE_RANK], wg_ref, bg_ref)
    bc = _block_cumsum(la, seq_len)
    o = _dot((qs * k).astype(bf16), j_ref[...]) * v
    for d in range(1, seq_len):
        dec = jnp.exp(jnp.where(row_t >= d, bc - pltpu.roll(bc, d, 0), NEG))
        pw = qs * pltpu.roll(k, d, 0) * dec
        o = o + _dot(pw.astype(bf16), j_ref[...]) * pltpu.roll(v, d, 0)
    qd = qs * jnp.exp(bc)
    tot = [bc[(g + 1) * seq_len - 1:(g + 1) * seq_len, :] for g in range(group)]
    kd = k * jnp.exp(jnp.concatenate(
        [tot[g] - bc[g * seq_len:(g + 1) * seq_len, :] for g in range(group)], axis=0))
    low = lax.broadcasted_iota(jnp.int32, (seq_len, LANES), 1) < HEAD_D
    low_st = lax.broadcasted_iota(jnp.int32, (HEAD_D, LANES), 1) < HEAD_D
    o_inter = []
    for g in range(group):
        rs = slice(g * seq_len, (g + 1) * seq_len)
        alpha = jnp.exp(tot[g])
        per_pair = []
        for pair in range(N_HEADS // 2):
            ls = slice(pair * LANES, (pair + 1) * LANES)
            st0 = sgla_ref[g, pair]
            q_pair = qd[rs, ls]
            q2 = jnp.concatenate([jnp.where(low, q_pair, 0.0), jnp.where(low, 0.0, q_pair)], axis=0)
            o2 = _dot_nt(q2.astype(bf16), st0.astype(bf16))
            per_pair.append(jnp.concatenate([o2[0:seq_len], o2[seq_len:2 * seq_len]], axis=1))
            kd_pair = kd[rs, ls].astype(bf16)
            upd = [_dot_tn(v[rs, (2 * pair + hh) * HEAD_D:(2 * pair + hh + 1) * HEAD_D].astype(bf16), kd_pair)
                   for hh in range(2)]
            new_pair = st0 * alpha[:, ls] + jnp.where(low_st, upd[0], upd[1])
            gla_out[g, 2 * pair] = new_pair[:, 0:HEAD_D]
            gla_out[g, 2 * pair + 1] = new_pair[:, HEAD_D:LANES]
        o_inter.append(jnp.concatenate(per_pair, axis=1))
    o = o + jnp.concatenate(o_inter, axis=0)
    b_out = _gla_norm_gate(o, p_scr[rows, C_GZ:C_GZ + GROUP_W], gnorm_ref, j_ref)
    mix_scr[rows, GROUP_W:2 * GROUP_W] = b_out.astype(bf16)

    cos = cos_ref[...]
    sin = sin_ref[...]
    q01 = _rope(p_scr[rows, C_SQ:C_SQ + LANES], cos, sin) * QK_SCALE
    q23 = _rope(p_scr[rows, C_SQ + LANES:C_SQ + 2 * LANES], cos, sin) * QK_SCALE
    kr = _rope(p_scr[rows, C_SK:C_SK + SWA_KV_W], cos, sin)
    vx = p_scr[rows, C_SV:C_SV + SWA_KV_W]
    qm = p_scr[rows, C_MQ:C_MQ + GROUP_W] * QK_SCALE
    hq = N_HEADS * seq_len
    sink_seq = _sink_column([sinks_ref[l, hh] for hh in range(N_HEADS)], seq_len)
    sink = jnp.concatenate([sink_seq] * group, axis=0)
    hq_all = group * hq
    qrow = lax.broadcasted_iota(jnp.int32, (hq_all, WINDOW), 0) % seq_len
    ccol = lax.broadcasted_iota(jnp.int32, (hq_all, WINDOW), 1)
    cache_valid = ccol > qrow
    nrow = lax.broadcasted_iota(jnp.int32, (hq_all, seq_len), 0) % seq_len
    ncol = lax.broadcasted_iota(jnp.int32, (hq_all, seq_len), 1)
    new_valid = ncol <= nrow
    keep_old = lax.broadcasted_iota(jnp.int32, (SWA_KV_W, WINDOW), 1) < WINDOW - seq_len

    def exact_split(a):
        hi = a.astype(bf16)
        r1 = a - hi.astype(f32)
        mid = r1.astype(bf16)
        lo = (r1 - mid.astype(f32)).astype(bf16)
        return jnp.concatenate([hi, mid, lo], axis=0)

    def shifted_cache(old_t, new_rows):
        placed = _dot_tn(exact_split(new_rows), place_ref[...])
        return jnp.where(keep_old, pltpu.roll(old_t, WINDOW - seq_len, 1), placed)

    seqs = [slice(g * seq_len, (g + 1) * seq_len) for g in range(group)]
    stk = [slice(g * hq, (g + 1) * hq) for g in range(group)]
    qst = [_stack_swa_q(q01[rs], q23[rs]).astype(bf16) for rs in seqs]
    s_c = jnp.concatenate([_dot(qst[g], kc_ref[g].astype(bf16)) for g in range(group)], axis=0)
    s_n = jnp.concatenate([_dot_nt(qst[g], kr[seqs[g]].astype(bf16)) for g in range(group)], axis=0)
    s_c = jnp.where(cache_valid, s_c, NEG)
    s_n = jnp.where(new_valid, s_n, NEG)
    mx = jnp.maximum(jnp.maximum(jnp.max(s_c, axis=1, keepdims=True), jnp.max(s_n, axis=1, keepdims=True)), sink)
    e_c = jnp.exp(s_c - mx)
    e_n = jnp.exp(s_n - mx)
    den = jnp.sum(e_c, axis=1, keepdims=True) + jnp.sum(e_n, axis=1, keepdims=True) + jnp.exp(sink - mx)
    e_c = e_c.astype(bf16)
    e_n = e_n.astype(bf16)
    ov = jnp.concatenate([_dot_nt(e_c[stk[g]], vc_ref[g].astype(bf16)) + _dot(e_n[stk[g]], vx[seqs[g]].astype(bf16))
                          for g in range(group)], axis=0) * (1.0 / den)
    oc = jnp.concatenate([_unstack_swa(ov[stk[g]], seq_len) for g in range(group)], axis=0)
    for g in range(group):
        swak_out[g] = shifted_cache(kc_ref[g], kr[seqs[g]])
        swav_out[g] = shifted_cache(vc_ref[g], vx[seqs[g]])

    sm = jnp.concatenate([_dot(_stack_heads(qm[seqs[g]]).astype(bf16), mkc_ref[g].astype(bf16))
                          for g in range(group)], axis=0)
    mm = jnp.max(sm, axis=1, keepdims=True)
    em = jnp.exp(sm - mm)
    dm = jnp.sum(em, axis=1, keepdims=True)
    em = em.astype(bf16)
    odv = jnp.concatenate([_dot_nt(em[stk[g]], mvc_ref[g].astype(bf16)) for g in range(group)], axis=0) * (1.0 / dm)
    od = jnp.concatenate([_unstack_heads(odv[stk[g]], seq_len) for g in range(group)], axis=0)
    mix_scr[rows, 2 * GROUP_W:3 * GROUP_W] = (oc * _silu(p_scr[rows, C_SZ:C_SZ + GROUP_W])).astype(bf16)
    mix_scr[rows, 3 * GROUP_W:4 * GROUP_W] = (od * _silu(p_scr[rows, C_MZ:C_MZ + GROUP_W])).astype(bf16)

    @pl.when(g_step == n_steps - 1)
    def _residual():
        for rws in chunks:
            mo = _dot(mix_scr[rws, :], woutb[...])
            hs[rws, :] = hs[rws, :] + _rmsnorm(mo, gpost_ref[...])

    @pl.when((l == n_layers - 1) & (g_step == n_steps - 1))
    def _store():
        cp = pltpu.make_async_copy(hs, y_hbm, sem.at[0])
        cp.start()
        cp.wait()


def _sample_layers(x, cos, sin, convb, sgla, kc, vc, mkc, mvc, gpre, gpost, wt, wout, convw, wg, bg, gnorm,
                   sinks, jmat, place, seq_len, group):
    depth, nseq = sgla.shape[0], sgla.shape[1]
    n_tok = nseq * seq_len
    assert nseq % group == 0 and seq_len == SUBLANES and n_tok % TL == 0
    n = group * seq_len

    def per_layer(shape, single_buffer=False):
        nd = len(shape)
        kw = dict(pipeline_mode=pl.Buffered(1)) if single_buffer else {}
        return pl.BlockSpec((None,) + tuple(shape), lambda l, g, _nd=nd: (l,) + (0,) * _nd, **kw)

    def per_group(shape):
        nd = len(shape) - 1
        return pl.BlockSpec((None,) + tuple(shape), lambda l, g, _nd=nd: (l, g) + (0,) * _nd)

    def const(shape):
        nd = len(shape)
        return pl.BlockSpec(shape, lambda l, g, _nd=nd: (0,) * _nd)

    in_specs = [
        pl.BlockSpec(memory_space=pltpu.SMEM),
        pl.BlockSpec(memory_space=pl.ANY),
        const((n, LANES)), const((n, LANES)),
        per_group((n, GROUP_W)),
        per_group((group, N_HEADS // 2, HEAD_D, LANES)),
        per_group((group, SWA_KV_W, WINDOW)), per_group((group, SWA_KV_W, WINDOW)),
        per_group((group, GROUP_W, N_MEM)), per_group((group, GROUP_W, N_MEM)),
        per_layer((1, D_MODEL)), per_layer((1, D_MODEL)),
        per_layer((IN_WIDTH, D_MODEL), True),
        per_layer((D_MODEL, D_MODEL), True),
        per_layer((CONV_W, GROUP_W)), per_layer((GATE_RANK, GROUP_W)), per_layer((1, GROUP_W)), per_layer((1, GROUP_W)),
        const((GROUP_W, GROUP_W)), const((3 * seq_len, WINDOW)),
    ]
    out_shape = (
        jax.ShapeDtypeStruct((n_tok, D_MODEL), f32),
        jax.ShapeDtypeStruct((depth, nseq, CONV_W - 1, GROUP_W), f32),
        jax.ShapeDtypeStruct((depth, nseq, N_HEADS, HEAD_D, HEAD_D), f32),
        jax.ShapeDtypeStruct((depth, nseq, SWA_KV_W, WINDOW), f32),
        jax.ShapeDtypeStruct((depth, nseq, SWA_KV_W, WINDOW), f32),
    )
    out_specs = (
        pl.BlockSpec(memory_space=pl.ANY),
        per_group((group, CONV_W - 1, GROUP_W)),
        per_group((group, N_HEADS, HEAD_D, HEAD_D)),
        per_group((group, SWA_KV_W, WINDOW)), per_group((group, SWA_KV_W, WINDOW)),
    )
    scratch = [
        pltpu.VMEM((n_tok, D_MODEL), f32),
        pltpu.VMEM((n_tok, NP), f32),
        pltpu.VMEM((n_tok, D_MODEL), bf16),
        pltpu.VMEM((D_MODEL, D_MODEL), bf16),
        pltpu.SemaphoreType.DMA((1,)),
    ]
    return pl.pallas_call(
        functools.partial(_sample_kernel, seq_len, group),
        grid=(depth, nseq // group),
        in_specs=in_specs,
        out_specs=out_specs,
        out_shape=out_shape,
        scratch_shapes=scratch,
        compiler_params=pltpu.CompilerParams(dimension_semantics=("arbitrary", "arbitrary"),
                                             vmem_limit_bytes=VMEM_LIMIT_BYTES),
        name="sample_layers",
    )(sinks, x, cos, sin, convb, sgla, kc, vc, mkc, mvc, gpre, gpost, wt, wout, convw, wg, bg, gnorm,
      jmat, place)


def _rope_tables(pos):
    half = HEAD_D // 2
    inv = np.power(ROPE_THETA, -np.arange(half, dtype=np.float64) / half)
    ang = np.asarray(pos, np.float64)[:, None] * inv[None, :]
    cos = np.tile(np.cos(ang), (1, LANES // half))
    sin = np.sin(ang)
    sin_signed = np.tile(np.concatenate([-sin, sin], axis=1), (1, LANES // HEAD_D))
    return cos.astype(np.float32), sin_signed.astype(np.float32)


def _rows_to_kernel_order(a):
    n, w = a.shape
    return a.reshape(n // SUB, SUBLANES, GLA_BLOCK, w).swapaxes(1, 2).reshape(n, w)


def _feature_major(cache):
    d, s, p, h, e = cache.shape
    return jnp.transpose(cache, (0, 1, 3, 4, 2)).reshape(d, s, h * e, p)


def _pair_states(state):
    dd, s, h, e, _ = state.shape
    a = jnp.swapaxes(state, -1, -2).reshape(dd, s, h // 2, 2, e, e)
    return jnp.transpose(a, (0, 1, 2, 4, 3, 5)).reshape(dd, s, h // 2, e, 2 * e)


def _position_major(cache_t, heads):
    d, s, he, p = cache_t.shape
    return jnp.transpose(cache_t.reshape(d, s, heads, he // heads, p), (0, 1, 4, 2, 3))


def kernel(x_prompt, x_sample, state_conv, state_gla, cache_swa_k, cache_swa_v, cache_mem_k, cache_mem_v,
           mem_prompt, norm_pre, norm_post, w_in, conv_w, gla_w_gate, gla_b_gate, gla_norm, swa_sinks,
           w_mem_kv, w_out):
    depth = w_in.shape[0]
    B, L, _ = x_prompt.shape
    nseq, seq_len, _ = x_sample.shape
    group = SUBLANES

    wt = jnp.swapaxes(w_in, 1, 2).astype(bf16)
    wout = w_out
    wmem = w_mem_kv
    wg = gla_w_gate.astype(bf16)
    head_id = np.arange(GROUP_W) // HEAD_D
    jmat = jnp.asarray(head_id[:, None] == head_id[None, :], dtype=bf16)
    lvl, swab = _subtile_constants()
    place = np.zeros((3 * seq_len, WINDOW), np.float32)
    for piece in range(3):
        place[piece * seq_len + np.arange(seq_len), WINDOW - seq_len + np.arange(seq_len)] = 1.0
    place = jnp.asarray(place, dtype=bf16)
    sinks = swa_sinks.astype(f32)
    gpre = norm_pre[:, None, :]
    gpost = norm_post[:, None, :]
    bg = gla_b_gate[:, None, :]
    gn = gla_norm[:, None, :]

    cos_p, sin_p = (jnp.asarray(_rows_to_kernel_order(a)) for a in _rope_tables(np.arange(L)))
    cos_s, sin_s = (jnp.asarray(np.tile(a, (group, 1))) for a in _rope_tables(PAST_LEN + np.arange(seq_len)))

    convb = jnp.pad(state_conv, ((0, 0), (0, 0), (0, seq_len - (CONV_W - 1)), (0, 0)))
    convb = convb.reshape(depth, nseq * seq_len, GROUP_W)
    ys, conv_s, gla_s, swak_s, swav_s = _sample_layers(
        x_sample.reshape(nseq * seq_len, D_MODEL), cos_s, sin_s, convb, _pair_states(state_gla),
        _feature_major(cache_swa_k), _feature_major(cache_swa_v),
        _feature_major(cache_mem_k), _feature_major(cache_mem_v),
        gpre, gpost, wt, wout, conv_w, wg, bg, gn, sinks, jmat, place, seq_len, group)

    hp = _to_kernel_order(x_prompt, 1)
    stacked = [jnp.zeros((depth, B) + shp, f32) for shp in (
        (CONV_W - 1, GROUP_W), (N_HEADS, HEAD_D, HEAD_D), (SWA_KV_W, WINDOW), (SWA_KV_W, WINDOW),
        (GROUP_W, N_MEM), (GROUP_W, N_MEM))]
    for l in range(depth):
        res = _prompt_layer(l, hp, mem_prompt, cos_p, sin_p, gpre, gpost, wt, wout, wmem, conv_w, wg, bg, gn,
                            sinks, jmat, lvl, swab, stacked)
        hp = res[0]
        stacked = list(res[1:])

    return (_from_kernel_order(hp, 1), ys.reshape(nseq, seq_len, D_MODEL),
            stacked[0], stacked[1],
            _position_major(stacked[2], 2), _position_major(stacked[3], 2),
            _position_major(stacked[4], N_HEADS), _position_major(stacked[5], N_HEADS),
            conv_s, jnp.swapaxes(gla_s, -1, -2), _position_major(swak_s, 2), _position_major(swav_s, 2))
```

```python
import functools

import jax
import jax.numpy as jnp
import numpy as np
from jax import lax
from jax.experimental import pallas as pl
from jax.experimental.pallas import tpu as pltpu

f32 = jnp.float32
bf16 = jnp.bfloat16

D_MODEL = 1024
GROUP_W = 256
HEAD_D = 64
N_HEADS = 4
SWA_KV_W = 128
N_MEM = 256
WINDOW = 128
CONV_W = 3
GATE_RANK = 16
GATE_NORM = 16.0
GLA_BLOCK = 16
ROPE_THETA = 10000.0
PAST_LEN = 8192
EPS = 1e-6
NEG = -1e30
QK_SCALE = HEAD_D ** -0.5

LANES = 128
SUBLANES = 8
VMEM_LIMIT_BYTES = 56 * 1024 * 1024

C_CX, C_CB, C_CC, C_CZ = 0, 256, 512, 768
C_GQ, C_GK, C_GV, C_GZ = 1024, 1280, 1536, 1792
C_SQ, C_SK, C_SV, C_SZ = 2048, 2304, 2432, 2560
C_MQ, C_MZ = 2816, 3072
C_LR = 3328
NP = 3456
IN_WIDTH = 3344
_O_GLR, _O_GZ = 1792, 1808

SUB = 128
TL = 512


def _dot(a, b):
    return jnp.dot(a, b, preferred_element_type=f32)


def _dot_nt(a, b):
    return lax.dot_general(a, b, (((1,), (1,)), ((), ())), preferred_element_type=f32)


def _dot_tn(a, b):
    return lax.dot_general(a, b, (((0,), (0,)), ((), ())), preferred_element_type=f32)


def _rmsnorm(x, g):
    return x * lax.rsqrt(jnp.mean(x * x, axis=-1, keepdims=True) + EPS) * g


def _silu(z):
    return z * (0.5 + 0.5 * jnp.tanh(0.5 * z))


def _log_sigmoid(x):
    return jnp.minimum(x, 0.0) - jnp.log1p(jnp.exp(-jnp.abs(x)))


PROJ_PIECES = tuple([(c, c + 256, c) for c in range(0, _O_GLR, 256)]
                    + [(_O_GZ + c, _O_GZ + c + 256, _O_GLR + c) for c in range(0, IN_WIDTH - _O_GZ, 256)]
                    + [(_O_GLR, _O_GZ, C_LR)])


def _project_piece(h, wt_ref, p_ref, rows, piece):
    w0, w1, c0 = PROJ_PIECES[piece]
    p_ref[rows, c0:c0 + (w1 - w0)] = _dot_nt(h, wt_ref[w0:w1, :])


def _project(h, wt_ref, p_ref, rows):
    for piece in range(len(PROJ_PIECES)):
        _project_piece(h, wt_ref, p_ref, rows, piece)


def _head_sum(x, j_ref):
    hi = x.astype(bf16)
    lo = (x - hi.astype(f32)).astype(bf16)
    return _dot(hi, j_ref[...]) + _dot(lo, j_ref[...])


def _rope(x, cos, sin_signed):
    lane = lax.broadcasted_iota(jnp.int32, x.shape, 1)
    swapped = jnp.where((lane % HEAD_D) < HEAD_D // 2,
                        pltpu.roll(x, LANES - HEAD_D // 2, 1), pltpu.roll(x, HEAD_D // 2, 1))
    return x * cos + swapped * sin_signed


def _stack_heads(x):
    lane_head = lax.broadcasted_iota(jnp.int32, x.shape, 1) // HEAD_D
    return jnp.concatenate([jnp.where(lane_head == h, x, 0.0) for h in range(N_HEADS)], axis=0)


def _unstack_heads(o, n):
    lane_head = lax.broadcasted_iota(jnp.int32, (n, GROUP_W), 1) // HEAD_D
    out = o[0:n]
    for h in range(1, N_HEADS):
        out = jnp.where(lane_head == h, o[h * n:(h + 1) * n], out)
    return out


def _stack_swa_q(q01, q23):
    low = lax.broadcasted_iota(jnp.int32, q01.shape, 1) < HEAD_D
    return jnp.concatenate([jnp.where(low, q01, 0.0), jnp.where(low, pltpu.roll(q01, HEAD_D, 1), 0.0),
                            jnp.where(low, 0.0, pltpu.roll(q23, HEAD_D, 1)), jnp.where(low, 0.0, q23)], axis=0)


def _unstack_swa(o, n):
    low = lax.broadcasted_iota(jnp.int32, (n, SWA_KV_W), 1) < HEAD_D
    c01 = jnp.where(low, o[0:n], pltpu.roll(o[n:2 * n], HEAD_D, 1))
    c23 = jnp.where(low, pltpu.roll(o[2 * n:3 * n], HEAD_D, 1), o[3 * n:4 * n])
    return jnp.concatenate([c01, c23], axis=1)


def _sink_column(sinks, n):
    return jnp.concatenate([jnp.full((n, 1), s, f32) for s in sinks], axis=0)


def _block_cumsum(la, block):
    row = lax.broadcasted_iota(jnp.int32, la.shape, 0) % block
    b = la
    s = 1
    while s < block:
        b = b + jnp.where(row >= s, pltpu.roll(b, s, 0), 0.0)
        s *= 2
    return b


def _gate_log_decay(p_lr, wg_ref, bg_ref):
    pre = _dot(p_lr.astype(bf16), wg_ref[...]) + bg_ref[...]
    return _log_sigmoid(pre) * (1.0 / GATE_NORM)


def _gla_norm_gate(o, gz, gnorm_ref, j_ref):
    ms = _head_sum(o * o, j_ref) * (1.0 / HEAD_D)
    return o * lax.rsqrt(ms + EPS) * gnorm_ref[...] * _silu(gz)


GROUPS = SUB // SUBLANES
LEVEL_GROUPS = (1, 2, 4)


def _row_groups(x):
    return [x[SUBLANES * r:SUBLANES * (r + 1), :] for r in range(x.shape[0] // SUBLANES)]


def _prompt_layer_kernel(nt, *refs):
    s = pl.program_id(0)
    pl.when(s == 0)(functools.partial(_prompt_first_step, *refs))
    for parity in range(2):
        pl.when((s > 0) & (s % 2 == parity))(functools.partial(_prompt_layer_step, nt, parity, *refs))


def _prompt_first_step(sinks_ref, x_ref, memp_ref, cos_ref, sin_ref, gpre_ref, gpost_ref, wt_ref,
                       wout_ref, wmem_ref, convw_ref, wg_ref, bg_ref, gnorm_ref, j_ref, lvl_ref, swab_ref,
                       y_ref, conv_out, gla_out, swak_out, swav_out, mk_out, mv_out,
                       p_buf, x_prev, woutb, *unused):
    x = x_ref[0]
    _project(_rmsnorm(x, gpre_ref[...]).astype(bf16), wt_ref, p_buf.at[0], slice(None))
    x_prev[...] = x
    woutb[...] = wout_ref[...].astype(bf16)


def _prompt_layer_step(nt, parity, sinks_ref, x_ref, memp_ref, cos_ref, sin_ref, gpre_ref, gpost_ref, wt_ref,
                       wout_ref, wmem_ref, convw_ref, wg_ref, bg_ref, gnorm_ref, j_ref, lvl_ref, swab_ref,
                       y_ref, conv_out, gla_out, swak_out, swav_out, mk_out, mv_out,
                       p_buf, x_prev, woutb, mix_scr, cbuf, st_scr, kprev, vprev, vtprev, mkb, mvtb, kv_nat):
    s = pl.program_id(0)
    t = jnp.maximum(s - 1, 0) % nt
    nsub = TL // SUB
    p_wr = p_buf.at[parity]
    p_scr = p_buf.at[1 - parity]

    @pl.when(t == 0)
    def _new_sequence():
        cbuf[...] = jnp.zeros((SUBLANES, GROUP_W), f32)
        st_scr[...] = jnp.zeros((GROUP_W, GROUP_W), f32)
        kprev[...] = jnp.zeros((SUB, SWA_KV_W), f32)
        vprev[...] = jnp.zeros((SUB, SWA_KV_W), f32)
        vtprev[...] = jnp.zeros((SWA_KV_W, SUB), f32)
        mkv = _dot(memp_ref[0].astype(bf16), wmem_ref[...].astype(bf16))
        mvt = mkv[:, GROUP_W:2 * GROUP_W].T
        mk_out[0] = mkv[:, 0:GROUP_W].T
        mv_out[0] = mvt
        mkb[...] = mkv[:, 0:GROUP_W].astype(bf16)
        mvtb[...] = mvt.astype(bf16)

    x = x_ref[0]
    h = _rmsnorm(x, gpre_ref[...]).astype(bf16)

    sub8 = lax.broadcasted_iota(jnp.int32, (SUBLANES, GROUP_W), 0)
    halves = (slice(0, TL // 2), slice(TL // 2, TL))
    pending = [(piece, half) for piece in range(len(PROJ_PIECES)) for half in range(2)]

    def project_next(n=2):
        for _ in range(n):
            if pending:
                piece, half = pending.pop(0)
                _project_piece(h[halves[half]], wt_ref, p_wr, halves[half], piece)

    def sub_tile(j):
        rows = slice(j * SUB, (j + 1) * SUB)
        project_next(1)

        u = p_scr[rows, C_CC:C_CC + GROUP_W] * p_scr[rows, C_CX:C_CX + GROUP_W]
        last = SUB - SUBLANES
        prev1 = jnp.where(sub8 == 0, cbuf[0:1, :], pltpu.roll(u[last:SUB, :], 1, 0))
        prev2 = jnp.where(sub8 == 0, cbuf[1:2, :], pltpu.roll(u[last - SUBLANES:last, :], 1, 0))
        um1 = jnp.concatenate([prev1, u[0:last, :]], axis=0)
        um2 = jnp.concatenate([prev2, prev1, u[0:last - SUBLANES, :]], axis=0)
        cy = convw_ref[0:1, :] * um2 + convw_ref[1:2, :] * um1 + convw_ref[2:3, :] * u
        cbuf[0:1, :] = u[SUB - 1:SUB, :]
        cbuf[1:2, :] = u[last - 1:last, :]
        a_out = p_scr[rows, C_CB:C_CB + GROUP_W] * cy * _silu(p_scr[rows, C_CZ:C_CZ + GROUP_W])
        mix_scr[rows, 0:GROUP_W] = a_out.astype(bf16)
        project_next(1)

        qg = _row_groups(p_scr[rows, C_GQ:C_GQ + GROUP_W] * QK_SCALE)
        k = p_scr[rows, C_GK:C_GK + GROUP_W]
        v = p_scr[rows, C_GV:C_GV + GROUP_W]
        kg = _row_groups(k)
        vg = _row_groups(v)
        lag = _row_groups(_gate_log_decay(p_scr[rows, C_LR:C_LR + GATE_RANK], wg_ref, bg_ref))
        bg_ = [lag[0]]
        for r in range(1, GROUPS):
            bg_.append(bg_[-1] + lag[r])
        tot = bg_[GROUPS - 1]
        pw = []
        for r in range(GROUPS):
            for s in range(r):
                pw.append(qg[r] * kg[s] * jnp.exp(bg_[r] - bg_[s]))
            pw.append(qg[r] * kg[r])
        scores = _dot(jnp.concatenate(pw, axis=0).astype(bf16), j_ref[...])
        project_next(1)
        og = []
        idx = 0
        for r in range(GROUPS):
            acc = None
            for s in range(r + 1):
                term = scores[SUBLANES * idx:SUBLANES * (idx + 1), :] * vg[s]
                acc = term if acc is None else acc + term
                idx += 1
            og.append(acc)
        o = jnp.concatenate(og, axis=0)

        def decayed(group):
            before = jnp.zeros((SUBLANES, GROUP_W), f32)
            after = jnp.zeros((SUBLANES, GROUP_W), f32)
            for s in range(1, group):
                before = before + jnp.where(sub8 % group >= s, pltpu.roll(tot, s, 0), 0.0)
                after = after + jnp.where(sub8 % group < group - s, pltpu.roll(tot, SUBLANES - s, 0), 0.0)
            qd = jnp.concatenate([qg[r] * jnp.exp(bg_[r] + before) for r in range(GROUPS)], axis=0)
            kd = jnp.concatenate([kg[r] * jnp.exp((tot - bg_[r]) + after) for r in range(GROUPS)], axis=0)
            return qd, kd

        attn = None
        for li, group in enumerate(LEVEL_GROUPS):
            qd, kd = decayed(group)
            s = _dot_nt(_stack_heads(qd).astype(bf16), kd.astype(bf16)) * lvl_ref[li]
            attn = s if attn is None else attn + s
        o = o + _unstack_heads(_dot(attn.astype(bf16), v.astype(bf16)), SUB)
        project_next(1)
        qd, kd = decayed(SUB // GLA_BLOCK)
        st = st_scr[...]
        o = o + _dot_nt(qd.astype(bf16), st.astype(bf16))
        total = jnp.sum(tot, axis=0, keepdims=True)
        upd = _dot_tn(v.astype(bf16), kd.astype(bf16))
        same_head = (lax.broadcasted_iota(jnp.int32, (GROUP_W, GROUP_W), 0) // HEAD_D
                     == lax.broadcasted_iota(jnp.int32, (GROUP_W, GROUP_W), 1) // HEAD_D)
        st_scr[...] = st * jnp.exp(total) + jnp.where(same_head, upd, 0.0)
        b_out = _gla_norm_gate(o, p_scr[rows, C_GZ:C_GZ + GROUP_W], gnorm_ref, j_ref)
        mix_scr[rows, GROUP_W:2 * GROUP_W] = b_out.astype(bf16)
        project_next(1)

        cos = cos_ref[rows, :]
        sin = sin_ref[rows, :]
        q01 = _rope(p_scr[rows, C_SQ:C_SQ + LANES], cos, sin) * QK_SCALE
        q23 = _rope(p_scr[rows, C_SQ + LANES:C_SQ + 2 * LANES], cos, sin) * QK_SCALE
        kr = _rope(p_scr[rows, C_SK:C_SK + SWA_KV_W], cos, sin)
        vx = p_scr[rows, C_SV:C_SV + SWA_KV_W]
        k_all = jnp.concatenate([kprev[...], kr], axis=0).astype(bf16)
        vt_new = vx.T
        vt_all = jnp.concatenate([vtprev[...], vt_new], axis=1).astype(bf16)
        first = jnp.where((t * nsub + j) == 0, 1, 0)
        sc = _dot_nt(k_all, _stack_swa_q(q01, q23).astype(bf16)) + swab_ref[first]
        lane_head = lax.broadcasted_iota(jnp.int32, (1, N_HEADS * SUB), 1) // SUB
        sink = jnp.full((1, N_HEADS * SUB), sinks_ref[0], f32)
        for hh in range(1, N_HEADS):
            sink = jnp.where(lane_head == hh, sinks_ref[hh], sink)
        project_next(1)
        mx = jnp.maximum(jnp.max(sc, axis=0, keepdims=True), sink)
        e = jnp.exp(sc - mx)
        den = jnp.sum(e, axis=0, keepdims=True) + jnp.exp(sink - mx)
        ot = _dot(vt_all, e.astype(bf16)) * (1.0 / den)
        oc = _unstack_swa(jnp.concatenate([ot[:, SUB * hh:SUB * (hh + 1)].T for hh in range(N_HEADS)], axis=0), SUB)
        mix_scr[rows, 2 * GROUP_W:3 * GROUP_W] = (oc * _silu(p_scr[rows, C_SZ:C_SZ + GROUP_W])).astype(bf16)
        kprev[...] = kr
        vprev[...] = vx
        vtprev[...] = vt_new
        project_next(1)

        qm = _stack_heads(p_scr[rows, C_MQ:C_MQ + GROUP_W] * QK_SCALE).astype(bf16)
        sm = _dot_nt(mkb[...], qm)
        mm = jnp.max(sm, axis=0, keepdims=True)
        em = jnp.exp(sm - mm)
        dm = jnp.sum(em, axis=0, keepdims=True)
        odt = _dot(mvtb[...], em.astype(bf16)) * (1.0 / dm)
        low = lax.broadcasted_iota(jnp.int32, (SUB, LANES), 1) < HEAD_D
        halves = []
        for pair in range(N_HEADS // 2):
            blk_rows = odt[LANES * pair:LANES * (pair + 1), :]
            even = blk_rows[:, SUB * (2 * pair):SUB * (2 * pair + 1)].T
            odd = blk_rows[:, SUB * (2 * pair + 1):SUB * (2 * pair + 2)].T
            halves.append(jnp.where(low, even, odd))
        od = jnp.concatenate(halves, axis=1)
        mix_scr[rows, 3 * GROUP_W:4 * GROUP_W] = (od * _silu(p_scr[rows, C_MZ:C_MZ + GROUP_W])).astype(bf16)

    for j in range(nsub):
        sub_tile(j)
    while pending:
        project_next(1)

    mo = _dot(mix_scr[...], woutb[...])
    y_ref[0] = x_prev[...] + _rmsnorm(mo, gpost_ref[...])
    x_prev[...] = x

    @pl.when(t == nt - 1)
    def _sequence_done():
        conv_out[0, 0:1, :] = cbuf[1:2, :]
        conv_out[0, 1:2, :] = cbuf[0:1, :]
        for src, dst in ((kprev, swak_out), (vprev, swav_out)):
            for r in range(GROUPS):
                kv_nat[pl.ds(r, SUBLANES, stride=GLA_BLOCK), :] = src[SUBLANES * r:SUBLANES * (r + 1), :]
            dst[0] = kv_nat[...].T
        st = st_scr[...]
        for hh in range(N_HEADS):
            gla_out[0, hh] = st[HEAD_D * hh:HEAD_D * (hh + 1), HEAD_D * hh:HEAD_D * (hh + 1)].T


def _layer_spec(shape, l):
    nd = len(shape)
    return pl.BlockSpec((None,) + tuple(shape), lambda s, _l=l, _nd=nd: (_l,) + (0,) * _nd)


def _const_spec(shape):
    nd = len(shape)
    return pl.BlockSpec(shape, lambda s, _nd=nd: (0,) * _nd)


def _prompt_layer(l, x, memp, cos, sin, gpre, gpost, wt, wout, wmem, convw, wg, bg, gnorm, sinks, jmat,
                  lvl, swab):
    B, L, _ = x.shape
    assert L % TL == 0
    nt = L // TL
    n_tiles = B * nt

    def projected(s):
        tile = jnp.minimum(s, n_tiles - 1)
        return tile // nt, tile % nt

    def mixed(s):
        tile = jnp.maximum(s - 1, 0)
        return tile // nt, tile % nt

    def per_sequence(shape):
        nd = len(shape) - 1
        return pl.BlockSpec(shape, lambda s, _nd=nd: (mixed(s)[0],) + (0,) * _nd)

    out_shape = (
        jax.ShapeDtypeStruct((B, L, D_MODEL), f32),
        jax.ShapeDtypeStruct((B, CONV_W - 1, GROUP_W), f32),
        jax.ShapeDtypeStruct((B, N_HEADS, HEAD_D, HEAD_D), f32),
        jax.ShapeDtypeStruct((B, SWA_KV_W, WINDOW), f32),
        jax.ShapeDtypeStruct((B, SWA_KV_W, WINDOW), f32),
        jax.ShapeDtypeStruct((B, GROUP_W, N_MEM), f32),
        jax.ShapeDtypeStruct((B, GROUP_W, N_MEM), f32),
    )
    in_specs = [
        pl.BlockSpec(memory_space=pltpu.SMEM),
        pl.BlockSpec((1, TL, D_MODEL), lambda s: projected(s) + (0,)),
        per_sequence((1, N_MEM, D_MODEL)),
        pl.BlockSpec((TL, LANES), lambda s: (mixed(s)[1], 0)),
        pl.BlockSpec((TL, LANES), lambda s: (mixed(s)[1], 0)),
        _layer_spec((1, D_MODEL), l), _layer_spec((1, D_MODEL), l),
        _layer_spec((IN_WIDTH, D_MODEL), l),
        _layer_spec((D_MODEL, D_MODEL), l), _layer_spec((D_MODEL, 2 * GROUP_W), l),
        _layer_spec((CONV_W, GROUP_W), l), _layer_spec((GATE_RANK, GROUP_W), l), _layer_spec((1, GROUP_W), l),
        _layer_spec((1, GROUP_W), l), _const_spec((GROUP_W, GROUP_W)),
        _const_spec((len(LEVEL_GROUPS), N_HEADS * SUB, SUB)), _const_spec((2, 2 * SUB, N_HEADS * SUB)),
    ]
    out_specs = (
        pl.BlockSpec((1, TL, D_MODEL), lambda s: mixed(s) + (0,)),
        per_sequence((1, CONV_W - 1, GROUP_W)),
        per_sequence((1, N_HEADS, HEAD_D, HEAD_D)),
        per_sequence((1, SWA_KV_W, WINDOW)),
        per_sequence((1, SWA_KV_W, WINDOW)),
        per_sequence((1, GROUP_W, N_MEM)),
        per_sequence((1, GROUP_W, N_MEM)),
    )
    scratch = [
        pltpu.VMEM((2, TL, NP), f32),
        pltpu.VMEM((TL, D_MODEL), f32),
        pltpu.VMEM((D_MODEL, D_MODEL), bf16),
        pltpu.VMEM((TL, D_MODEL), bf16),
        pltpu.VMEM((SUBLANES, GROUP_W), f32),
        pltpu.VMEM((GROUP_W, GROUP_W), f32),
        pltpu.VMEM((SUB, SWA_KV_W), f32),
        pltpu.VMEM((SUB, SWA_KV_W), f32),
        pltpu.VMEM((SWA_KV_W, SUB), f32),
        pltpu.VMEM((N_MEM, GROUP_W), bf16),
        pltpu.VMEM((GROUP_W, N_MEM), bf16),
        pltpu.VMEM((WINDOW, SWA_KV_W), f32),
    ]
    return pl.pallas_call(
        functools.partial(_prompt_layer_kernel, nt),
        grid=(n_tiles + 1,),
        in_specs=in_specs,
        out_specs=out_specs,
        out_shape=out_shape,
        scratch_shapes=scratch,
        compiler_params=pltpu.CompilerParams(dimension_semantics=("arbitrary",),
                                             vmem_limit_bytes=VMEM_LIMIT_BYTES),
        name="prompt_layer",
    )(sinks[l], x, memp, cos, sin, gpre, gpost, wt, wout, wmem, convw, wg, bg, gnorm, jmat, lvl, swab)


def _subtile_constants():
    row = np.arange(SUB)
    tok = (row % SUBLANES) * GLA_BLOCK + row // SUBLANES
    blk = row % SUBLANES
    levels = []
    for group in LEVEL_GROUPS:
        g = blk // group
        sel = (g[:, None] == g[None, :] + 1) & (g[:, None] % 2 == 1)
        levels.append(np.tile(sel, (N_HEADS, 1)))
    key_tok = np.concatenate([tok - SUB, tok])
    valid = (key_tok[None, :] <= tok[:, None]) & (key_tok[None, :] > tok[:, None] - WINDOW)
    bias = [np.where(valid, 0.0, NEG), np.where(valid & (key_tok[None, :] >= 0), 0.0, NEG)]
    bias = np.stack([np.tile(b, (N_HEADS, 1)).T for b in bias])
    return jnp.asarray(np.stack(levels), dtype=f32), jnp.asarray(bias, dtype=f32)


def _to_kernel_order(x, axis):
    shp = x.shape
    n = shp[axis]
    x = x.reshape(shp[:axis] + (n // SUB, SUBLANES, GLA_BLOCK) + shp[axis + 1:])
    return jnp.swapaxes(x, axis + 1, axis + 2).reshape(shp)


def _from_kernel_order(x, axis):
    shp = x.shape
    n = shp[axis]
    x = x.reshape(shp[:axis] + (n // SUB, GLA_BLOCK, SUBLANES) + shp[axis + 1:])
    return jnp.swapaxes(x, axis + 1, axis + 2).reshape(shp)


def _sample_kernel(seq_len, group, sinks_ref, x_hbm, cos_ref, sin_ref, convb_ref, sgla_ref, kc_ref, vc_ref,
                   mkc_ref, mvc_ref, gpre_ref, gpost_ref, wt_ref, wout_ref, convw_ref, wg_ref, bg_ref,
                   gnorm_ref, j_ref, place_ref,
                   y_hbm, conv_out, gla_out, swak_out, swav_out,
                   hs, p_scr, mix_scr, woutb, sem):
    l = pl.program_id(0)
    g_step = pl.program_id(1)
    n_layers = pl.num_programs(0)
    n_steps = pl.num_programs(1)
    n_tok = hs.shape[0]
    n = group * seq_len
    chunks = [slice(c * TL, (c + 1) * TL) for c in range(n_tok // TL)]

    @pl.when((l == 0) & (g_step == 0))
    def _load():
        cp = pltpu.make_async_copy(x_hbm, hs, sem.at[0])
        cp.start()
        cp.wait()

    @pl.when(g_step == 0)
    def _project_all():
        woutb[...] = wout_ref[...].astype(bf16)
        for rows in chunks:
            h = _rmsnorm(hs[rows, :], gpre_ref[...]).astype(bf16)
            _project(h, wt_ref, p_scr, rows)

    rows = pl.ds(pl.multiple_of(g_step * n, n), n)
    row_t = lax.broadcasted_iota(jnp.int32, (n, GROUP_W), 0) % seq_len

    u = p_scr[rows, C_CC:C_CC + GROUP_W] * p_scr[rows, C_CX:C_CX + GROUP_W]
    hist = convb_ref[...]
    um1 = jnp.where(row_t >= 1, pltpu.roll(u, 1, 0), pltpu.roll(hist, n - 1, 0))
    um2 = jnp.where(row_t >= 2, pltpu.roll(u, 2, 0), hist)
    cy = convw_ref[0:1, :] * um2 + convw_ref[1:2, :] * um1 + convw_ref[2:3, :] * u
    a_out = p_scr[rows, C_CB:C_CB + GROUP_W] * cy * _silu(p_scr[rows, C_CZ:C_CZ + GROUP_W])
    mix_scr[rows, 0:GROUP_W] = a_out.astype(bf16)
    for g in range(group):
        conv_out[g] = u[g * seq_len + seq_len - (CONV_W - 1):(g + 1) * seq_len, :]

    qs = p_scr[rows, C_GQ:C_GQ + GROUP_W] * QK_SCALE
    k = p_scr[rows, C_GK:C_GK + GROUP_W]
    v = p_scr[rows, C_GV:C_GV + GROUP_W]
    la = _gate_log_decay(p_scr[rows, C_LR:C_LR + GATE_RANK], wg_ref, bg_ref)
    bc = _block_cumsum(la, seq_len)
    o = _dot((qs * k).astype(bf16), j_ref[...]) * v
    for d in range(1, seq_len):
        dec = jnp.exp(jnp.where(row_t >= d, bc - pltpu.roll(bc, d, 0), NEG))
        pw = qs * pltpu.roll(k, d, 0) * dec
        o = o + _dot(pw.astype(bf16), j_ref[...]) * pltpu.roll(v, d, 0)
    qd = qs * jnp.exp(bc)
    tot = [bc[(g + 1) * seq_len - 1:(g + 1) * seq_len, :] for g in range(group)]
    kd = k * jnp.exp(jnp.concatenate(
        [tot[g] - bc[g * seq_len:(g + 1) * seq_len, :] for g in range(group)], axis=0))
    low = lax.broadcasted_iota(jnp.int32, (seq_len, LANES), 1) < HEAD_D
    low_st = lax.broadcasted_iota(jnp.int32, (HEAD_D, LANES), 1) < HEAD_D
    o_inter = []
    for g in range(group):
        rs = slice(g * seq_len, (g + 1) * seq_len)
        alpha = jnp.exp(tot[g])
        per_pair = []
        for pair in range(N_HEADS // 2):
            ls = slice(pair * LANES, (pair + 1) * LANES)
            st0 = sgla_ref[g, pair]
            q_pair = qd[rs, ls]
            q2 = jnp.concatenate([jnp.where(low, q_pair, 0.0), jnp.where(low, 0.0, q_pair)], axis=0)
            o2 = _dot_nt(q2.astype(bf16), st0.astype(bf16))
            per_pair.append(jnp.concatenate([o2[0:seq_len], o2[seq_len:2 * seq_len]], axis=1))
            kd_pair = kd[rs, ls].astype(bf16)
            upd = [_dot_tn(v[rs, (2 * pair + hh) * HEAD_D:(2 * pair + hh + 1) * HEAD_D].astype(bf16), kd_pair)
                   for hh in range(2)]
            new_pair = st0 * alpha[:, ls] + jnp.where(low_st, upd[0], upd[1])
            gla_out[g, 2 * pair] = new_pair[:, 0:HEAD_D]
            gla_out[g, 2 * pair + 1] = new_pair[:, HEAD_D:LANES]
        o_inter.append(jnp.concatenate(per_pair, axis=1))
    o = o + jnp.concatenate(o_inter, axis=0)
    b_out = _gla_norm_gate(o, p_scr[rows, C_GZ:C_GZ + GROUP_W], gnorm_ref, j_ref)
    mix_scr[rows, GROUP_W:2 * GROUP_W] = b_out.astype(bf16)

    cos = cos_ref[...]
    sin = sin_ref[...]
    q01 = _rope(p_scr[rows, C_SQ:C_SQ + LANES], cos, sin) * QK_SCALE
    q23 = _rope(p_scr[rows, C_SQ + LANES:C_SQ + 2 * LANES], cos, sin) * QK_SCALE
    kr = _rope(p_scr[rows, C_SK:C_SK + SWA_KV_W], cos, sin)
    vx = p_scr[rows, C_SV:C_SV + SWA_KV_W]
    qm = p_scr[rows, C_MQ:C_MQ + GROUP_W] * QK_SCALE
    hq = N_HEADS * seq_len
    sink_seq = _sink_column([sinks_ref[l, hh] for hh in range(N_HEADS)], seq_len)
    sink = jnp.concatenate([sink_seq] * group, axis=0)
    hq_all = group * hq
    qrow = lax.broadcasted_iota(jnp.int32, (hq_all, WINDOW), 0) % seq_len
    ccol = lax.broadcasted_iota(jnp.int32, (hq_all, WINDOW), 1)
    cache_valid = ccol > qrow
    nrow = lax.broadcasted_iota(jnp.int32, (hq_all, seq_len), 0) % seq_len
    ncol = lax.broadcasted_iota(jnp.int32, (hq_all, seq_len), 1)
    new_valid = ncol <= nrow
    keep_old = lax.broadcasted_iota(jnp.int32, (SWA_KV_W, WINDOW), 1) < WINDOW - seq_len

    def exact_split(a):
        hi = a.astype(bf16)
        r1 = a - hi.astype(f32)
        mid = r1.astype(bf16)
        lo = (r1 - mid.astype(f32)).astype(bf16)
        return jnp.concatenate([hi, mid, lo], axis=0)

    def shifted_cache(old_t, new_rows):
        placed = _dot_tn(exact_split(new_rows), place_ref[...])
        return jnp.where(keep_old, pltpu.roll(old_t, WINDOW - seq_len, 1), placed)

    seqs = [slice(g * seq_len, (g + 1) * seq_len) for g in range(group)]
    stk = [slice(g * hq, (g + 1) * hq) for g in range(group)]
    qst = [_stack_swa_q(q01[rs], q23[rs]).astype(bf16) for rs in seqs]
    s_c = jnp.concatenate([_dot(qst[g], kc_ref[g].astype(bf16)) for g in range(group)], axis=0)
    s_n = jnp.concatenate([_dot_nt(qst[g], kr[seqs[g]].astype(bf16)) for g in range(group)], axis=0)
    s_c = jnp.where(cache_valid, s_c, NEG)
    s_n = jnp.where(new_valid, s_n, NEG)
    mx = jnp.maximum(jnp.maximum(jnp.max(s_c, axis=1, keepdims=True), jnp.max(s_n, axis=1, keepdims=True)), sink)
    e_c = jnp.exp(s_c - mx)
    e_n = jnp.exp(s_n - mx)
    den = jnp.sum(e_c, axis=1, keepdims=True) + jnp.sum(e_n, axis=1, keepdims=True) + jnp.exp(sink - mx)
    e_c = e_c.astype(bf16)
    e_n = e_n.astype(bf16)
    ov = jnp.concatenate([_dot_nt(e_c[stk[g]], vc_ref[g].astype(bf16)) + _dot(e_n[stk[g]], vx[seqs[g]].astype(bf16))
                          for g in range(group)], axis=0) * (1.0 / den)
    oc = jnp.concatenate([_unstack_swa(ov[stk[g]], seq_len) for g in range(group)], axis=0)
    for g in range(group):
        swak_out[g] = shifted_cache(kc_ref[g], kr[seqs[g]])
        swav_out[g] = shifted_cache(vc_ref[g], vx[seqs[g]])

    sm = jnp.concatenate([_dot(_stack_heads(qm[seqs[g]]).astype(bf16), mkc_ref[g].astype(bf16))
                          for g in range(group)], axis=0)
    mm = jnp.max(sm, axis=1, keepdims=True)
    em = jnp.exp(sm - mm)
    dm = jnp.sum(em, axis=1, keepdims=True)
    em = em.astype(bf16)
    odv = jnp.concatenate([_dot_nt(em[stk[g]], mvc_ref[g].astype(bf16)) for g in range(group)], axis=0) * (1.0 / dm)
    od = jnp.concatenate([_unstack_heads(odv[stk[g]], seq_len) for g in range(group)], axis=0)
    mix_scr[rows, 2 * GROUP_W:3 * GROUP_W] = (oc * _silu(p_scr[rows, C_SZ:C_SZ + GROUP_W])).astype(bf16)
    mix_scr[rows, 3 * GROUP_W:4 * GROUP_W] = (od * _silu(p_scr[rows, C_MZ:C_MZ + GROUP_W])).astype(bf16)

    @pl.when(g_step == n_steps - 1)
    def _residual():
        for rws in chunks:
            mo = _dot(mix_scr[rws, :], woutb[...])
            hs[rws, :] = hs[rws, :] + _rmsnorm(mo, gpost_ref[...])

    @pl.when((l == n_layers - 1) & (g_step == n_steps - 1))
    def _store():
        cp = pltpu.make_async_copy(hs, y_hbm, sem.at[0])
        cp.start()
        cp.wait()


def _sample_layers(x, cos, sin, convb, sgla, kc, vc, mkc, mvc, gpre, gpost, wt, wout, convw, wg, bg, gnorm,
                   sinks, jmat, place, seq_len, group):
    depth, nseq = sgla.shape[0], sgla.shape[1]
    n_tok = nseq * seq_len
    assert nseq % group == 0 and seq_len == SUBLANES and n_tok % TL == 0
    n = group * seq_len

    def per_layer(shape, single_buffer=False):
        nd = len(shape)
        kw = dict(pipeline_mode=pl.Buffered(1)) if single_buffer else {}
        return pl.BlockSpec((None,) + tuple(shape), lambda l, g, _nd=nd: (l,) + (0,) * _nd, **kw)

    def per_group(shape):
        nd = len(shape) - 1
        return pl.BlockSpec((None,) + tuple(shape), lambda l, g, _nd=nd: (l, g) + (0,) * _nd)

    def const(shape):
        nd = len(shape)
        return pl.BlockSpec(shape, lambda l, g, _nd=nd: (0,) * _nd)

    in_specs = [
        pl.BlockSpec(memory_space=pltpu.SMEM),
        pl.BlockSpec(memory_space=pl.ANY),
        const((n, LANES)), const((n, LANES)),
        per_group((n, GROUP_W)),
        per_group((group, N_HEADS // 2, HEAD_D, LANES)),
        per_group((group, SWA_KV_W, WINDOW)), per_group((group, SWA_KV_W, WINDOW)),
        per_group((group, GROUP_W, N_MEM)), per_group((group, GROUP_W, N_MEM)),
        per_layer((1, D_MODEL)), per_layer((1, D_MODEL)),
        per_layer((IN_WIDTH, D_MODEL), True),
        per_layer((D_MODEL, D_MODEL), True),
        per_layer((CONV_W, GROUP_W)), per_layer((GATE_RANK, GROUP_W)), per_layer((1, GROUP_W)), per_layer((1, GROUP_W)),
        const((GROUP_W, GROUP_W)), const((3 * seq_len, WINDOW)),
    ]
    out_shape = (
        jax.ShapeDtypeStruct((n_tok, D_MODEL), f32),
        jax.ShapeDtypeStruct((depth, nseq, CONV_W - 1, GROUP_W), f32),
        jax.ShapeDtypeStruct((depth, nseq, N_HEADS, HEAD_D, HEAD_D), f32),
        jax.ShapeDtypeStruct((depth, nseq, SWA_KV_W, WINDOW), f32),
        jax.ShapeDtypeStruct((depth, nseq, SWA_KV_W, WINDOW), f32),
    )
    out_specs = (
        pl.BlockSpec(memory_space=pl.ANY),
        per_group((group, CONV_W - 1, GROUP_W)),
        per_group((group, N_HEADS, HEAD_D, HEAD_D)),
        per_group((group, SWA_KV_W, WINDOW)), per_group((group, SWA_KV_W, WINDOW)),
    )
    scratch = [
        pltpu.VMEM((n_tok, D_MODEL), f32),
        pltpu.VMEM((n_tok, NP), f32),
        pltpu.VMEM((n_tok, D_MODEL), bf16),
        pltpu.VMEM((D_MODEL, D_MODEL), bf16),
        pltpu.SemaphoreType.DMA((1,)),
    ]
    return pl.pallas_call(
        functools.partial(_sample_kernel, seq_len, group),
        grid=(depth, nseq // group),
        in_specs=in_specs,
        out_specs=out_specs,
        out_shape=out_shape,
        scratch_shapes=scratch,
        compiler_params=pltpu.CompilerParams(dimension_semantics=("arbitrary", "arbitrary"),
                                             vmem_limit_bytes=VMEM_LIMIT_BYTES),
        name="sample_layers",
    )(sinks, x, cos, sin, convb, sgla, kc, vc, mkc, mvc, gpre, gpost, wt, wout, convw, wg, bg, gnorm,
      jmat, place)


def _rope_tables(pos):
    half = HEAD_D // 2
    inv = np.power(ROPE_THETA, -np.arange(half, dtype=np.float64) / half)
    ang = np.asarray(pos, np.float64)[:, None] * inv[None, :]
    cos = np.tile(np.cos(ang), (1, LANES // half))
    sin = np.sin(ang)
    sin_signed = np.tile(np.concatenate([-sin, sin], axis=1), (1, LANES // HEAD_D))
    return cos.astype(np.float32), sin_signed.astype(np.float32)


def _rows_to_kernel_order(a):
    n, w = a.shape
    return a.reshape(n // SUB, SUBLANES, GLA_BLOCK, w).swapaxes(1, 2).reshape(n, w)


def _feature_major(cache):
    d, s, p, h, e = cache.shape
    return jnp.transpose(cache, (0, 1, 3, 4, 2)).reshape(d, s, h * e, p)


def _pair_states(state):
    dd, s, h, e, _ = state.shape
    a = jnp.swapaxes(state, -1, -2).reshape(dd, s, h // 2, 2, e, e)
    return jnp.transpose(a, (0, 1, 2, 4, 3, 5)).reshape(dd, s, h // 2, e, 2 * e)


def _position_major(cache_t, heads):
    d, s, he, p = cache_t.shape
    return jnp.transpose(cache_t.reshape(d, s, heads, he // heads, p), (0, 1, 4, 2, 3))


def kernel(x_prompt, x_sample, state_conv, state_gla, cache_swa_k, cache_swa_v, cache_mem_k, cache_mem_v,
           mem_prompt, norm_pre, norm_post, w_in, conv_w, gla_w_gate, gla_b_gate, gla_norm, swa_sinks,
           w_mem_kv, w_out):
    depth = w_in.shape[0]
    B, L, _ = x_prompt.shape
    nseq, seq_len, _ = x_sample.shape
    group = 8

    wt = jnp.swapaxes(w_in, 1, 2).astype(bf16)
    wout = w_out
    wmem = w_mem_kv
    wg = gla_w_gate.astype(bf16)
    head_id = np.arange(GROUP_W) // HEAD_D
    jmat = jnp.asarray(head_id[:, None] == head_id[None, :], dtype=bf16)
    lvl, swab = _subtile_constants()
    place = np.zeros((3 * seq_len, WINDOW), np.float32)
    for piece in range(3):
        place[piece * seq_len + np.arange(seq_len), WINDOW - seq_len + np.arange(seq_len)] = 1.0
    place = jnp.asarray(place, dtype=bf16)
    sinks = swa_sinks.astype(f32)
    gpre = norm_pre[:, None, :]
    gpost = norm_post[:, None, :]
    bg = gla_b_gate[:, None, :]
    gn = gla_norm[:, None, :]

    cos_p, sin_p = (jnp.asarray(_rows_to_kernel_order(a)) for a in _rope_tables(np.arange(L)))
    cos_s, sin_s = (jnp.asarray(np.tile(a, (group, 1))) for a in _rope_tables(PAST_LEN + np.arange(seq_len)))

    convb = jnp.pad(state_conv, ((0, 0), (0, 0), (0, seq_len - (CONV_W - 1)), (0, 0)))
    convb = convb.reshape(depth, nseq * seq_len, GROUP_W)
    ys, conv_s, gla_s, swak_s, swav_s = _sample_layers(
        x_sample.reshape(nseq * seq_len, D_MODEL), cos_s, sin_s, convb, _pair_states(state_gla),
        _feature_major(cache_swa_k), _feature_major(cache_swa_v),
        _feature_major(cache_mem_k), _feature_major(cache_mem_v),
        gpre, gpost, wt, wout, conv_w, wg, bg, gn, sinks, jmat, place, seq_len, group)

    hp = _to_kernel_order(x_prompt, 1)
    outs_p = [[] for _ in range(6)]
    for l in range(depth):
        res = _prompt_layer(l, hp, mem_prompt, cos_p, sin_p, gpre, gpost, wt, wout, wmem, conv_w, wg, bg, gn,
                            sinks, jmat, lvl, swab)
        hp = res[0]
        for i in range(6):
            outs_p[i].append(res[i + 1])

    return (_from_kernel_order(hp, 1), ys.reshape(nseq, seq_len, D_MODEL),
            jnp.stack(outs_p[0]), jnp.stack(outs_p[1]),
            _position_major(jnp.stack(outs_p[2]), 2), _position_major(jnp.stack(outs_p[3]), 2),
            _position_major(jnp.stack(outs_p[4]), N_HEADS), _position_major(jnp.stack(outs_p[5]), N_HEADS),
            conv_s, jnp.swapaxes(gla_s, -1, -2), _position_major(swak_s, 2), _position_major(swav_s, 2))
```

```python
import functools

import jax
import jax.numpy as jnp
import numpy as np
from jax import lax
from jax.experimental import pallas as pl
from jax.experimental.pallas import tpu as pltpu

f32 = jnp.float32
bf16 = jnp.bfloat16

D_MODEL = 1024
GROUP_W = 256
HEAD_D = 64
N_HEADS = 4
SWA_KV_W = 128
N_MEM = 256
WINDOW = 128
CONV_W = 3
GATE_RANK = 16
GATE_NORM = 16.0
GLA_BLOCK = 16
ROPE_THETA = 10000.0
PAST_LEN = 8192
EPS = 1e-6
NEG = -1e30
QK_SCALE = HEAD_D ** -0.5

LANES = 128
SUBLANES = 8
VMEM_LIMIT_BYTES = 56 * 1024 * 1024

C_CX, C_CB, C_CC, C_CZ = 0, 256, 512, 768
C_GQ, C_GK, C_GV, C_GZ = 1024, 1280, 1536, 1792
C_SQ, C_SK, C_SV, C_SZ = 2048, 2304, 2432, 2560
C_MQ, C_MZ = 2816, 3072
C_LR = 3328
NP = 3456
IN_WIDTH = 3344
_O_GLR, _O_GZ = 1792, 1808

SUB = 128
TL = 512
CACHE_RING = 3


def _dot(a, b):
    return jnp.dot(a, b, preferred_element_type=f32)


def _dot_nt(a, b):
    return lax.dot_general(a, b, (((1,), (1,)), ((), ())), preferred_element_type=f32)


def _dot_tn(a, b):
    return lax.dot_general(a, b, (((0,), (0,)), ((), ())), preferred_element_type=f32)


def _rmsnorm(x, g):
    return x * lax.rsqrt(jnp.mean(x * x, axis=-1, keepdims=True) + EPS) * g


def _silu(z):
    return z * (0.5 + 0.5 * jnp.tanh(0.5 * z))


def _log_sigmoid(x):
    return jnp.minimum(x, 0.0) - jnp.log1p(jnp.exp(-jnp.abs(x)))


PROJ_PIECES = tuple([(c, c + 256, c) for c in range(0, _O_GLR, 256)]
                    + [(_O_GZ + c, _O_GZ + c + 256, _O_GLR + c) for c in range(0, IN_WIDTH - _O_GZ, 256)]
                    + [(_O_GLR, _O_GZ, C_LR)])


def _project_piece(h, wt_ref, p_ref, rows, piece):
    w0, w1, c0 = PROJ_PIECES[piece]
    p_ref[rows, c0:c0 + (w1 - w0)] = _dot_nt(h, wt_ref[w0:w1, :])


def _project(h, wt_ref, p_ref, rows):
    for piece in range(len(PROJ_PIECES)):
        _project_piece(h, wt_ref, p_ref, rows, piece)


def _head_sum(x, j_ref):
    hi = x.astype(bf16)
    lo = (x - hi.astype(f32)).astype(bf16)
    return _dot(hi, j_ref[...]) + _dot(lo, j_ref[...])


def _rope(x, cos, sin_signed):
    lane = lax.broadcasted_iota(jnp.int32, x.shape, 1)
    swapped = jnp.where((lane % HEAD_D) < HEAD_D // 2,
                        pltpu.roll(x, LANES - HEAD_D // 2, 1), pltpu.roll(x, HEAD_D // 2, 1))
    return x * cos + swapped * sin_signed


def _stack_heads(x):
    lane_head = lax.broadcasted_iota(jnp.int32, x.shape, 1) // HEAD_D
    return jnp.concatenate([jnp.where(lane_head == h, x, 0.0) for h in range(N_HEADS)], axis=0)


def _unstack_heads(o, n):
    lane_head = lax.broadcasted_iota(jnp.int32, (n, GROUP_W), 1) // HEAD_D
    out = o[0:n]
    for h in range(1, N_HEADS):
        out = jnp.where(lane_head == h, o[h * n:(h + 1) * n], out)
    return out


def _stack_swa_q(q01, q23):
    low = lax.broadcasted_iota(jnp.int32, q01.shape, 1) < HEAD_D
    return jnp.concatenate([jnp.where(low, q01, 0.0), jnp.where(low, pltpu.roll(q01, HEAD_D, 1), 0.0),
                            jnp.where(low, 0.0, pltpu.roll(q23, HEAD_D, 1)), jnp.where(low, 0.0, q23)], axis=0)


def _unstack_swa(o, n):
    low = lax.broadcasted_iota(jnp.int32, (n, SWA_KV_W), 1) < HEAD_D
    c01 = jnp.where(low, o[0:n], pltpu.roll(o[n:2 * n], HEAD_D, 1))
    c23 = jnp.where(low, pltpu.roll(o[2 * n:3 * n], HEAD_D, 1), o[3 * n:4 * n])
    return jnp.concatenate([c01, c23], axis=1)


def _sink_column(sinks, n):
    return jnp.concatenate([jnp.full((n, 1), s, f32) for s in sinks], axis=0)


def _block_cumsum(la, block):
    row = lax.broadcasted_iota(jnp.int32, la.shape, 0) % block
    b = la
    s = 1
    while s < block:
        b = b + jnp.where(row >= s, pltpu.roll(b, s, 0), 0.0)
        s *= 2
    return b


def _gate_log_decay(p_lr, wg_ref, bg_ref):
    pre = _dot(p_lr.astype(bf16), wg_ref[...]) + bg_ref[...]
    return _log_sigmoid(pre) * (1.0 / GATE_NORM)


def _gla_norm_gate(o, gz, gnorm_ref, j_ref):
    ms = _head_sum(o * o, j_ref) * (1.0 / HEAD_D)
    return o * lax.rsqrt(ms + EPS) * gnorm_ref[...] * _silu(gz)


GROUPS = SUB // SUBLANES
LEVEL_GROUPS = (1, 2, 4)


def _row_groups(x):
    return [x[SUBLANES * r:SUBLANES * (r + 1), :] for r in range(x.shape[0] // SUBLANES)]


def _prompt_layer_kernel(nt, *refs):
    s = pl.program_id(0)
    pl.when(s == 0)(functools.partial(_prompt_first_step, *refs))
    for parity in range(2):
        pl.when((s > 0) & (s % 2 == parity))(functools.partial(_prompt_layer_step, nt, parity, *refs))


def _prompt_first_step(sinks_ref, x_ref, memp_ref, cos_ref, sin_ref, gpre_ref, gpost_ref, wt_ref,
                       wout_ref, wmem_ref, convw_ref, wg_ref, bg_ref, gnorm_ref, j_ref, lvl_ref, swab_ref,
                       y_ref, conv_out, gla_out, swak_out, swav_out, mk_out, mv_out,
                       p_buf, x_prev, woutb, *unused):
    x = x_ref[0]
    _project(_rmsnorm(x, gpre_ref[...]).astype(bf16), wt_ref, p_buf.at[0], slice(None))
    x_prev[...] = x
    woutb[...] = wout_ref[...].astype(bf16)


def _prompt_layer_step(nt, parity, sinks_ref, x_ref, memp_ref, cos_ref, sin_ref, gpre_ref, gpost_ref, wt_ref,
                       wout_ref, wmem_ref, convw_ref, wg_ref, bg_ref, gnorm_ref, j_ref, lvl_ref, swab_ref,
                       y_ref, conv_out, gla_out, swak_out, swav_out, mk_out, mv_out,
                       p_buf, x_prev, woutb, mix_scr, cbuf, st_scr, kprev, vprev, vtprev, mkb, mvtb, kv_nat):
    s = pl.program_id(0)
    t = jnp.maximum(s - 1, 0) % nt
    nsub = TL // SUB
    p_wr = p_buf.at[parity]
    p_scr = p_buf.at[1 - parity]

    @pl.when(t == 0)
    def _new_sequence():
        cbuf[...] = jnp.zeros((SUBLANES, GROUP_W), f32)
        st_scr[...] = jnp.zeros((GROUP_W, GROUP_W), f32)
        kprev[...] = jnp.zeros((SUB, SWA_KV_W), f32)
        vprev[...] = jnp.zeros((SUB, SWA_KV_W), f32)
        vtprev[...] = jnp.zeros((SWA_KV_W, SUB), f32)
        mkv = _dot(memp_ref[0].astype(bf16), wmem_ref[...].astype(bf16))
        mvt = mkv[:, GROUP_W:2 * GROUP_W].T
        mk_out[0] = mkv[:, 0:GROUP_W].T
        mv_out[0] = mvt
        mkb[...] = mkv[:, 0:GROUP_W].astype(bf16)
        mvtb[...] = mvt.astype(bf16)

    x = x_ref[0]
    h = _rmsnorm(x, gpre_ref[...]).astype(bf16)

    sub8 = lax.broadcasted_iota(jnp.int32, (SUBLANES, GROUP_W), 0)
    halves = (slice(0, TL // 2), slice(TL // 2, TL))
    pending = [(piece, half) for piece in range(len(PROJ_PIECES)) for half in range(2)]

    def project_next(n=2):
        for _ in range(n):
            if pending:
                piece, half = pending.pop(0)
                _project_piece(h[halves[half]], wt_ref, p_wr, halves[half], piece)

    def sub_tile(j):
        rows = slice(j * SUB, (j + 1) * SUB)
        project_next(1)

        u = p_scr[rows, C_CC:C_CC + GROUP_W] * p_scr[rows, C_CX:C_CX + GROUP_W]
        last = SUB - SUBLANES
        prev1 = jnp.where(sub8 == 0, cbuf[0:1, :], pltpu.roll(u[last:SUB, :], 1, 0))
        prev2 = jnp.where(sub8 == 0, cbuf[1:2, :], pltpu.roll(u[last - SUBLANES:last, :], 1, 0))
        um1 = jnp.concatenate([prev1, u[0:last, :]], axis=0)
        um2 = jnp.concatenate([prev2, prev1, u[0:last - SUBLANES, :]], axis=0)
        cy = convw_ref[0:1, :] * um2 + convw_ref[1:2, :] * um1 + convw_ref[2:3, :] * u
        cbuf[0:1, :] = u[SUB - 1:SUB, :]
        cbuf[1:2, :] = u[last - 1:last, :]
        a_out = p_scr[rows, C_CB:C_CB + GROUP_W] * cy * _silu(p_scr[rows, C_CZ:C_CZ + GROUP_W])
        mix_scr[rows, 0:GROUP_W] = a_out.astype(bf16)
        project_next(1)

        qg = _row_groups(p_scr[rows, C_GQ:C_GQ + GROUP_W] * QK_SCALE)
        k = p_scr[rows, C_GK:C_GK + GROUP_W]
        v = p_scr[rows, C_GV:C_GV + GROUP_W]
        kg = _row_groups(k)
        vg = _row_groups(v)
        lag = _row_groups(_gate_log_decay(p_scr[rows, C_LR:C_LR + GATE_RANK], wg_ref, bg_ref))
        bg_ = [lag[0]]
        for r in range(1, GROUPS):
            bg_.append(bg_[-1] + lag[r])
        tot = bg_[GROUPS - 1]
        pw = []
        for r in range(GROUPS):
            for s in range(r):
                pw.append(qg[r] * kg[s] * jnp.exp(bg_[r] - bg_[s]))
            pw.append(qg[r] * kg[r])
        scores = _dot(jnp.concatenate(pw, axis=0).astype(bf16), j_ref[...])
        project_next(1)
        og = []
        idx = 0
        for r in range(GROUPS):
            acc = None
            for s in range(r + 1):
                term = scores[SUBLANES * idx:SUBLANES * (idx + 1), :] * vg[s]
                acc = term if acc is None else acc + term
                idx += 1
            og.append(acc)
        o = jnp.concatenate(og, axis=0)

        def decayed(group):
            before = jnp.zeros((SUBLANES, GROUP_W), f32)
            after = jnp.zeros((SUBLANES, GROUP_W), f32)
            for s in range(1, group):
                before = before + jnp.where(sub8 % group >= s, pltpu.roll(tot, s, 0), 0.0)
                after = after + jnp.where(sub8 % group < group - s, pltpu.roll(tot, SUBLANES - s, 0), 0.0)
            qd = jnp.concatenate([qg[r] * jnp.exp(bg_[r] + before) for r in range(GROUPS)], axis=0)
            kd = jnp.concatenate([kg[r] * jnp.exp((tot - bg_[r]) + after) for r in range(GROUPS)], axis=0)
            return qd, kd

        attn = None
        for li, group in enumerate(LEVEL_GROUPS):
            qd, kd = decayed(group)
            s = _dot_nt(_stack_heads(qd).astype(bf16), kd.astype(bf16)) * lvl_ref[li]
            attn = s if attn is None else attn + s
        o = o + _unstack_heads(_dot(attn.astype(bf16), v.astype(bf16)), SUB)
        project_next(1)
        qd, kd = decayed(SUB // GLA_BLOCK)
        st = st_scr[...]
        o = o + _dot_nt(qd.astype(bf16), st.astype(bf16))
        total = jnp.sum(tot, axis=0, keepdims=True)
        upd = _dot_tn(v.astype(bf16), kd.astype(bf16))
        same_head = (lax.broadcasted_iota(jnp.int32, (GROUP_W, GROUP_W), 0) // HEAD_D
                     == lax.broadcasted_iota(jnp.int32, (GROUP_W, GROUP_W), 1) // HEAD_D)
        st_scr[...] = st * jnp.exp(total) + jnp.where(same_head, upd, 0.0)
        b_out = _gla_norm_gate(o, p_scr[rows, C_GZ:C_GZ + GROUP_W], gnorm_ref, j_ref)
        mix_scr[rows, GROUP_W:2 * GROUP_W] = b_out.astype(bf16)
        project_next(1)

        cos = cos_ref[rows, :]
        sin = sin_ref[rows, :]
        q01 = _rope(p_scr[rows, C_SQ:C_SQ + LANES], cos, sin) * QK_SCALE
        q23 = _rope(p_scr[rows, C_SQ + LANES:C_SQ + 2 * LANES], cos, sin) * QK_SCALE
        kr = _rope(p_scr[rows, C_SK:C_SK + SWA_KV_W], cos, sin)
        vx = p_scr[rows, C_SV:C_SV + SWA_KV_W]
        k_all = jnp.concatenate([kprev[...], kr], axis=0).astype(bf16)
        vt_new = vx.T
        vt_all = jnp.concatenate([vtprev[...], vt_new], axis=1).astype(bf16)
        first = jnp.where((t * nsub + j) == 0, 1, 0)
        sc = _dot_nt(k_all, _stack_swa_q(q01, q23).astype(bf16)) + swab_ref[first]
        lane_head = lax.broadcasted_iota(jnp.int32, (1, N_HEADS * SUB), 1) // SUB
        sink = jnp.full((1, N_HEADS * SUB), sinks_ref[0], f32)
        for hh in range(1, N_HEADS):
            sink = jnp.where(lane_head == hh, sinks_ref[hh], sink)
        project_next(1)
        mx = jnp.maximum(jnp.max(sc, axis=0, keepdims=True), sink)
        e = jnp.exp(sc - mx)
        den = jnp.sum(e, axis=0, keepdims=True) + jnp.exp(sink - mx)
        ot = _dot(vt_all, e.astype(bf16)) * (1.0 / den)
        oc = _unstack_swa(jnp.concatenate([ot[:, SUB * hh:SUB * (hh + 1)].T for hh in range(N_HEADS)], axis=0), SUB)
        mix_scr[rows, 2 * GROUP_W:3 * GROUP_W] = (oc * _silu(p_scr[rows, C_SZ:C_SZ + GROUP_W])).astype(bf16)
        kprev[...] = kr
        vprev[...] = vx
        vtprev[...] = vt_new
        project_next(1)

        qm = _stack_heads(p_scr[rows, C_MQ:C_MQ + GROUP_W] * QK_SCALE).astype(bf16)
        sm = _dot_nt(mkb[...], qm)
        mm = jnp.max(sm, axis=0, keepdims=True)
        em = jnp.exp(sm - mm)
        dm = jnp.sum(em, axis=0, keepdims=True)
        odt = _dot(mvtb[...], em.astype(bf16)) * (1.0 / dm)
        low = lax.broadcasted_iota(jnp.int32, (SUB, LANES), 1) < HEAD_D
        halves = []
        for pair in range(N_HEADS // 2):
            blk_rows = odt[LANES * pair:LANES * (pair + 1), :]
            even = blk_rows[:, SUB * (2 * pair):SUB * (2 * pair + 1)].T
            odd = blk_rows[:, SUB * (2 * pair + 1):SUB * (2 * pair + 2)].T
            halves.append(jnp.where(low, even, odd))
        od = jnp.concatenate(halves, axis=1)
        mix_scr[rows, 3 * GROUP_W:4 * GROUP_W] = (od * _silu(p_scr[rows, C_MZ:C_MZ + GROUP_W])).astype(bf16)

    for j in range(nsub):
        sub_tile(j)
    while pending:
        project_next(1)

    mo = _dot(mix_scr[...], woutb[...])
    y_ref[0] = x_prev[...] + _rmsnorm(mo, gpost_ref[...])
    x_prev[...] = x

    @pl.when(t == nt - 1)
    def _sequence_done():
        conv_out[0, 0:1, :] = cbuf[1:2, :]
        conv_out[0, 1:2, :] = cbuf[0:1, :]
        for src, dst in ((kprev, swak_out), (vprev, swav_out)):
            for r in range(GROUPS):
                kv_nat[pl.ds(r, SUBLANES, stride=GLA_BLOCK), :] = src[SUBLANES * r:SUBLANES * (r + 1), :]
            dst[0] = kv_nat[...].T
        st = st_scr[...]
        for hh in range(N_HEADS):
            gla_out[0, hh] = st[HEAD_D * hh:HEAD_D * (hh + 1), HEAD_D * hh:HEAD_D * (hh + 1)].T


def _layer_spec(shape, l):
    nd = len(shape)
    return pl.BlockSpec((None,) + tuple(shape), lambda s, _l=l, _nd=nd: (_l,) + (0,) * _nd)


def _const_spec(shape):
    nd = len(shape)
    return pl.BlockSpec(shape, lambda s, _nd=nd: (0,) * _nd)


def _prompt_layer(l, x, memp, cos, sin, gpre, gpost, wt, wout, wmem, convw, wg, bg, gnorm, sinks, jmat,
                  lvl, swab):
    B, L, _ = x.shape
    assert L % TL == 0
    nt = L // TL
    n_tiles = B * nt

    def projected(s):
        tile = jnp.minimum(s, n_tiles - 1)
        return tile // nt, tile % nt

    def mixed(s):
        tile = jnp.maximum(s - 1, 0)
        return tile // nt, tile % nt

    def per_sequence(shape):
        nd = len(shape) - 1
        return pl.BlockSpec(shape, lambda s, _nd=nd: (mixed(s)[0],) + (0,) * _nd)

    out_shape = (
        jax.ShapeDtypeStruct((B, L, D_MODEL), f32),
        jax.ShapeDtypeStruct((B, CONV_W - 1, GROUP_W), f32),
        jax.ShapeDtypeStruct((B, N_HEADS, HEAD_D, HEAD_D), f32),
        jax.ShapeDtypeStruct((B, SWA_KV_W, WINDOW), f32),
        jax.ShapeDtypeStruct((B, SWA_KV_W, WINDOW), f32),
        jax.ShapeDtypeStruct((B, GROUP_W, N_MEM), f32),
        jax.ShapeDtypeStruct((B, GROUP_W, N_MEM), f32),
    )
    in_specs = [
        pl.BlockSpec(memory_space=pltpu.SMEM),
        pl.BlockSpec((1, TL, D_MODEL), lambda s: projected(s) + (0,)),
        per_sequence((1, N_MEM, D_MODEL)),
        pl.BlockSpec((TL, LANES), lambda s: (mixed(s)[1], 0)),
        pl.BlockSpec((TL, LANES), lambda s: (mixed(s)[1], 0)),
        _layer_spec((1, D_MODEL), l), _layer_spec((1, D_MODEL), l),
        _layer_spec((IN_WIDTH, D_MODEL), l),
        _layer_spec((D_MODEL, D_MODEL), l), _layer_spec((D_MODEL, 2 * GROUP_W), l),
        _layer_spec((CONV_W, GROUP_W), l), _layer_spec((GATE_RANK, GROUP_W), l), _layer_spec((1, GROUP_W), l),
        _layer_spec((1, GROUP_W), l), _const_spec((GROUP_W, GROUP_W)),
        _const_spec((len(LEVEL_GROUPS), N_HEADS * SUB, SUB)), _const_spec((2, 2 * SUB, N_HEADS * SUB)),
    ]
    out_specs = (
        pl.BlockSpec((1, TL, D_MODEL), lambda s: mixed(s) + (0,)),
        per_sequence((1, CONV_W - 1, GROUP_W)),
        per_sequence((1, N_HEADS, HEAD_D, HEAD_D)),
        per_sequence((1, SWA_KV_W, WINDOW)),
        per_sequence((1, SWA_KV_W, WINDOW)),
        per_sequence((1, GROUP_W, N_MEM)),
        per_sequence((1, GROUP_W, N_MEM)),
    )
    scratch = [
        pltpu.VMEM((2, TL, NP), f32),
        pltpu.VMEM((TL, D_MODEL), f32),
        pltpu.VMEM((D_MODEL, D_MODEL), bf16),
        pltpu.VMEM((TL, D_MODEL), bf16),
        pltpu.VMEM((SUBLANES, GROUP_W), f32),
        pltpu.VMEM((GROUP_W, GROUP_W), f32),
        pltpu.VMEM((SUB, SWA_KV_W), f32),
        pltpu.VMEM((SUB, SWA_KV_W), f32),
        pltpu.VMEM((SWA_KV_W, SUB), f32),
        pltpu.VMEM((N_MEM, GROUP_W), bf16),
        pltpu.VMEM((GROUP_W, N_MEM), bf16),
        pltpu.VMEM((WINDOW, SWA_KV_W), f32),
    ]
    return pl.pallas_call(
        functools.partial(_prompt_layer_kernel, nt),
        grid=(n_tiles + 1,),
        in_specs=in_specs,
        out_specs=out_specs,
        out_shape=out_shape,
        scratch_shapes=scratch,
        compiler_params=pltpu.CompilerParams(dimension_semantics=("arbitrary",),
                                             vmem_limit_bytes=VMEM_LIMIT_BYTES),
        name="prompt_layer",
    )(sinks[l], x, memp, cos, sin, gpre, gpost, wt, wout, wmem, convw, wg, bg, gnorm, jmat, lvl, swab)


def _subtile_constants():
    row = np.arange(SUB)
    tok = (row % SUBLANES) * GLA_BLOCK + row // SUBLANES
    blk = row % SUBLANES
    levels = []
    for group in LEVEL_GROUPS:
        g = blk // group
        sel = (g[:, None] == g[None, :] + 1) & (g[:, None] % 2 == 1)
        levels.append(np.tile(sel, (N_HEADS, 1)))
    key_tok = np.concatenate([tok - SUB, tok])
    valid = (key_tok[None, :] <= tok[:, None]) & (key_tok[None, :] > tok[:, None] - WINDOW)
    bias = [np.where(valid, 0.0, NEG), np.where(valid & (key_tok[None, :] >= 0), 0.0, NEG)]
    bias = np.stack([np.tile(b, (N_HEADS, 1)).T for b in bias])
    return jnp.asarray(np.stack(levels), dtype=f32), jnp.asarray(bias, dtype=f32)


def _to_kernel_order(x, axis):
    shp = x.shape
    n = shp[axis]
    x = x.reshape(shp[:axis] + (n // SUB, SUBLANES, GLA_BLOCK) + shp[axis + 1:])
    return jnp.swapaxes(x, axis + 1, axis + 2).reshape(shp)


def _from_kernel_order(x, axis):
    shp = x.shape
    n = shp[axis]
    x = x.reshape(shp[:axis] + (n // SUB, GLA_BLOCK, SUBLANES) + shp[axis + 1:])
    return jnp.swapaxes(x, axis + 1, axis + 2).reshape(shp)


def _sample_kernel(seq_len, group, sinks_ref, x_hbm, cos_ref, sin_ref, convb_ref, sgla_ref, kc_ref, vc_ref,
                   mkc_ref, mvc_ref, gpre_ref, gpost_ref, wt_ref, wout_ref, convw_ref, wg_ref, bg_ref,
                   gnorm_ref, j_ref, place_ref,
                   y_hbm, conv_out, gla_out, swak_out, swav_out,
                   hs, p_scr, mix_scr, woutb, sem, mk_ring, mv_ring, ring_sem):
    l = pl.program_id(0)
    g_step = pl.program_id(1)
    n_layers = pl.num_programs(0)
    n_steps = pl.num_programs(1)
    n_tok = hs.shape[0]
    n = group * seq_len
    chunks = [slice(c * TL, (c + 1) * TL) for c in range(n_tok // TL)]

    step = l * n_steps + g_step
    total_steps = n_layers * n_steps

    def cache_copies(q):
        lq, gq = q // n_steps, q % n_steps
        slot = q % CACHE_RING
        return [pltpu.make_async_copy(hbm.at[lq, pl.ds(gq * group, group)], ring.at[slot], ring_sem.at[i, slot])
                for i, (hbm, ring) in enumerate(((mkc_ref, mk_ring), (mvc_ref, mv_ring)))]

    @pl.when(step == 0)
    def _prefill():
        for q in range(CACHE_RING - 1):
            for cp in cache_copies(q):
                cp.start()

    @pl.when(step + CACHE_RING - 1 < total_steps)
    def _prefetch():
        for cp in cache_copies(step + CACHE_RING - 1):
            cp.start()

    @pl.when((l == 0) & (g_step == 0))
    def _load():
        cp = pltpu.make_async_copy(x_hbm, hs, sem.at[0])
        cp.start()
        cp.wait()

    @pl.when(g_step == 0)
    def _project_all():
        woutb[...] = wout_ref[...].astype(bf16)
        for rows in chunks:
            h = _rmsnorm(hs[rows, :], gpre_ref[...]).astype(bf16)
            _project(h, wt_ref, p_scr, rows)

    rows = pl.ds(pl.multiple_of(g_step * n, n), n)
    row_t = lax.broadcasted_iota(jnp.int32, (n, GROUP_W), 0) % seq_len

    u = p_scr[rows, C_CC:C_CC + GROUP_W] * p_scr[rows, C_CX:C_CX + GROUP_W]
    hist = convb_ref[...]
    um1 = jnp.where(row_t >= 1, pltpu.roll(u, 1, 0), pltpu.roll(hist, n - 1, 0))
    um2 = jnp.where(row_t >= 2, pltpu.roll(u, 2, 0), hist)
    cy = convw_ref[0:1, :] * um2 + convw_ref[1:2, :] * um1 + convw_ref[2:3, :] * u
    a_out = p_scr[rows, C_CB:C_CB + GROUP_W] * cy * _silu(p_scr[rows, C_CZ:C_CZ + GROUP_W])
    mix_scr[rows, 0:GROUP_W] = a_out.astype(bf16)
    for g in range(group):
        conv_out[g] = u[g * seq_len + seq_len - (CONV_W - 1):(g + 1) * seq_len, :]

    qs = p_scr[rows, C_GQ:C_GQ + GROUP_W] * QK_SCALE
    k = p_scr[rows, C_GK:C_GK + GROUP_W]
    v = p_scr[rows, C_GV:C_GV + GROUP_W]
    la = _gate_log_decay(p_scr[rows, C_LR:C_LR + GATE_RANK], wg_ref, bg_ref)
    bc = _block_cumsum(la, seq_len)
    o = _dot((qs * k).astype(bf16), j_ref[...]) * v
    for d in range(1, seq_len):
        dec = jnp.exp(jnp.where(row_t >= d, bc - pltpu.roll(bc, d, 0), NEG))
        pw = qs * pltpu.roll(k, d, 0) * dec
        o = o + _dot(pw.astype(bf16), j_ref[...]) * pltpu.roll(v, d, 0)
    qd = qs * jnp.exp(bc)
    tot = [bc[(g + 1) * seq_len - 1:(g + 1) * seq_len, :] for g in range(group)]
    kd = k * jnp.exp(jnp.concatenate(
        [tot[g] - bc[g * seq_len:(g + 1) * seq_len, :] for g in range(group)], axis=0))
    low = lax.broadcasted_iota(jnp.int32, (seq_len, LANES), 1) < HEAD_D
    low_st = lax.broadcasted_iota(jnp.int32, (HEAD_D, LANES), 1) < HEAD_D
    o_inter = []
    for g in range(group):
        rs = slice(g * seq_len, (g + 1) * seq_len)
        alpha = jnp.exp(tot[g])
        per_pair = []
        for pair in range(N_HEADS // 2):
            ls = slice(pair * LANES, (pair + 1) * LANES)
            st0 = sgla_ref[g, pair]
            q_pair = qd[rs, ls]
            q2 = jnp.concatenate([jnp.where(low, q_pair, 0.0), jnp.where(low, 0.0, q_pair)], axis=0)
            o2 = _dot_nt(q2.astype(bf16), st0.astype(bf16))
            per_pair.append(jnp.concatenate([o2[0:seq_len], o2[seq_len:2 * seq_len]], axis=1))
            kd_pair = kd[rs, ls].astype(bf16)
            upd = [_dot_tn(v[rs, (2 * pair + hh) * HEAD_D:(2 * pair + hh + 1) * HEAD_D].astype(bf16), kd_pair)
                   for hh in range(2)]
            new_pair = st0 * alpha[:, ls] + jnp.where(low_st, upd[0], upd[1])
            gla_out[g, 2 * pair] = new_pair[:, 0:HEAD_D]
            gla_out[g, 2 * pair + 1] = new_pair[:, HEAD_D:LANES]
        o_inter.append(jnp.concatenate(per_pair, axis=1))
    o = o + jnp.concatenate(o_inter, axis=0)
    b_out = _gla_norm_gate(o, p_scr[rows, C_GZ:C_GZ + GROUP_W], gnorm_ref, j_ref)
    mix_scr[rows, GROUP_W:2 * GROUP_W] = b_out.astype(bf16)

    cos = cos_ref[...]
    sin = sin_ref[...]
    q01 = _rope(p_scr[rows, C_SQ:C_SQ + LANES], cos, sin) * QK_SCALE
    q23 = _rope(p_scr[rows, C_SQ + LANES:C_SQ + 2 * LANES], cos, sin) * QK_SCALE
    kr = _rope(p_scr[rows, C_SK:C_SK + SWA_KV_W], cos, sin)
    vx = p_scr[rows, C_SV:C_SV + SWA_KV_W]
    qm = p_scr[rows, C_MQ:C_MQ + GROUP_W] * QK_SCALE
    hq = N_HEADS * seq_len
    sink_seq = _sink_column([sinks_ref[l, hh] for hh in range(N_HEADS)], seq_len)
    sink = jnp.concatenate([sink_seq] * group, axis=0)
    hq_all = group * hq
    qrow = lax.broadcasted_iota(jnp.int32, (hq_all, WINDOW), 0) % seq_len
    ccol = lax.broadcasted_iota(jnp.int32, (hq_all, WINDOW), 1)
    cache_valid = ccol > qrow
    nrow = lax.broadcasted_iota(jnp.int32, (hq_all, seq_len), 0) % seq_len
    ncol = lax.broadcasted_iota(jnp.int32, (hq_all, seq_len), 1)
    new_valid = ncol <= nrow
    keep_old = lax.broadcasted_iota(jnp.int32, (SWA_KV_W, WINDOW), 1) < WINDOW - seq_len

    def exact_split(a):
        hi = a.astype(bf16)
        r1 = a - hi.astype(f32)
        mid = r1.astype(bf16)
        lo = (r1 - mid.astype(f32)).astype(bf16)
        return jnp.concatenate([hi, mid, lo], axis=0)

    def shifted_cache(old_t, new_rows):
        placed = _dot_tn(exact_split(new_rows), place_ref[...])
        return jnp.where(keep_old, pltpu.roll(old_t, WINDOW - seq_len, 1), placed)

    seqs = [slice(g * seq_len, (g + 1) * seq_len) for g in range(group)]
    stk = [slice(g * hq, (g + 1) * hq) for g in range(group)]
    qst = [_stack_swa_q(q01[rs], q23[rs]).astype(bf16) for rs in seqs]
    s_c = jnp.concatenate([_dot(qst[g], kc_ref[g].astype(bf16)) for g in range(group)], axis=0)
    s_n = jnp.concatenate([_dot_nt(qst[g], kr[seqs[g]].astype(bf16)) for g in range(group)], axis=0)
    s_c = jnp.where(cache_valid, s_c, NEG)
    s_n = jnp.where(new_valid, s_n, NEG)
    mx = jnp.maximum(jnp.maximum(jnp.max(s_c, axis=1, keepdims=True), jnp.max(s_n, axis=1, keepdims=True)), sink)
    e_c = jnp.exp(s_c - mx)
    e_n = jnp.exp(s_n - mx)
    den = jnp.sum(e_c, axis=1, keepdims=True) + jnp.sum(e_n, axis=1, keepdims=True) + jnp.exp(sink - mx)
    e_c = e_c.astype(bf16)
    e_n = e_n.astype(bf16)
    ov = jnp.concatenate([_dot_nt(e_c[stk[g]], vc_ref[g].astype(bf16)) + _dot(e_n[stk[g]], vx[seqs[g]].astype(bf16))
                          for g in range(group)], axis=0) * (1.0 / den)
    oc = jnp.concatenate([_unstack_swa(ov[stk[g]], seq_len) for g in range(group)], axis=0)
    for g in range(group):
        swak_out[g] = shifted_cache(kc_ref[g], kr[seqs[g]])
        swav_out[g] = shifted_cache(vc_ref[g], vx[seqs[g]])

    for cp in cache_copies(step):
        cp.wait()
    slot = step % CACHE_RING
    sm = jnp.concatenate([_dot(_stack_heads(qm[seqs[g]]).astype(bf16), mk_ring[slot, g].astype(bf16))
                          for g in range(group)], axis=0)
    mm = jnp.max(sm, axis=1, keepdims=True)
    em = jnp.exp(sm - mm)
    dm = jnp.sum(em, axis=1, keepdims=True)
    em = em.astype(bf16)
    odv = jnp.concatenate([_dot_nt(em[stk[g]], mv_ring[slot, g].astype(bf16)) for g in range(group)], axis=0) * (1.0 / dm)
    od = jnp.concatenate([_unstack_heads(odv[stk[g]], seq_len) for g in range(group)], axis=0)
    mix_scr[rows, 2 * GROUP_W:3 * GROUP_W] = (oc * _silu(p_scr[rows, C_SZ:C_SZ + GROUP_W])).astype(bf16)
    mix_scr[rows, 3 * GROUP_W:4 * GROUP_W] = (od * _silu(p_scr[rows, C_MZ:C_MZ + GROUP_W])).astype(bf16)

    @pl.when(g_step == n_steps - 1)
    def _residual():
        for rws in chunks:
            mo = _dot(mix_scr[rws, :], woutb[...])
            hs[rws, :] = hs[rws, :] + _rmsnorm(mo, gpost_ref[...])

    @pl.when((l == n_layers - 1) & (g_step == n_steps - 1))
    def _store():
        cp = pltpu.make_async_copy(hs, y_hbm, sem.at[0])
        cp.start()
        cp.wait()


def _sample_layers(x, cos, sin, convb, sgla, kc, vc, mkc, mvc, gpre, gpost, wt, wout, convw, wg, bg, gnorm,
                   sinks, jmat, place, seq_len, group):
    depth, nseq = sgla.shape[0], sgla.shape[1]
    n_tok = nseq * seq_len
    assert nseq % group == 0 and seq_len == SUBLANES and n_tok % TL == 0
    assert depth * (nseq // group) >= CACHE_RING - 1
    n = group * seq_len

    def per_layer(shape, single_buffer=False):
        nd = len(shape)
        kw = dict(pipeline_mode=pl.Buffered(1)) if single_buffer else {}
        return pl.BlockSpec((None,) + tuple(shape), lambda l, g, _nd=nd: (l,) + (0,) * _nd, **kw)

    def per_group(shape):
        nd = len(shape) - 1
        return pl.BlockSpec((None,) + tuple(shape), lambda l, g, _nd=nd: (l, g) + (0,) * _nd)

    def const(shape):
        nd = len(shape)
        return pl.BlockSpec(shape, lambda l, g, _nd=nd: (0,) * _nd)

    in_specs = [
        pl.BlockSpec(memory_space=pltpu.SMEM),
        pl.BlockSpec(memory_space=pl.ANY),
        const((n, LANES)), const((n, LANES)),
        per_group((n, GROUP_W)),
        per_group((group, N_HEADS // 2, HEAD_D, LANES)),
        per_group((group, SWA_KV_W, WINDOW)), per_group((group, SWA_KV_W, WINDOW)),
        pl.BlockSpec(memory_space=pl.ANY), pl.BlockSpec(memory_space=pl.ANY),
        per_layer((1, D_MODEL)), per_layer((1, D_MODEL)),
        per_layer((IN_WIDTH, D_MODEL), True),
        per_layer((D_MODEL, D_MODEL), True),
        per_layer((CONV_W, GROUP_W)), per_layer((GATE_RANK, GROUP_W)), per_layer((1, GROUP_W)), per_layer((1, GROUP_W)),
        const((GROUP_W, GROUP_W)), const((3 * seq_len, WINDOW)),
    ]
    out_shape = (
        jax.ShapeDtypeStruct((n_tok, D_MODEL), f32),
        jax.ShapeDtypeStruct((depth, nseq, CONV_W - 1, GROUP_W), f32),
        jax.ShapeDtypeStruct((depth, nseq, N_HEADS, HEAD_D, HEAD_D), f32),
        jax.ShapeDtypeStruct((depth, nseq, SWA_KV_W, WINDOW), f32),
        jax.ShapeDtypeStruct((depth, nseq, SWA_KV_W, WINDOW), f32),
    )
    out_specs = (
        pl.BlockSpec(memory_space=pl.ANY),
        per_group((group, CONV_W - 1, GROUP_W)),
        per_group((group, N_HEADS, HEAD_D, HEAD_D)),
        per_group((group, SWA_KV_W, WINDOW)), per_group((group, SWA_KV_W, WINDOW)),
    )
    scratch = [
        pltpu.VMEM((n_tok, D_MODEL), f32),
        pltpu.VMEM((n_tok, NP), f32),
        pltpu.VMEM((n_tok, D_MODEL), bf16),
        pltpu.VMEM((D_MODEL, D_MODEL), bf16),
        pltpu.SemaphoreType.DMA((1,)),
        pltpu.VMEM((CACHE_RING, group, GROUP_W, N_MEM), f32),
        pltpu.VMEM((CACHE_RING, group, GROUP_W, N_MEM), f32),
        pltpu.SemaphoreType.DMA((2, CACHE_RING)),
    ]
    return pl.pallas_call(
        functools.partial(_sample_kernel, seq_len, group),
        grid=(depth, nseq // group),
        in_specs=in_specs,
        out_specs=out_specs,
        out_shape=out_shape,
        scratch_shapes=scratch,
        compiler_params=pltpu.CompilerParams(dimension_semantics=("arbitrary", "arbitrary"),
                                             vmem_limit_bytes=VMEM_LIMIT_BYTES),
        name="sample_layers",
    )(sinks, x, cos, sin, convb, sgla, kc, vc, mkc, mvc, gpre, gpost, wt, wout, convw, wg, bg, gnorm,
      jmat, place)


def _rope_tables(pos):
    half = HEAD_D // 2
    inv = np.power(ROPE_THETA, -np.arange(half, dtype=np.float64) / half)
    ang = np.asarray(pos, np.float64)[:, None] * inv[None, :]
    cos = np.tile(np.cos(ang), (1, LANES // half))
    sin = np.sin(ang)
    sin_signed = np.tile(np.concatenate([-sin, sin], axis=1), (1, LANES // HEAD_D))
    return cos.astype(np.float32), sin_signed.astype(np.float32)


def _rows_to_kernel_order(a):
    n, w = a.shape
    return a.reshape(n // SUB, SUBLANES, GLA_BLOCK, w).swapaxes(1, 2).reshape(n, w)


def _feature_major(cache):
    d, s, p, h, e = cache.shape
    return jnp.transpose(cache, (0, 1, 3, 4, 2)).reshape(d, s, h * e, p)


def _pair_states(state):
    dd, s, h, e, _ = state.shape
    a = jnp.swapaxes(state, -1, -2).reshape(dd, s, h // 2, 2, e, e)
    return jnp.transpose(a, (0, 1, 2, 4, 3, 5)).reshape(dd, s, h // 2, e, 2 * e)


def _position_major(cache_t, heads):
    d, s, he, p = cache_t.shape
    return jnp.transpose(cache_t.reshape(d, s, heads, he // heads, p), (0, 1, 4, 2, 3))


def kernel(x_prompt, x_sample, state_conv, state_gla, cache_swa_k, cache_swa_v, cache_mem_k, cache_mem_v,
           mem_prompt, norm_pre, norm_post, w_in, conv_w, gla_w_gate, gla_b_gate, gla_norm, swa_sinks,
           w_mem_kv, w_out):
    depth = w_in.shape[0]
    B, L, _ = x_prompt.shape
    nseq, seq_len, _ = x_sample.shape
    group = 8

    wt = jnp.swapaxes(w_in, 1, 2).astype(bf16)
    wout = w_out
    wmem = w_mem_kv
    wg = gla_w_gate.astype(bf16)
    head_id = np.arange(GROUP_W) // HEAD_D
    jmat = jnp.asarray(head_id[:, None] == head_id[None, :], dtype=bf16)
    lvl, swab = _subtile_constants()
    place = np.zeros((3 * seq_len, WINDOW), np.float32)
    for piece in range(3):
        place[piece * seq_len + np.arange(seq_len), WINDOW - seq_len + np.arange(seq_len)] = 1.0
    place = jnp.asarray(place, dtype=bf16)
    sinks = swa_sinks.astype(f32)
    gpre = norm_pre[:, None, :]
    gpost = norm_post[:, None, :]
    bg = gla_b_gate[:, None, :]
    gn = gla_norm[:, None, :]

    cos_p, sin_p = (jnp.asarray(_rows_to_kernel_order(a)) for a in _rope_tables(np.arange(L)))
    cos_s, sin_s = (jnp.asarray(np.tile(a, (group, 1))) for a in _rope_tables(PAST_LEN + np.arange(seq_len)))

    convb = jnp.pad(state_conv, ((0, 0), (0, 0), (0, seq_len - (CONV_W - 1)), (0, 0)))
    convb = convb.reshape(depth, nseq * seq_len, GROUP_W)
    ys, conv_s, gla_s, swak_s, swav_s = _sample_layers(
        x_sample.reshape(nseq * seq_len, D_MODEL), cos_s, sin_s, convb, _pair_states(state_gla),
        _feature_major(cache_swa_k), _feature_major(cache_swa_v),
        _feature_major(cache_mem_k), _feature_major(cache_mem_v),
        gpre, gpost, wt, wout, conv_w, wg, bg, gn, sinks, jmat, place, seq_len, group)

    hp = _to_kernel_order(x_prompt, 1)
    outs_p = [[] for _ in range(6)]
    for l in range(depth):
        res = _prompt_layer(l, hp, mem_prompt, cos_p, sin_p, gpre, gpost, wt, wout, wmem, conv_w, wg, bg, gn,
                            sinks, jmat, lvl, swab)
        hp = res[0]
        for i in range(6):
            outs_p[i].append(res[i + 1])

    return (_from_kernel_order(hp, 1), ys.reshape(nseq, seq_len, D_MODEL),
            jnp.stack(outs_p[0]), jnp.stack(outs_p[1]),
            _position_major(jnp.stack(outs_p[2]), 2), _position_major(jnp.stack(outs_p[3]), 2),
            _position_major(jnp.stack(outs_p[4]), N_HEADS), _position_major(jnp.stack(outs_p[5]), N_HEADS),
            conv_s, jnp.swapaxes(gla_s, -1, -2), _position_major(swak_s, 2), _position_major(swav_s, 2))
```
